```python
import math
import jax, jax.numpy as jnp
from jax import lax
import numpy as np

D_MODEL = 1024
BATCH = 8
SEQ = 2048
DEPTH = 2

GRID_W = 64
CTX_LEN = 256
EPS = 1e-6
ROPE_BASE = 10000.0
Q_BLOCK = 128

MLA_HEADS = 8
MLA_Q_LORA = 384
MLA_KV_LORA = 256
MLA_NOPE = 64
MLA_ROPE = 32
MLA_V = 64
MLA_SCALE = (MLA_NOPE + MLA_ROPE) ** -0.5

SSD_HEADS = 8
SSD_HEADDIM = 64
SSD_INNER = SSD_HEADS * SSD_HEADDIM
SSD_GROUPS = 2
SSD_STATE = 64
SSD_CONV = 5
SSD_CHUNK = 128
SSD_CONV_DIM = SSD_INNER + 2 * SSD_GROUPS * SSD_STATE

SWA_HEADS = 8
SWA_KV_HEADS = 2
SWA_GROUP = SWA_HEADS // SWA_KV_HEADS
SWA_HEAD_DIM = 64
SWA_WINDOW = 128
SWA_BLOCK = 128
SWA_SCALE = SWA_HEAD_DIM ** -0.5

HY_WIDTH = 512
HY_ORDER = 2
HY_SHORT = 3
HY_BANDS = 16
HY_EMB = 1 + 2 * HY_BANDS
HY_HIDDEN = 64
HY_DECAY_TARGET = 1e-2
HY_FAST_DECAY = 0.3
HY_SLOW_DECAY = 1.5

N_BRANCH = 4
BRANCH_W = 512
FFN_HIDDEN = ((8 * D_MODEL // 3 + 255) // 256) * 256

IN_SPLITS = (MLA_Q_LORA, MLA_KV_LORA, MLA_ROPE, SSD_INNER, SSD_CONV_DIM, 2 * SSD_HEADS,
             SWA_HEADS * SWA_HEAD_DIM, 2 * SWA_KV_HEADS * SWA_HEAD_DIM, 3 * HY_WIDTH,
             N_BRANCH * D_MODEL)
IN_COLS = sum(IN_SPLITS)

kernel_name = 'hybrid_mla_ssd_swa_hyena_dit'


def rmsnorm(x, g):
    xf = x.astype(jnp.float32)
    y = xf * lax.rsqrt(jnp.mean(xf * xf, axis=-1, keepdims=True) + EPS)
    return (y * g.astype(jnp.float32)).astype(x.dtype)


def split_cols(x, sizes):
    return jnp.split(x, np.cumsum(sizes)[:-1].tolist(), axis=-1)


def modulate(h, shift, scale):
    return h * (1 + scale) + shift


def rope_tables(rows, dim):
    row = jnp.repeat(jnp.arange(rows, dtype=jnp.float32), GRID_W)
    col = jnp.tile(jnp.arange(GRID_W, dtype=jnp.float32), rows)
    n_freq = dim // 4
    inv = ROPE_BASE ** (-jnp.arange(n_freq, dtype=jnp.float32) / n_freq)
    ang = jnp.concatenate([row[:, None] * inv, col[:, None] * inv], axis=-1)
    return jnp.cos(ang), jnp.sin(ang)


def apply_rope(x, cos, sin):
    x1, x2 = jnp.split(x, 2, axis=-1)
    c = cos[:, None, :].astype(x.dtype)
    s = sin[:, None, :].astype(x.dtype)
    return jnp.concatenate([x1 * c - x2 * s, x1 * s + x2 * c], axis=-1)


def depthwise_conv(x, w, b):
    k = w.shape[0]
    y = lax.conv_general_dilated(x, w[:, None, :].astype(x.dtype), window_strides=(1,),
                                 padding=[(k // 2, k // 2)], dimension_numbers=('NWC', 'WIO', 'NWC'),
                                 feature_group_count=x.shape[-1])
    return y + b.astype(x.dtype)


def mla_queries(qa, q_norm, w_uq, rope):
    bsz, n = qa.shape[:2]
    q = (rmsnorm(qa, q_norm) @ w_uq).reshape(bsz, n, MLA_HEADS, MLA_NOPE + MLA_ROPE)
    q_nope, q_rope = split_cols(q, (MLA_NOPE, MLA_ROPE))
    if rope is not None:
        q_rope = apply_rope(q_rope, *rope)
    return jnp.concatenate([q_nope, q_rope], axis=-1)


def mla_keys_values(ckv, kr, kv_norm, w_ukv, rope):
    bsz, n = ckv.shape[:2]
    kv = (rmsnorm(ckv, kv_norm) @ w_ukv).reshape(bsz, n, MLA_HEADS, MLA_NOPE + MLA_V)
    k_nope, v = split_cols(kv, (MLA_NOPE, MLA_V))
    k_rope = kr[:, :, None, :]
    if rope is not None:
        k_rope = apply_rope(k_rope, *rope)
    k_rope = jnp.broadcast_to(k_rope, (bsz, n, MLA_HEADS, MLA_ROPE))
    return jnp.concatenate([k_nope, k_rope], axis=-1), v


def block_dense_attention(q, k, v, scale):
    bsz, n, h, dq = q.shape
    nb = n // Q_BLOCK
    qb = q.reshape(bsz, nb, Q_BLOCK, h, dq).transpose(1, 0, 2, 3, 4)

    def one_block(qi):
        s = jnp.einsum('bqhd,bkhd->bhqk', qi, k).astype(jnp.float32) * scale
        p = jax.nn.softmax(s, axis=-1).astype(v.dtype)
        return jnp.einsum('bhqk,bkhd->bqhd', p, v)

    o = lax.map(one_block, qb)
    return o.transpose(1, 0, 2, 3, 4).reshape(bsz, n, h * v.shape[-1])


def swa_queries(swq, rope):
    bsz, n = swq.shape[:2]
    q = swq.reshape(bsz, n, SWA_HEADS, SWA_HEAD_DIM)
    if rope is not None:
        q = apply_rope(q, *rope)
    return q.reshape(bsz, n, SWA_KV_HEADS, SWA_GROUP, SWA_HEAD_DIM)


def swa_keys_values(swkv, rope):
    bsz, n = swkv.shape[:2]
    k, v = split_cols(swkv, (SWA_KV_HEADS * SWA_HEAD_DIM, SWA_KV_HEADS * SWA_HEAD_DIM))
    k = k.reshape(bsz, n, SWA_KV_HEADS, SWA_HEAD_DIM)
    v = v.reshape(bsz, n, SWA_KV_HEADS, SWA_HEAD_DIM)
    if rope is not None:
        k = apply_rope(k, *rope)
    return k, v


def sink_attention(q, k, v, sink, mask):
    s = jnp.einsum('bqkgd,bskd->bkgqs', q, k).astype(jnp.float32) * SWA_SCALE
    if mask is not None:
        s = jnp.where(mask, s, -jnp.inf)
    snk = sink.astype(jnp.float32).reshape(1, SWA_KV_HEADS, SWA_GROUP, 1, 1)
    snk = jnp.broadcast_to(snk, s.shape[:-1] + (1,))
    p = jax.nn.softmax(jnp.concatenate([s, snk], axis=-1), axis=-1)[..., :-1]
    return jnp.einsum('bkgqs,bskd->bqkgd', p.astype(v.dtype), v)


def swa_banded(q, k, v, k_ctx, v_ctx, sink):
    bsz, n = q.shape[:2]
    nb = n // SWA_BLOCK
    pad = ((0, 0), (SWA_BLOCK, SWA_BLOCK), (0, 0), (0, 0))
    kp = jnp.pad(k, pad).reshape(bsz, nb + 2, SWA_BLOCK, SWA_KV_HEADS, SWA_HEAD_DIM)
    vp = jnp.pad(v, pad).reshape(bsz, nb + 2, SWA_BLOCK, SWA_KV_HEADS, SWA_HEAD_DIM)

    def band(t):
        return jnp.concatenate([t[:, :-2], t[:, 1:-1], t[:, 2:]], axis=2).transpose(1, 0, 2, 3, 4)

    qb = q.reshape(bsz, nb, SWA_BLOCK, SWA_KV_HEADS, SWA_GROUP, SWA_HEAD_DIM).transpose(1, 0, 2, 3, 4, 5)
    q_off = jnp.arange(SWA_BLOCK)
    k_off = jnp.arange(3 * SWA_BLOCK) - SWA_BLOCK
    ctx_mask = jnp.ones((SWA_BLOCK, k_ctx.shape[1]), dtype=bool)

    def one_block(args):
        qi, ki, vi, bi = args
        q_pos = bi * SWA_BLOCK + q_off
        k_pos = bi * SWA_BLOCK + k_off
        in_band = ((jnp.abs(q_pos[:, None] - k_pos[None, :]) <= SWA_WINDOW)
                   & (k_pos >= 0)[None, :] & (k_pos < n)[None, :])
        mask = jnp.concatenate([in_band, ctx_mask], axis=1)
        return sink_attention(qi, jnp.concatenate([ki, k_ctx], axis=1),
                              jnp.concatenate([vi, v_ctx], axis=1), sink, mask)

    o = lax.map(one_block, (qb, band(kp), band(vp), jnp.arange(nb)))
    return o.transpose(1, 0, 2, 3, 4, 5).reshape(bsz, n, SWA_HEADS * SWA_HEAD_DIM)


def ssd_inputs(xbc, dt_raw, conv_w, conv_b, dt_bias):
    bsz, n = xbc.shape[:2]
    xbc = jax.nn.silu(depthwise_conv(xbc, conv_w, conv_b)).astype(jnp.float32)
    xs, bm, cm = split_cols(xbc, (SSD_INNER, SSD_GROUPS * SSD_STATE, SSD_GROUPS * SSD_STATE))
    rep = SSD_HEADS // SSD_GROUPS
    xs = xs.reshape(bsz, n, SSD_HEADS, SSD_HEADDIM)
    bm = jnp.repeat(bm.reshape(bsz, n, SSD_GROUPS, SSD_STATE), rep, axis=2)
    cm = jnp.repeat(cm.reshape(bsz, n, SSD_GROUPS, SSD_STATE), rep, axis=2)
    dt = jax.nn.softplus(dt_raw.astype(jnp.float32).reshape(bsz, n, 2, SSD_HEADS)
                         + dt_bias.astype(jnp.float32))
    return xs, bm, cm, dt


def ssd_chunked(x, dt, a, bm, cm, h0, want_y):
    bsz, n, h, p = x.shape
    nc = n // SSD_CHUNK
    xdt = (x * dt[..., None]).reshape(bsz, nc, SSD_CHUNK, h, p)
    bc = bm.reshape(bsz, nc, SSD_CHUNK, h, SSD_STATE)
    cc = cm.reshape(bsz, nc, SSD_CHUNK, h, SSD_STATE)
    cs = jnp.cumsum((dt * a).reshape(bsz, nc, SSD_CHUNK, h).transpose(0, 1, 3, 2), axis=-1)
    decay_end = jnp.exp(cs[..., -1:] - cs)
    chunk_states = jnp.einsum('bcjhn,bchj,bcjhp->bchpn', bc, decay_end, xdt)
    chunk_decay = jnp.exp(cs[..., -1])

    def step(state, inp):
        dec, st = inp
        return dec[..., None, None] * state + st, state

    h_last, h_in = lax.scan(step, h0, (chunk_decay.transpose(1, 0, 2), chunk_states.transpose(1, 0, 2, 3, 4)))
    if not want_y:
        return None, h_last
    h_in = h_in.transpose(1, 0, 2, 3, 4)
    seg = cs[..., :, None] - cs[..., None, :]
    lower = jnp.tril(jnp.ones((SSD_CHUNK, SSD_CHUNK), dtype=bool))
    lmat = jnp.exp(jnp.where(lower, seg, -jnp.inf))
    scores = jnp.einsum('bcihn,bcjhn->bchij', cc, bc) * lmat
    y = (jnp.einsum('bchij,bcjhp->bcihp', scores, xdt)
         + jnp.einsum('bcihn,bchpn,bchi->bcihp', cc, h_in, jnp.exp(cs)))
    return y.reshape(bsz, n, h, p), h_last


def ssd_bidirectional(xs, bm, cm, dt, a, h0_f, h0_b, want_y):
    y_f, h_f = ssd_chunked(xs, dt[:, :, 0], a[0], bm, cm, h0_f, want_y)
    y_b, h_b = ssd_chunked(xs[:, ::-1], dt[:, ::-1, 1], a[1], bm[:, ::-1], cm[:, ::-1], h0_b, want_y)
    y = y_f + y_b[:, ::-1] if want_y else None
    return y, h_f, h_b


def ssd_output(y, xs, z, d_skip, norm_g):
    bsz, n = z.shape[:2]
    y = y + d_skip.astype(jnp.float32)[:, None] * xs
    yg = (y.reshape(bsz, n, SSD_INNER) * jax.nn.silu(z.astype(jnp.float32))).reshape(
        bsz, n, SSD_GROUPS, SSD_INNER // SSD_GROUPS)
    yg = yg * lax.rsqrt(jnp.mean(yg * yg, axis=-1, keepdims=True) + EPS)
    return (yg.reshape(bsz, n, SSD_INNER) * norm_g.astype(jnp.float32)).astype(z.dtype)


def hyena_filter_spectra(n, w1, b1, w2, b2, w3, freq):
    f32 = jnp.float32
    t = jnp.arange(n, dtype=f32)
    tn = t / n
    bands = jnp.linspace(1e-4, HY_BANDS - 1, HY_BANDS, dtype=f32)
    ang = 2 * math.pi * t[:, None] * bands[None, :] / n
    feat = jnp.concatenate([tn[:, None], jnp.cos(ang), -jnp.sin(ang)], axis=-1)
    fr = freq.astype(f32)
    h = jnp.sin(fr * (feat @ w1.astype(f32) + b1.astype(f32)))
    h = jnp.sin(fr * (h @ w2.astype(f32) + b2.astype(f32)))
    h = (h @ w3.astype(f32)).reshape(n, HY_ORDER, 2, HY_WIDTH)
    deltas = jnp.abs(jnp.linspace(math.log(HY_DECAY_TARGET) / HY_SLOW_DECAY,
                                  math.log(HY_DECAY_TARGET) / HY_FAST_DECAY, HY_WIDTH, dtype=f32))
    h = h * jnp.exp(-tn[:, None] * deltas[None, :])[:, None, None, :]
    h = h / jnp.sum(jnp.abs(h), axis=(0, 2), keepdims=True)
    k = jnp.concatenate([h[:, :, 0], jnp.zeros((1, HY_ORDER, HY_WIDTH), f32), h[:0:-1, :, 1]], axis=0)
    return jnp.fft.rfft(k, axis=0)


def fft_long_conv(u, spectrum, d):
    n = u.shape[1]
    uf = jnp.fft.rfft(u, n=2 * n, axis=1)
    y = jnp.fft.irfft(uf * spectrum[None], n=2 * n, axis=1)[:, :n]
    return y + u * d[None, None, :]


def hyena(u3, conv_w, conv_b, w1, b1, w2, b2, w3, freq, d):
    n = u3.shape[1]
    out_dtype = u3.dtype
    u3 = depthwise_conv(u3, conv_w, conv_b).astype(jnp.float32)
    v, x1, x2 = jnp.split(u3, 3, axis=-1)
    spec = hyena_filter_spectra(n, w1, b1, w2, b2, w3, freq)
    d = d.astype(jnp.float32)
    z = x1 * fft_long_conv(v, spec[:, 0], d[0])
    z = x2 * fft_long_conv(z, spec[:, 1], d[1])
    return z.astype(out_dtype)


def merge_branches(branches, gate_cols, w_branch, w_out):
    bsz, n = gate_cols.shape[:2]
    yb = jnp.stack(branches, axis=2)
    proj = jnp.einsum('btim,imd->btid', yb, w_branch)
    g = jax.nn.sigmoid(gate_cols.reshape(bsz, n, N_BRANCH, D_MODEL))
    return jnp.sum(g * proj, axis=2) @ w_out


def swiglu(h, w_in, w_out):
    g, u = jnp.split(h @ w_in, 2, axis=-1)
    return (jax.nn.silu(g) * u) @ w_out


def token_mixers(a_lat, a_ctx, ctx_out, rope_mla, rope_swa, w_in, mla_q_norm, mla_w_uq, mla_kv_norm,
                 mla_w_ukv, ssd_conv_w, ssd_conv_b, ssd_dt_bias, ssd_a_log, ssd_d, ssd_norm, swa_sink,
                 hy_conv_w, hy_conv_b, hy_w1, hy_b1, hy_w2, hy_b2, hy_w3, hy_freq, hy_d, w_branch, w_out):
    bsz, n_c = a_ctx.shape[:2]
    qa_l, ckv_l, kr_l, z_l, xbc_l, dt_l, swq_l, swkv_l, hy_l, gate_l = split_cols(a_lat @ w_in, IN_SPLITS)
    qa_c, ckv_c, kr_c, z_c, xbc_c, dt_c, swq_c, swkv_c, hy_c, gate_c = split_cols(a_ctx @ w_in, IN_SPLITS)

    k_c, v_c = mla_keys_values(ckv_c, kr_c, mla_kv_norm, mla_w_ukv, None)
    k_l, v_l = mla_keys_values(ckv_l, kr_l, mla_kv_norm, mla_w_ukv, rope_mla)
    q_l = mla_queries(qa_l, mla_q_norm, mla_w_uq, rope_mla)
    mla_l = block_dense_attention(q_l, jnp.concatenate([k_l, k_c], axis=1),
                                  jnp.concatenate([v_l, v_c], axis=1), MLA_SCALE)

    a = -jnp.exp(ssd_a_log.astype(jnp.float32))
    h0 = jnp.zeros((bsz, SSD_HEADS, SSD_HEADDIM, SSD_STATE), jnp.float32)
    xs_c, bm_c, cm_c, dtv_c = ssd_inputs(xbc_c, dt_c, ssd_conv_w, ssd_conv_b, ssd_dt_bias)
    xs_l, bm_l, cm_l, dtv_l = ssd_inputs(xbc_l, dt_l, ssd_conv_w, ssd_conv_b, ssd_dt_bias)
    y_c, h_f, h_b = ssd_bidirectional(xs_c, bm_c, cm_c, dtv_c, a, h0, h0, ctx_out)
    y_l, _, _ = ssd_bidirectional(xs_l, bm_l, cm_l, dtv_l, a, h_f, h_b, True)
    ssd_l = ssd_output(y_l, xs_l, z_l, ssd_d, ssd_norm)

    ksw_c, vsw_c = swa_keys_values(swkv_c, None)
    ksw_l, vsw_l = swa_keys_values(swkv_l, rope_swa)
    swa_l = swa_banded(swa_queries(swq_l, rope_swa), ksw_l, vsw_l, ksw_c, vsw_c, swa_sink)

    hy_out_l = hyena(hy_l, hy_conv_w, hy_conv_b, hy_w1, hy_b1, hy_w2, hy_b2, hy_w3, hy_freq, hy_d)

    out_l = merge_branches([mla_l, ssd_l, swa_l, hy_out_l], gate_l, w_branch, w_out)
    if not ctx_out:
        return out_l, None

    mla_c = block_dense_attention(mla_queries(qa_c, mla_q_norm, mla_w_uq, None), k_c, v_c, MLA_SCALE)
    ssd_c = ssd_output(y_c, xs_c, z_c, ssd_d, ssd_norm)
    swa_c = sink_attention(swa_queries(swq_c, None), ksw_c, vsw_c, swa_sink, None).reshape(
        bsz, n_c, SWA_HEADS * SWA_HEAD_DIM)
    hy_out_c = hyena(hy_c, hy_conv_w, hy_conv_b, hy_w1, hy_b1, hy_w2, hy_b2, hy_w3, hy_freq, hy_d)
    out_c = merge_branches([mla_c, ssd_c, swa_c, hy_out_c], gate_c, w_branch, w_out)
    return out_l, out_c


def setup_inputs(seed: int = 0) -> dict:
    key = jax.random.key(seed)
    ks = jax.random.split(key, 32)
    f32 = jnp.float32

    def nrm(i, shape, scale):
        return jax.random.normal(ks[i], shape, f32) * scale

    L = DEPTH
    dt0 = jnp.exp(jax.random.uniform(ks[14], (L, 2, SSD_HEADS), f32, math.log(1e-3), math.log(1e-1)))
    return {
        'x': nrm(0, (BATCH, SEQ, D_MODEL), 1.0),
        'c': nrm(1, (BATCH, D_MODEL), 1.0),
        'ctx': nrm(2, (BATCH, CTX_LEN, D_MODEL), 1.0),
        'c_ctx': nrm(3, (D_MODEL,), 1.0),
        'ada_w': nrm(4, (L, D_MODEL, 6 * D_MODEL), 0.5 * D_MODEL ** -0.5),
        'ada_b': nrm(5, (L, 6 * D_MODEL), 0.02),
        'norm_g': 1.0 + nrm(6, (L, 4, D_MODEL), 0.05),
        'w_in': nrm(7, (L, D_MODEL, IN_COLS), D_MODEL ** -0.5),
        'mla_q_norm': 1.0 + nrm(8, (L, MLA_Q_LORA), 0.05),
        'mla_w_uq': nrm(9, (L, MLA_Q_LORA, MLA_HEADS * (MLA_NOPE + MLA_ROPE)), MLA_Q_LORA ** -0.5),
        'mla_kv_norm': 1.0 + nrm(10, (L, MLA_KV_LORA), 0.05),
        'mla_w_ukv': nrm(11, (L, MLA_KV_LORA, MLA_HEADS * (MLA_NOPE + MLA_V)), MLA_KV_LORA ** -0.5),
        'ssd_conv_w': nrm(12, (L, SSD_CONV, SSD_CONV_DIM), SSD_CONV ** -0.5),
        'ssd_conv_b': nrm(13, (L, SSD_CONV_DIM), 0.02),
        'ssd_dt_bias': dt0 + jnp.log(-jnp.expm1(-dt0)),
        'ssd_a_log': jnp.log(jax.random.uniform(ks[15], (L, 2, SSD_HEADS), f32, 1.0, 16.0)),
        'ssd_d': 1.0 + nrm(16, (L, SSD_HEADS), 0.1),
        'ssd_norm': 1.0 + nrm(17, (L, SSD_INNER), 0.05),
        'swa_sink': nrm(18, (L, SWA_HEADS), 0.5),
        'hy_conv_w': nrm(19, (L, HY_SHORT, 3 * HY_WIDTH), HY_SHORT ** -0.5),
        'hy_conv_b': nrm(20, (L, 3 * HY_WIDTH), 0.02),
        'hy_w1': nrm(21, (L, HY_EMB, HY_HIDDEN), HY_EMB ** -0.5),
        'hy_b1': nrm(22, (L, HY_HIDDEN), 0.1),
        'hy_w2': nrm(23, (L, HY_HIDDEN, HY_HIDDEN), HY_HIDDEN ** -0.5),
        'hy_b2': nrm(24, (L, HY_HIDDEN), 0.1),
        'hy_w3': nrm(25, (L, HY_HIDDEN, HY_ORDER * 2 * HY_WIDTH), HY_HIDDEN ** -0.5),
        'hy_freq': 1.0 + nrm(26, (L, HY_HIDDEN), 0.05),
        'hy_d': nrm(27, (L, HY_ORDER, HY_WIDTH), 0.1),
        'w_branch': nrm(28, (L, N_BRANCH, BRANCH_W, D_MODEL), BRANCH_W ** -0.5),
        'w_out': nrm(29, (L, D_MODEL, D_MODEL), D_MODEL ** -0.5),
        'ffn_w_in': nrm(30, (L, D_MODEL, 2 * FFN_HIDDEN), D_MODEL ** -0.5),
        'ffn_w_out': nrm(31, (L, FFN_HIDDEN, D_MODEL), FFN_HIDDEN ** -0.5),
    }


def reference(x, c, ctx, c_ctx, ada_w, ada_b, norm_g, w_in, mla_q_norm, mla_w_uq, mla_kv_norm, mla_w_ukv,
              ssd_conv_w, ssd_conv_b, ssd_dt_bias, ssd_a_log, ssd_d, ssd_norm, swa_sink, hy_conv_w, hy_conv_b,
              hy_w1, hy_b1, hy_w2, hy_b2, hy_w3, hy_freq, hy_d, w_branch, w_out, ffn_w_in, ffn_w_out):
    bsz, n, _ = x.shape
    rows = n // GRID_W
    rope_mla = rope_tables(rows, MLA_ROPE)
    rope_swa = rope_tables(rows, SWA_HEAD_DIM)
    s_lat = jax.nn.silu(c)
    s_ctx = jax.nn.silu(c_ctx)
    xc = ctx
    for l in range(DEPTH):
        ctx_out = l < DEPTH - 1
        mod_l = (s_lat @ ada_w[l] + ada_b[l]).reshape(bsz, 6, 1, D_MODEL).transpose(1, 0, 2, 3)
        mod_c = (s_ctx @ ada_w[l] + ada_b[l]).reshape(6, 1, 1, D_MODEL)
        a_l = modulate(rmsnorm(x, norm_g[l, 0]), mod_l[0], mod_l[1])
        a_c = modulate(rmsnorm(xc, norm_g[l, 0]), mod_c[0], mod_c[1])
        y_l, y_c = token_mixers(a_l, a_c, ctx_out, rope_mla, rope_swa, w_in[l], mla_q_norm[l], mla_w_uq[l],
                                mla_kv_norm[l], mla_w_ukv[l], ssd_conv_w[l], ssd_conv_b[l], ssd_dt_bias[l],
                                ssd_a_log[l], ssd_d[l], ssd_norm[l], swa_sink[l], hy_conv_w[l], hy_conv_b[l],
                                hy_w1[l], hy_b1[l], hy_w2[l], hy_b2[l], hy_w3[l], hy_freq[l], hy_d[l],
                                w_branch[l], w_out[l])
        x = x + mod_l[2] * rmsnorm(y_l, norm_g[l, 1])
        f_l = modulate(rmsnorm(x, norm_g[l, 2]), mod_l[3], mod_l[4])
        x = x + mod_l[5] * rmsnorm(swiglu(f_l, ffn_w_in[l], ffn_w_out[l]), norm_g[l, 3])
        if ctx_out:
            xc = xc + mod_c[2] * rmsnorm(y_c, norm_g[l, 1])
            f_c = modulate(rmsnorm(xc, norm_g[l, 2]), mod_c[3], mod_c[4])
            xc = xc + mod_c[5] * rmsnorm(swiglu(f_c, ffn_w_in[l], ffn_w_out[l]), norm_g[l, 3])
    return x
```

```python
import functools
import math

import jax
import jax.numpy as jnp
from jax import lax
from jax.experimental import pallas as pl
from jax.experimental.pallas import tpu as pltpu

F32 = jnp.float32
MXU_DTYPE = jnp.bfloat16
HIGHEST = lax.Precision.HIGHEST

GRID_W = 64
EPS = 1e-6
ROPE_BASE = 10000.0
MLA_HEADS, MLA_NOPE, MLA_ROPE, MLA_V = 8, 64, 32, 64
MLA_SCALE = (MLA_NOPE + MLA_ROPE) ** -0.5
SSD_HEADS, SSD_HEADDIM, SSD_GROUPS, SSD_STATE, SSD_CHUNK = 8, 64, 2, 64, 128
SSD_INNER = SSD_HEADS * SSD_HEADDIM
SWA_HEADS, SWA_KV_HEADS, SWA_HEAD_DIM, SWA_WINDOW, SWA_BLOCK = 8, 2, 64, 128, 128
SWA_SCALE = SWA_HEAD_DIM ** -0.5
HY_WIDTH, HY_ORDER, HY_BANDS, HY_HIDDEN = 512, 2, 16, 64
HY_DECAY_TARGET, HY_FAST_DECAY, HY_SLOW_DECAY = 1e-2, 0.3, 1.5
N_BRANCH = 4
LANES = 128

VMEM_LIMIT = 56 * 1024 * 1024
_WHOLE_VMEM = pl.BlockSpec(memory_space=pltpu.VMEM)


def _params(n_grid, vmem=VMEM_LIMIT):
    return pltpu.CompilerParams(dimension_semantics=("arbitrary",) * n_grid, vmem_limit_bytes=vmem)


def _silu(x):
    return x * jax.nn.sigmoid(x)


def _softplus(x):
    return jnp.maximum(x, 0.0) + jnp.log1p(jnp.exp(-jnp.abs(x)))


def _rms(x, g):
    return x * lax.rsqrt(jnp.mean(x * x, axis=-1, keepdims=True) + EPS) * g


def _ada_kernel(s_ref, w_ref, b_ref, o_ref):
    s = _silu(s_ref[...])
    o_ref[...] = jnp.dot(s, w_ref[...], preferred_element_type=F32, precision=HIGHEST) + b_ref[...]


def _ada(cond, w, b, tn=1536):
    m, k = cond.shape
    n = w.shape[1]
    return pl.pallas_call(
        _ada_kernel,
        grid=(n // tn,),
        in_specs=[pl.BlockSpec((m, k), lambda j: (0, 0)),
                  pl.BlockSpec((k, tn), lambda j: (0, j)),
                  pl.BlockSpec((1, tn), lambda j: (0, j))],
        out_specs=pl.BlockSpec((m, tn), lambda j: (0, j)),
        out_shape=jax.ShapeDtypeStruct((m, n), F32),
        compiler_params=_params(1),
        name="ada_ln",
    )(cond, w, b.reshape(1, n))


def _norm_mm_kernel(*refs, n_w, modulate):
    it = iter(refs)
    x_ref, g_ref = next(it), next(it)
    if modulate:
        sh_ref, sc_ref = next(it), next(it)
    w_refs = [next(it) for _ in range(n_w)]
    o_refs = [next(it) for _ in range(n_w)]
    a = _rms(x_ref[...].astype(F32), g_ref[...])
    if modulate:
        a = a * (1.0 + sc_ref[...]) + sh_ref[...]
    a = a.astype(MXU_DTYPE)
    for w_ref, o_ref in zip(w_refs, o_refs):
        n = w_ref.shape[1]
        for c0 in range(0, n, 1024):
            c1 = min(n, c0 + 1024)
            o_ref[:, c0:c1] = jnp.dot(a, w_ref[:, c0:c1], preferred_element_type=F32).astype(o_ref.dtype)


def _norm_matmul(x, g, ws, out_dtypes, shift=None, scale=None, tm=256, name="norm_matmul"):
    bsz, t, k = x.shape
    tm = min(tm, t)
    modulate = shift is not None
    in_specs = [pl.BlockSpec((None, tm, k), lambda b, i: (b, i, 0)),
                pl.BlockSpec((1, k), lambda b, i: (0, 0))]
    args = [x, g.reshape(1, k).astype(F32)]
    if modulate:
        in_specs += [pl.BlockSpec((None, 1, k), lambda b, i: (b, 0, 0))] * 2
        args += [shift, scale]
    in_specs += [_WHOLE_VMEM] * len(ws)
    args += list(ws)
    return pl.pallas_call(
        functools.partial(_norm_mm_kernel, n_w=len(ws), modulate=modulate),
        grid=(bsz, t // tm),
        in_specs=in_specs,
        out_specs=[pl.BlockSpec((None, tm, w.shape[1]), lambda b, i: (b, i, 0)) for w in ws],
        out_shape=[jax.ShapeDtypeStruct((bsz, t, w.shape[1]), dt) for w, dt in zip(ws, out_dtypes)],
        compiler_params=_params(2),
        name=name,
    )(*args)


def _mla_kernel(*refs, t_l, s_c):
    if s_c:
        (qp, qr, cq, sq, knp_l, krp_l, krr_l, ck, sk, v_l, knp_c, krp_c, v_c, o_ref, kcat, vall) = refs
    else:
        (qp, qr, cq, sq, knp_l, krp_l, krr_l, ck, sk, v_l, o_ref, kcat, vall) = refs

    @pl.when(pl.program_id(1) == 0)
    def _():
        kro = krp_l[...].astype(F32) * ck[...] + krr_l[...].astype(F32) * sk[...]
        for h in range(MLA_HEADS):
            blk = slice(h * LANES, (h + 1) * LANES)
            kcat[h, 0:t_l, :] = (knp_l[:, blk].astype(F32) + kro).astype(kcat.dtype)
            if s_c:
                kcat[h, t_l:t_l + s_c, :] = (knp_c[:, blk].astype(F32)
                                             + krp_c[...].astype(F32)).astype(kcat.dtype)
        vall[0:t_l, :] = v_l[...]
        if s_c:
            vall[t_l:t_l + s_c, :] = v_c[...]

    tq = qp.shape[0]
    lane_lo = lax.broadcasted_iota(jnp.int32, (tq, LANES), 1) < MLA_V
    cqv, sqv = cq[...], sq[...]
    for j in range(MLA_HEADS // 2):
        outs = []
        for h in (2 * j, 2 * j + 1):
            blk = slice(h * LANES, (h + 1) * LANES)
            q = (qp[:, blk].astype(F32) * cqv + qr[:, blk].astype(F32) * sqv).astype(MXU_DTYPE)
            s = lax.dot_general(q, kcat[h], (((1,), (1,)), ((), ())), preferred_element_type=F32)
            p = jnp.exp(s - jnp.max(s, axis=-1, keepdims=True))
            l = jnp.sum(p, axis=-1, keepdims=True)
            o = jnp.dot(p.astype(MXU_DTYPE), vall[:, j * LANES:(j + 1) * LANES],
                        preferred_element_type=F32)
            outs.append(o * (1.0 / l))
        o_ref[:, j * LANES:(j + 1) * LANES] = jnp.where(lane_lo, outs[0], outs[1]).astype(o_ref.dtype)


def _mla_attention(qp, qr, cq, sq, knp_l, krp_l, krr_l, ck, sk, v_l, ctx=None, tq=256):
    bsz, t_q, _ = qp.shape
    t_l = knp_l.shape[1]
    s_c = 0 if ctx is None else ctx[0].shape[1]
    tq = min(tq, t_q)
    hw = MLA_HEADS * LANES
    vw = MLA_HEADS * MLA_V
    per_b = lambda rows, cols: pl.BlockSpec((None, rows, cols), lambda b, i: (b, 0, 0))
    in_specs = [pl.BlockSpec((None, tq, hw), lambda b, i: (b, i, 0)),
                pl.BlockSpec((None, tq, hw), lambda b, i: (b, i, 0)),
                pl.BlockSpec((tq, LANES), lambda b, i: (i, 0)),
                pl.BlockSpec((tq, LANES), lambda b, i: (i, 0)),
                per_b(t_l, hw), per_b(t_l, LANES), per_b(t_l, LANES),
                pl.BlockSpec((t_l, LANES), lambda b, i: (0, 0)),
                pl.BlockSpec((t_l, LANES), lambda b, i: (0, 0)),
                per_b(t_l, vw)]
    args = [qp, qr, cq, sq, knp_l, krp_l, krr_l, ck, sk, v_l]
    if s_c:
        in_specs += [per_b(s_c, hw), per_b(s_c, LANES), per_b(s_c, vw)]
        args += list(ctx)
    return pl.pallas_call(
        functools.partial(_mla_kernel, t_l=t_l, s_c=s_c),
        grid=(bsz, t_q // tq),
        in_specs=in_specs,
        out_specs=pl.BlockSpec((None, tq, vw), lambda b, i: (b, i, 0)),
        out_shape=jax.ShapeDtypeStruct((bsz, t_q, vw), MXU_DTYPE),
        scratch_shapes=[pltpu.VMEM((MLA_HEADS, t_l + s_c, LANES), MXU_DTYPE),
                        pltpu.VMEM((t_l + s_c, vw), MXU_DTYPE)],
        compiler_params=_params(2),
        name="mla_attention",
    )(*args)


def _swa_kernel(*refs, band, t_k):
    if band:
        (sink, q, qrot, cq, sq, k, krot, ck, sk, v, kc, vc, o_ref, kro) = refs
    else:
        (sink, q, qrot, cq, sq, kc, vc, o_ref) = refs
    i = pl.program_id(1)
    tq = q.shape[0]
    gw = 2 * SWA_HEAD_DIM
    kb_rows = 3 * SWA_BLOCK

    if band:
        @pl.when(i == 0)
        def _():
            ckv, skv = ck[...], sk[...]
            for g in range(SWA_KV_HEADS):
                blk = slice(g * gw, (g + 1) * gw)
                kro[:, blk] = (k[:, blk].astype(F32) * ckv + krot[:, blk].astype(F32) * skv).astype(kro.dtype)

        start = pl.multiple_of(jnp.clip((i - 1) * SWA_BLOCK, 0, t_k - kb_rows), SWA_BLOCK)
        q_pos = i * SWA_BLOCK + lax.broadcasted_iota(jnp.int32, (tq, kb_rows), 0)
        k_pos = start + lax.broadcasted_iota(jnp.int32, (tq, kb_rows), 1)
        in_band = jnp.abs(q_pos - k_pos) <= SWA_WINDOW
        kb = kro[pl.ds(start, kb_rows), :]
        vb = v[pl.ds(start, kb_rows), :]

    lane_lo = lax.broadcasted_iota(jnp.int32, (tq, LANES), 1) < SWA_HEAD_DIM
    cqv, sqv = cq[...], sq[...]
    nt = (((1,), (1,)), ((), ()))
    for j in range(SWA_HEADS // 2):
        g = j // (SWA_HEADS // SWA_KV_HEADS // 2)
        gblk = slice(g * gw, (g + 1) * gw)
        blk = slice(j * LANES, (j + 1) * LANES)
        qro = q[:, blk].astype(F32) * cqv + qrot[:, blk].astype(F32) * sqv
        outs = []
        for hh in range(2):
            qm = jnp.where(lane_lo if hh == 0 else ~lane_lo, qro, 0.0).astype(MXU_DTYPE)
            snk = sink[2 * j + hh]
            s_c = lax.dot_general(qm, kc[:, gblk], nt, preferred_element_type=F32)
            m = jnp.maximum(jnp.max(s_c, axis=-1, keepdims=True), snk)
            if band:
                s_b = lax.dot_general(qm, kb[:, gblk], nt, preferred_element_type=F32)
                s_b = jnp.where(in_band, s_b, -1e30)
                m = jnp.maximum(m, jnp.max(s_b, axis=-1, keepdims=True))
            p_c = jnp.exp(s_c - m)
            l = jnp.sum(p_c, axis=-1, keepdims=True) + jnp.exp(snk - m)
            o = jnp.dot(p_c.astype(MXU_DTYPE), vc[:, gblk], preferred_element_type=F32)
            if band:
                p_b = jnp.exp(s_b - m)
                l = l + jnp.sum(p_b, axis=-1, keepdims=True)
                o = o + jnp.dot(p_b.astype(MXU_DTYPE), vb[:, gblk], preferred_element_type=F32)
            outs.append(o * (1.0 / l))
        o_ref[:, blk] = jnp.where(lane_lo, outs[0], outs[1]).astype(o_ref.dtype)


def _swa_attention(sink, q, qrot, cq, sq, kc, vc, lat=None):
    bsz, t, hw = q.shape
    s_c = kc.shape[1]
    tq = SWA_BLOCK
    kvw = kc.shape[2]
    band = lat is not None
    per_b = lambda rows, cols: pl.BlockSpec((None, rows, cols), lambda b, i: (b, 0, 0))
    qspec = pl.BlockSpec((None, tq, hw), lambda b, i: (b, i, 0))
    tspec = pl.BlockSpec((tq, LANES), lambda b, i: (i, 0))
    in_specs = [pl.BlockSpec(memory_space=pltpu.SMEM), qspec, qspec, tspec, tspec]
    args = [sink.astype(F32), q, qrot, cq, sq]
    scratch = []
    if band:
        k, krot, ck, sk, v = lat
        full_t = pl.BlockSpec((t, LANES), lambda b, i: (0, 0))
        in_specs += [per_b(t, kvw), per_b(t, kvw), full_t, full_t, per_b(t, kvw)]
        args += [k, krot, ck, sk, v]
        scratch = [pltpu.VMEM((t, kvw), MXU_DTYPE)]
    in_specs += [per_b(s_c, kvw), per_b(s_c, kvw)]
    args += [kc, vc]
    return pl.pallas_call(
        functools.partial(_swa_kernel, band=band, t_k=t),
        grid=(bsz, t // tq),
        in_specs=in_specs,
        out_specs=pl.BlockSpec((None, tq, hw), lambda b, i: (b, i, 0)),
        out_shape=jax.ShapeDtypeStruct((bsz, t, hw), MXU_DTYPE),
        scratch_shapes=scratch,
        compiler_params=_params(2),
        name="swa_attention" if band else "ctx_sink_attention",
    )(*args)


def _dwconv_kernel(*refs, n_seg, act):
    x_refs = refs[:n_seg]
    w_ref, b_ref, o_ref = refs[n_seg:]
    taps = w_ref.shape[0]
    off = 0
    for x_ref in x_refs:
        t = x_ref.shape[0]
        x = x_ref[...].astype(F32)
        row = lax.broadcasted_iota(jnp.int32, x.shape, 0)
        acc = jnp.zeros_like(x) + b_ref[...]
        for kk in range(taps):
            d = kk - taps // 2
            if d == 0:
                xs = x
            else:
                xs = pltpu.roll(x, (-d) % t, axis=0)
                xs = jnp.where((row + d >= 0) & (row + d < t), xs, 0.0)
            acc = acc + xs * w_ref[kk:kk + 1, :]
        if act:
            acc = _silu(acc)
        o_ref[off:off + t, :] = acc.astype(o_ref.dtype)
        off += t


def _dwconv(xs, w, b, act, tc=256, name="dwconv"):
    bsz, _, c = xs[0].shape
    t_tot = sum(x.shape[1] for x in xs)
    taps = w.shape[0]
    in_specs = [pl.BlockSpec((None, x.shape[1], tc), lambda bb, j: (bb, 0, j)) for x in xs]
    in_specs += [pl.BlockSpec((taps, tc), lambda bb, j: (0, j)), pl.BlockSpec((1, tc), lambda bb, j: (0, j))]
    return pl.pallas_call(
        functools.partial(_dwconv_kernel, n_seg=len(xs), act=act),
        grid=(bsz, c // tc),
        in_specs=in_specs,
        out_specs=pl.BlockSpec((None, t_tot, tc), lambda bb, j: (bb, 0, j)),
        out_shape=jax.ShapeDtypeStruct((bsz, t_tot, c), F32),
        compiler_params=_params(2),
        name=name,
    )(*xs, w.astype(F32), b.reshape(1, c).astype(F32))


def _ssd_scan_kernel(xs_f, bm_f, cm_f, dt_f, dtt_f, xs_b, bm_b, cm_b, dt_b, dtt_b,
                     b_row, al_row, b_col, al_col, yf_ref, yb_ref, st_ref):
    @pl.when(pl.program_id(1) == 0)
    def _():
        st_ref[...] = jnp.zeros_like(st_ref)

    q = SSD_CHUNK
    row = lax.broadcasted_iota(jnp.int32, (q, q), 0)
    col = lax.broadcasted_iota(jnp.int32, (q, q), 1)
    lane_lo = col < SSD_HEADDIM
    a_row = -jnp.exp(al_row[...])
    a_col = -jnp.exp(al_col[...])
    nt = (((1,), (1,)), ((), ()))
    tn = (((0,), (0,)), ((), ()))
    dirs = ((xs_f, bm_f, cm_f, dt_f, dtt_f, yf_ref), (xs_b, bm_b, cm_b, dt_b, dtt_b, yb_ref))
    for d, (xs_ref, bm_ref, cm_ref, dt_ref, dtt_ref, y_ref) in enumerate(dirs):
        feeds = (row >= col) if d == 0 else (row <= col)
        feeds_t = (row <= col) if d == 0 else (row >= col)
        last = q - 1 if d == 0 else 0
        dt = _softplus(dt_ref[...] + b_row[...])
        dtt = _softplus(dtt_ref[...] + b_col[...])
        cs = jnp.dot(feeds.astype(F32), dt * a_row, preferred_element_type=F32, precision=HIGHEST)
        cst = jnp.dot(dtt * a_col, feeds_t.astype(F32), preferred_element_type=F32, precision=HIGHEST)
        bblk = bm_ref[...]
        cblk = cm_ref[...]
        for g in range(SSD_GROUPS):
            gmask = (col // SSD_STATE) == g
            cg = jnp.where(gmask, cblk, 0.0).astype(MXU_DTYPE)
            bg = jnp.where(gmask, bblk, 0.0)
            gmat = lax.dot_general(cg, bg.astype(MXU_DTYPE), nt, preferred_element_type=F32)
            for jj in range(SSD_HEADS // SSD_GROUPS // 2):
                j = g * (SSD_HEADS // SSD_GROUPS // 2) + jj
                xblk = xs_ref[:, j * LANES:(j + 1) * LANES]
                xb = xblk.astype(MXU_DTYPE)
                s_in = st_ref[d, j]
                y_inter = jnp.dot(cg, s_in.astype(MXU_DTYPE), preferred_element_type=F32)
                ys, news, decs = [], [], []
                for hh in range(2):
                    c = d * SSD_HEADS + 2 * j + hh
                    ccol = cs[:, c:c + 1]
                    crow = cst[c:c + 1, :]
                    lmat = jnp.where(feeds, jnp.exp(ccol - crow), 0.0)
                    sc = (gmat * lmat * dtt[c:c + 1, :]).astype(MXU_DTYPE)
                    y_intra = jnp.dot(sc, xb, preferred_element_type=F32)
                    ys.append(y_intra + jnp.exp(ccol) * y_inter)
                    tot = cst[c:c + 1, last:last + 1]
                    w = dt[:, c:c + 1] * jnp.exp(tot - ccol)
                    bw = (bg * w).astype(MXU_DTYPE)
                    xh = jnp.where(lane_lo if hh == 0 else ~lane_lo, xblk, 0.0).astype(MXU_DTYPE)
                    news.append(lax.dot_general(bw, xh, tn, preferred_element_type=F32))
                    decs.append(jnp.exp(tot))
                y_ref[:, j * LANES:(j + 1) * LANES] = jnp.where(lane_lo, ys[0], ys[1])
                st_ref[d, j] = s_in * jnp.where(lane_lo, decs[0], decs[1]) + news[0] + news[1]


def _ssd_scan(xbc, dt_raw, dt_raw_t, dt_bias, a_log, nc_ctx):
    bsz, t_c, _ = xbc.shape
    q = SSD_CHUNK
    nc = t_c // q
    nh2 = 2 * SSD_HEADS

    def fwd(s):
        return s

    def bwd(s):
        return jnp.where(s < nc_ctx, nc_ctx - 1 - s, nc + nc_ctx - 1 - s)

    def specs(order):
        return [pl.BlockSpec((None, q, SSD_INNER), lambda b, s: (b, order(s), 0)),
                pl.BlockSpec((None, q, LANES), lambda b, s: (b, order(s), SSD_INNER // LANES)),
                pl.BlockSpec((None, q, LANES), lambda b, s: (b, order(s), SSD_INNER // LANES + 1)),
                pl.BlockSpec((None, q, nh2), lambda b, s: (b, order(s), 0)),
                pl.BlockSpec((None, nh2, q), lambda b, s: (b, 0, order(s)))]

    small = lambda r, c: pl.BlockSpec((r, c), lambda b, s: (0, 0))
    bias = dt_bias.astype(F32).reshape(1, nh2)
    alog = a_log.astype(F32).reshape(1, nh2)
    return pl.pallas_call(
        _ssd_scan_kernel,
        grid=(bsz, nc),
        in_specs=specs(fwd) + specs(bwd) + [small(1, nh2), small(1, nh2), small(nh2, 1), small(nh2, 1)],
        out_specs=[pl.BlockSpec((None, q, SSD_INNER), lambda b, s: (b, fwd(s), 0)),
                   pl.BlockSpec((None, q, SSD_INNER), lambda b, s: (b, bwd(s), 0))],
        out_shape=[jax.ShapeDtypeStruct((bsz, t_c, SSD_INNER), F32)] * 2,
        scratch_shapes=[pltpu.VMEM((2, SSD_HEADS // 2, LANES, LANES), F32)],
        compiler_params=_params(2),
        name="ssd_scan",
    )(xbc, xbc, xbc, dt_raw, dt_raw_t, xbc, xbc, xbc, dt_raw, dt_raw_t,
      bias, alog, bias.reshape(nh2, 1), alog.reshape(nh2, 1))


def _ssd_out_kernel(yf, yb, xs, z, dexp, ng, o_ref):
    y = yf[...] + yb[...] + dexp[...] * xs[...]
    yg = y * _silu(z[...].astype(F32))
    gw = SSD_INNER // SSD_GROUPS
    for g in range(SSD_GROUPS):
        blk = slice(g * gw, (g + 1) * gw)
        o_ref[:, blk] = _rms(yg[:, blk], ng[:, blk]).astype(o_ref.dtype)


def _ssd_output(yf, yb, xbc, z, d_skip, norm_g, row0, tm=256):
    bsz, t, _ = z.shape
    tm = min(tm, t)
    off = row0 // tm
    sp_y = pl.BlockSpec((None, tm, SSD_INNER), lambda b, i: (b, i + off, 0))
    vec = pl.BlockSpec((1, SSD_INNER), lambda b, i: (0, 0))
    dexp = jnp.repeat(d_skip.astype(F32), SSD_HEADDIM).reshape(1, SSD_INNER)
    return pl.pallas_call(
        _ssd_out_kernel,
        grid=(bsz, t // tm),
        in_specs=[sp_y, sp_y, sp_y, pl.BlockSpec((None, tm, SSD_INNER), lambda b, i: (b, i, 0)), vec, vec],
        out_specs=pl.BlockSpec((None, tm, SSD_INNER), lambda b, i: (b, i, 0)),
        out_shape=jax.ShapeDtypeStruct((bsz, t, SSD_INNER), MXU_DTYPE),
        compiler_params=_params(2),
        name="ssd_output",
    )(yf, yb, xbc, z, dexp, norm_g.astype(F32).reshape(1, SSD_INNER))


def _hy_mlp_kernel(feat_ref, w1, b1, w2, b2, w3, fr, delta, h_ref, cs_ref):
    feat = feat_ref[...]
    h = jnp.sin(fr[...] * (jnp.dot(feat, w1[...], preferred_element_type=F32, precision=HIGHEST) + b1[...]))
    h = jnp.sin(fr[...] * (jnp.dot(h, w2[...], preferred_element_type=F32, precision=HIGHEST) + b2[...]))
    h = jnp.dot(h, w3[...], preferred_element_type=F32, precision=HIGHEST)
    h = h * jnp.exp(-feat[:, 0:1] * delta[...])
    h_ref[...] = h
    s = jnp.sum(jnp.abs(h), axis=0, keepdims=True)

    @pl.when(pl.program_id(0) == 0)
    def _():
        cs_ref[...] = s

    @pl.when(pl.program_id(0) != 0)
    def _():
        cs_ref[...] += s


def _hy_spectrum_kernel(h0_ref, h1_ref, c0_ref, c1_ref, cm_ref, sf_ref, ar_ref, ai_ref, br_ref):
    n = h0_ref.shape[0]
    inv = 1.0 / (c0_ref[...] + c1_ref[...])
    row = lax.broadcasted_iota(jnp.int32, h0_ref.shape, 0)
    first = row == 0
    h0 = h0_ref[...] * inv
    h1 = jnp.where(first, 0.0, h1_ref[...] * inv)
    a = h0 + h1
    kr = jnp.dot(cm_ref[...], a.astype(MXU_DTYPE), preferred_element_type=F32)
    kq = jnp.dot(sf_ref[...], (h1 - h0).astype(MXU_DTYPE), preferred_element_type=F32)
    k_nyq = jnp.sum(jnp.where(row % 2 == 0, a, -a), axis=0, keepdims=True)
    inv_n = 1.0 / (2 * n)
    ar = kr * jnp.where(first, inv_n, 2.0 * inv_n)
    ar_ref[...] = ar
    ai_ref[...] = jnp.where(first, 0.0, kq * (2.0 * inv_n))
    br_ref[...] = jnp.where(first, k_nyq * inv_n, ar)


def _hyena_filters(n, w1, b1, w2, b2, w3, freq, cm, sf):
    t = jnp.arange(n, dtype=F32)
    tnorm = t / n
    bands = jnp.linspace(1e-4, HY_BANDS - 1, HY_BANDS, dtype=F32)
    ang = 2 * math.pi * t[:, None] * bands[None, :] / n
    feat = jnp.concatenate([tnorm[:, None], jnp.cos(ang), -jnp.sin(ang)], axis=-1)
    emb = feat.shape[1]
    feat = jnp.pad(feat, ((0, 0), (0, HY_HIDDEN - emb)))
    w1p = jnp.pad(w1.astype(F32), ((0, HY_HIDDEN - emb), (0, 0)))
    deltas = jnp.abs(jnp.linspace(math.log(HY_DECAY_TARGET) / HY_SLOW_DECAY,
                                  math.log(HY_DECAY_TARGET) / HY_FAST_DECAY, HY_WIDTH, dtype=F32))
    ncol = HY_ORDER * 2 * HY_WIDTH
    delta_row = jnp.tile(deltas, HY_ORDER * 2).reshape(1, ncol)
    tt = min(n, 256)
    small = lambda r, c: pl.BlockSpec((r, c), lambda i: (0, 0))
    row = lambda v: v.astype(F32).reshape(1, -1)
    h, colsum = pl.pallas_call(
        _hy_mlp_kernel,
        grid=(n // tt,),
        in_specs=[pl.BlockSpec((tt, HY_HIDDEN), lambda i: (i, 0)),
                  small(HY_HIDDEN, HY_HIDDEN), small(1, HY_HIDDEN),
                  small(HY_HIDDEN, HY_HIDDEN), small(1, HY_HIDDEN),
                  small(HY_HIDDEN, ncol), small(1, HY_HIDDEN), small(1, ncol)],
        out_specs=[pl.BlockSpec((tt, ncol), lambda i: (i, 0)), small(1, ncol)],
        out_shape=[jax.ShapeDtypeStruct((n, ncol), F32), jax.ShapeDtypeStruct((1, ncol), F32)],
        compiler_params=_params(1),
        name="hyena_filter_mlp",
    )(feat, w1p, row(b1), w2.astype(F32), row(b2), w3.astype(F32), row(freq), delta_row)

    tc = 256
    per_o = HY_WIDTH // tc
    side0 = lambda jc: (jc // per_o) * 2 * per_o + jc % per_o
    side1 = lambda jc: (jc // per_o) * 2 * per_o + per_o + jc % per_o
    nout = HY_ORDER * HY_WIDTH
    out_spec = pl.BlockSpec((n, tc), lambda jc: (0, jc))
    return pl.pallas_call(
        _hy_spectrum_kernel,
        grid=(nout // tc,),
        in_specs=[pl.BlockSpec((n, tc), lambda jc: (0, side0(jc))),
                  pl.BlockSpec((n, tc), lambda jc: (0, side1(jc))),
                  pl.BlockSpec((1, tc), lambda jc: (0, side0(jc))),
                  pl.BlockSpec((1, tc), lambda jc: (0, side1(jc))),
                  _WHOLE_VMEM, _WHOLE_VMEM],
        out_specs=[out_spec] * 3,
        out_shape=[jax.ShapeDtypeStruct((n, nout), F32)] * 3,
        compiler_params=_params(1),
        name="hyena_filter_spectrum",
    )(h, h, colsum, colsum, cm, sf)


def _hy_fwd_kernel(u_ref, cm_ref, sf_ref, ar_ref, ai_ref, br_ref, yr_ref, yi_ref):
    u = u_ref[...].astype(MXU_DTYPE)
    p = jnp.dot(cm_ref[...], u, preferred_element_type=F32)
    q = jnp.dot(sf_ref[...], u, preferred_element_type=F32)
    ai = ai_ref[...]
    yr_ref[...] = (p * ar_ref[...] + q * ai).astype(yr_ref.dtype)
    yi_ref[...] = (q * br_ref[...] - p * ai).astype(yi_ref.dtype)


def _hy_inv_kernel(yr_ref, yi_ref, cm_ref, si_ref, u_ref, xg_ref, d_ref, o_ref):
    y = (jnp.dot(cm_ref[...], yr_ref[...], preferred_element_type=F32)
         + jnp.dot(si_ref[...], yi_ref[...], preferred_element_type=F32))
    u = u_ref[...].astype(F32)
    o_ref[...] = (xg_ref[...].astype(F32) * (y + u * d_ref[...])).astype(o_ref.dtype)


def _hyena_conv(u, u_col0, xg, xg_col0, spectra, order, d, tables, out_dtype, tc=256):
    cm, sf, si = tables
    ar, ai, br = spectra
    bsz, n, _ = u.shape
    nct = HY_WIDTH // tc
    ucol, gcol, scol = u_col0 // tc, xg_col0 // tc, order * nct
    tok = lambda c0: pl.BlockSpec((None, n, tc), lambda c, b: (b, 0, c0 + c))
    spec_sp = pl.BlockSpec((n, tc), lambda c, b: (0, scol + c))
    mid = pl.BlockSpec((None, n, tc), lambda c, b: (b, 0, c))
    yr, yi = pl.pallas_call(
        _hy_fwd_kernel,
        grid=(nct, bsz),
        in_specs=[tok(ucol), _WHOLE_VMEM, _WHOLE_VMEM, spec_sp, spec_sp, spec_sp],
        out_specs=[mid, mid],
        out_shape=[jax.ShapeDtypeStruct((bsz, n, HY_WIDTH), MXU_DTYPE)] * 2,
        compiler_params=_params(2),
        name="hyena_dft_forward",
    )(u, cm, sf, ar, ai, br)
    return pl.pallas_call(
        _hy_inv_kernel,
        grid=(nct, bsz),
        in_specs=[mid, mid, _WHOLE_VMEM, _WHOLE_VMEM, tok(ucol), tok(gcol),
                  pl.BlockSpec((1, tc), lambda c, b: (0, c))],
        out_specs=mid,
        out_shape=jax.ShapeDtypeStruct((bsz, n, HY_WIDTH), out_dtype),
        compiler_params=_params(2),
        name="hyena_dft_inverse",
    )(yr, yi, cm, si, u, xg, d.astype(F32).reshape(1, HY_WIDTH))


def _dft_tables(n):
    idx = jnp.arange(n, dtype=jnp.int32)
    prod = (idx[:, None] * idx[None, :]) % (2 * n)
    ang = prod.astype(F32) * (math.pi / n)
    alt = jnp.where(idx % 2 == 0, 1.0, -1.0).astype(F32)
    cm = jnp.cos(ang)
    sf = jnp.where(idx[:, None] == 0, alt[None, :], jnp.sin(ang))
    return cm.astype(MXU_DTYPE), sf.astype(MXU_DTYPE), sf.T.astype(MXU_DTYPE)


def _hyena(hy, conv_w, conv_b, spectra, d, tables):
    u3 = _dwconv([hy], conv_w, conv_b, act=False, name="hyena_short_conv")
    z = _hyena_conv(u3, 0, u3, HY_WIDTH, spectra, 0, d[0], tables, F32)
    return _hyena_conv(z, 0, u3, 2 * HY_WIDTH, spectra, 1, d[1], tables, MXU_DTYPE)


def _merge_kernel(b0, b1, b2, b3, gate_ref, wb_ref, wo_ref, x_ref, mg_ref, g_ref, o_ref):
    d = x_ref.shape[1]
    acc = None
    for i, br in enumerate((b0, b1, b2, b3)):
        proj = jnp.dot(br[...], wb_ref[i], preferred_element_type=F32)
        term = jax.nn.sigmoid(gate_ref[:, i * d:(i + 1) * d].astype(F32)) * proj
        acc = term if acc is None else acc + term
    y = jnp.dot(acc.astype(MXU_DTYPE), wo_ref[...], preferred_element_type=F32)
    o_ref[...] = x_ref[...] + mg_ref[...] * _rms(y, g_ref[...])


def _merge(branches, gate, w_branch, w_out, x, mod_gate, g, tm=256):
    bsz, t, d = x.shape
    tm = min(tm, t)
    bw = branches[0].shape[2]
    tok = lambda w: pl.BlockSpec((None, tm, w), lambda b, i: (b, i, 0))
    return pl.pallas_call(
        _merge_kernel,
        grid=(bsz, t // tm),
        in_specs=[tok(bw)] * N_BRANCH + [tok(N_BRANCH * d), _WHOLE_VMEM, _WHOLE_VMEM, tok(d),
                                         pl.BlockSpec((None, 1, d), lambda b, i: (b, 0, 0)),
                                         pl.BlockSpec((1, d), lambda b, i: (0, 0))],
        out_specs=tok(d),
        out_shape=jax.ShapeDtypeStruct((bsz, t, d), F32),
        compiler_params=_params(2),
        name="merge_branches",
    )(*branches, gate, w_branch, w_out, x, mod_gate, g.astype(F32).reshape(1, d))


def _ffn_kernel(x_ref, g_in, sh_ref, sc_ref, wg_ref, wu_ref, wo_ref, g_out, mg_ref, o_ref, *, chunk):
    x = x_ref[...]
    a = (_rms(x, g_in[...]) * (1.0 + sc_ref[...]) + sh_ref[...]).astype(MXU_DTYPE)
    hidden = wg_ref.shape[1]
    acc = None
    for c0 in range(0, hidden, chunk):
        gch = jnp.dot(a, wg_ref[:, c0:c0 + chunk], preferred_element_type=F32)
        uch = jnp.dot(a, wu_ref[:, c0:c0 + chunk], preferred_element_type=F32)
        act = (_silu(gch) * uch).astype(MXU_DTYPE)
        part = jnp.dot(act, wo_ref[c0:c0 + chunk, :], preferred_element_type=F32)
        acc = part if acc is None else acc + part
    o_ref[...] = x + mg_ref[...] * _rms(acc, g_out[...])


def _ffn(x, g_in, shift, scale, w_gate, w_up, w_out, g_out, mod_gate, tm=256):
    bsz, t, d = x.shape
    tm = min(tm, t)
    hidden = w_gate.shape[1]
    chunk = hidden // 2
    tok = pl.BlockSpec((None, tm, d), lambda b, i: (b, i, 0))
    per_b = pl.BlockSpec((None, 1, d), lambda b, i: (b, 0, 0))
    vec = pl.BlockSpec((1, d), lambda b, i: (0, 0))
    return pl.pallas_call(
        functools.partial(_ffn_kernel, chunk=chunk),
        grid=(bsz, t // tm),
        in_specs=[tok, vec, per_b, per_b, _WHOLE_VMEM, _WHOLE_VMEM, _WHOLE_VMEM, vec, per_b],
        out_specs=tok,
        out_shape=jax.ShapeDtypeStruct((bsz, t, d), F32),
        compiler_params=_params(2),
        name="swiglu_ffn",
    )(x, g_in.astype(F32).reshape(1, d), shift, scale, w_gate, w_up, w_out,
      g_out.astype(F32).reshape(1, d), mod_gate)


def _rot_cols(w, half):
    return jnp.concatenate([-w[:, half:], w[:, :half]], axis=1)


def _rot_heads(w, heads, dim):
    k = w.shape[0]
    w3 = w.reshape(k, heads, dim)
    return jnp.concatenate([-w3[:, :, dim // 2:], w3[:, :, :dim // 2]], axis=2).reshape(k, heads * dim)


def _dup_heads(w, heads, dim):
    k = w.shape[0]
    w3 = w.reshape(k, heads, 1, dim)
    return jnp.broadcast_to(w3, (k, heads, 2, dim)).reshape(k, heads * 2 * dim)


def _rope_tables(rows, dim):
    row = jnp.repeat(jnp.arange(rows, dtype=F32), GRID_W)
    col = jnp.tile(jnp.arange(GRID_W, dtype=F32), rows)
    n_freq = dim // 4
    inv = ROPE_BASE ** (-jnp.arange(n_freq, dtype=F32) / n_freq)
    ang = jnp.concatenate([row[:, None] * inv, col[:, None] * inv], axis=-1)
    return jnp.cos(ang), jnp.sin(ang)


def _mla_tables(t, rope):
    ones = jnp.ones((t, MLA_NOPE), F32)
    zeros_n = jnp.zeros((t, MLA_NOPE), F32)
    pad = jnp.zeros((t, LANES - MLA_NOPE - MLA_ROPE), F32)
    if rope is None:
        c = jnp.ones((t, MLA_ROPE), F32)
        s = jnp.zeros((t, MLA_ROPE), F32)
    else:
        c = jnp.concatenate([rope[0], rope[0]], axis=1)
        s = jnp.concatenate([rope[1], rope[1]], axis=1)
    plain_q = jnp.concatenate([ones, c, pad], axis=1)
    rot = jnp.concatenate([zeros_n, s, pad], axis=1)
    plain_k = jnp.concatenate([zeros_n, c, pad], axis=1)
    return plain_q, rot, plain_k


def _swa_tables(t, rope):
    if rope is None:
        return jnp.ones((t, LANES), F32), jnp.zeros((t, LANES), F32)
    c = jnp.concatenate([rope[0]] * 4, axis=1)
    s = jnp.concatenate([rope[1]] * 4, axis=1)
    return c, s


def _layer_weights(w_in, mla_w_uq, mla_w_ukv):
    dm = w_in.shape[0]
    sizes = (384, 256, MLA_ROPE, SSD_INNER, SSD_INNER + 2 * SSD_GROUPS * SSD_STATE, 2 * SSD_HEADS,
             SWA_HEADS * SWA_HEAD_DIM, 2 * SWA_KV_HEADS * SWA_HEAD_DIM, 3 * HY_WIDTH, N_BRANCH * dm)
    offs = [0]
    for s in sizes:
        offs.append(offs[-1] + s)
    seg = lambda i: w_in[:, offs[i]:offs[i + 1]]
    cast = lambda w: w.astype(MXU_DTYPE)
    pad_kr = lambda w: jnp.pad(w, ((0, 0), (MLA_NOPE, LANES - MLA_NOPE - MLA_ROPE)))
    w_kr = seg(2)
    kvw = SWA_KV_HEADS * SWA_HEAD_DIM
    w_swk, w_swv = seg(7)[:, :kvw], seg(7)[:, kvw:]
    w = {
        "qa": cast(seg(0)), "ckv": cast(seg(1)),
        "krp": cast(pad_kr(w_kr)), "krr": cast(pad_kr(_rot_cols(w_kr, MLA_ROPE // 2))),
        "z": cast(seg(3)), "xbc": cast(seg(4)), "dt": cast(seg(5)),
        "swq": cast(seg(6)), "swq_rot": cast(_rot_heads(seg(6), SWA_HEADS, SWA_HEAD_DIM)),
        "swk": cast(_dup_heads(w_swk, SWA_KV_HEADS, SWA_HEAD_DIM)),
        "swk_rot": cast(_dup_heads(_rot_heads(w_swk, SWA_KV_HEADS, SWA_HEAD_DIM), SWA_KV_HEADS, SWA_HEAD_DIM)),
        "swv": cast(_dup_heads(w_swv, SWA_KV_HEADS, SWA_HEAD_DIM)),
        "hy": cast(seg(8)), "gate": cast(seg(9)),
    }
    kq = mla_w_uq.shape[0]
    dq = MLA_NOPE + MLA_ROPE
    uq = mla_w_uq.reshape(kq, MLA_HEADS, dq)
    padq = ((0, 0), (0, 0), (0, LANES - dq))
    w["uq"] = cast(jnp.pad(uq, padq).reshape(kq, MLA_HEADS * LANES))
    uq_rope = uq[:, :, MLA_NOPE:]
    uq_rot = jnp.concatenate([-uq_rope[:, :, MLA_ROPE // 2:], uq_rope[:, :, :MLA_ROPE // 2]], axis=2)
    uq_rot = jnp.pad(uq_rot, ((0, 0), (0, 0), (MLA_NOPE, LANES - dq)))
    w["uq_rot"] = cast(uq_rot.reshape(kq, MLA_HEADS * LANES))
    kk = mla_w_ukv.shape[0]
    ukv = mla_w_ukv.reshape(kk, MLA_HEADS, MLA_NOPE + MLA_V)
    w["uk"] = cast(jnp.pad(ukv[:, :, :MLA_NOPE], ((0, 0), (0, 0), (0, LANES - MLA_NOPE))).reshape(kk, MLA_HEADS * LANES))
    w["uv"] = cast(ukv[:, :, MLA_NOPE:].reshape(kk, MLA_HEADS * MLA_V))
    return w


def _token_mixers(x, xc, mod_l, mod_c, ctx_out, rope_mla, rope_swa, norm_g0, w, p, dft_l, dft_c):
    bsz, t, dm = x.shape
    s_c = xc.shape[1]
    bf = MXU_DTYPE

    lat_names = ["qa", "ckv", "krp", "krr", "z", "xbc", "dt", "swq", "swq_rot", "swk", "swk_rot", "swv", "hy", "gate"]
    lat_dt = [bf, bf, bf, bf, bf, bf, F32, bf, bf, bf, bf, bf, bf, bf]
    lat = dict(zip(lat_names, _norm_matmul(x, norm_g0, [w[k] for k in lat_names], lat_dt,
                                           shift=mod_l[0], scale=mod_l[1], name="in_proj_latent")))
    ctx_names = ["ckv", "krp", "xbc", "dt", "swk", "swv"] + (["qa", "z", "swq", "hy", "gate"] if ctx_out else [])
    ctx_dt = [F32 if k == "dt" else bf for k in ctx_names]
    ctx = dict(zip(ctx_names, _norm_matmul(xc, norm_g0, [w[k] for k in ctx_names], ctx_dt,
                                           shift=mod_c[0], scale=mod_c[1], name="in_proj_context")))

    knp_l, v_l = _norm_matmul(lat["ckv"], p["mla_kv_norm"], [w["uk"], w["uv"]], [bf, bf], name="mla_kv_up")
    knp_c, v_c = _norm_matmul(ctx["ckv"], p["mla_kv_norm"], [w["uk"], w["uv"]], [bf, bf], name="mla_kv_up_ctx")
    qp, qr = _norm_matmul(lat["qa"], p["mla_q_norm"], [w["uq"], w["uq_rot"]], [bf, bf], name="mla_q_up")
    cq, sq, ck = _mla_tables(t, rope_mla)
    mla_l = _mla_attention(qp, qr, cq * MLA_SCALE, sq * MLA_SCALE, knp_l, lat["krp"], lat["krr"], ck, sq, v_l,
                           ctx=(knp_c, ctx["krp"], v_c))

    xbc = _dwconv([ctx["xbc"], lat["xbc"]], p["ssd_conv_w"], p["ssd_conv_b"], act=True, name="ssd_conv")
    dt_raw = jnp.concatenate([ctx["dt"], lat["dt"]], axis=1)
    yf, yb = _ssd_scan(xbc, dt_raw, jnp.swapaxes(dt_raw, 1, 2), p["ssd_dt_bias"], p["ssd_a_log"], s_c // SSD_CHUNK)
    ssd_l = _ssd_output(yf, yb, xbc, lat["z"], p["ssd_d"], p["ssd_norm"], s_c)

    csw, ssw = _swa_tables(t, rope_swa)
    swa_l = _swa_attention(p["swa_sink"], lat["swq"], lat["swq_rot"], csw * SWA_SCALE, ssw * SWA_SCALE,
                           ctx["swk"], ctx["swv"], lat=(lat["swk"], lat["swk_rot"], csw, ssw, lat["swv"]))

    hy_args = (p["hy_w1"], p["hy_b1"], p["hy_w2"], p["hy_b2"], p["hy_w3"], p["hy_freq"])
    spec_l = _hyena_filters(t, *hy_args, dft_l[0], dft_l[1])
    hy_l = _hyena(lat["hy"], p["hy_conv_w"], p["hy_conv_b"], spec_l, p["hy_d"], dft_l)

    w_branch = p["w_branch"].astype(bf)
    w_out = p["w_out"].astype(bf)
    x_new = _merge([mla_l, ssd_l, swa_l, hy_l], lat["gate"], w_branch, w_out, x, mod_l[2], p["norm_g1"])
    if not ctx_out:
        return x_new, None

    qp_c, = _norm_matmul(ctx["qa"], p["mla_q_norm"], [w["uq"]], [bf], name="mla_q_up_ctx")
    cq_c, sq_c, ck_c = _mla_tables(s_c, None)
    mla_c = _mla_attention(qp_c, qp_c, cq_c * MLA_SCALE, sq_c, knp_c, ctx["krp"], ctx["krp"], ck_c, sq_c, v_c)
    ssd_c = _ssd_output(yf, yb, xbc, ctx["z"], p["ssd_d"], p["ssd_norm"], 0)
    c1, s0 = _swa_tables(s_c, None)
    swa_c = _swa_attention(p["swa_sink"], ctx["swq"], ctx["swq"], c1 * SWA_SCALE, s0, ctx["swk"], ctx["swv"])
    spec_c = _hyena_filters(s_c, *hy_args, dft_c[0], dft_c[1])
    hy_c = _hyena(ctx["hy"], p["hy_conv_w"], p["hy_conv_b"], spec_c, p["hy_d"], dft_c)
    xc_new = _merge([mla_c, ssd_c, swa_c, hy_c], ctx["gate"], w_branch, w_out, xc, mod_c[2], p["norm_g1"])
    return x_new, xc_new


def kernel(x, c, ctx, c_ctx, ada_w, ada_b, norm_g, w_in, mla_q_norm, mla_w_uq, mla_kv_norm, mla_w_ukv, ssd_conv_w, ssd_conv_b, ssd_dt_bias, ssd_a_log, ssd_d, ssd_norm, swa_sink, hy_conv_w, hy_conv_b, hy_w1, hy_b1, hy_w2, hy_b2, hy_w3, hy_freq, hy_d, w_branch, w_out, ffn_w_in, ffn_w_out):
    bsz, t, dm = x.shape
    s_c = ctx.shape[1]
    depth = ada_w.shape[0]
    rows = t // GRID_W
    rope_mla = _rope_tables(rows, MLA_ROPE)
    rope_swa = _rope_tables(rows, SWA_HEAD_DIM)
    dft_l = _dft_tables(t)
    dft_c = _dft_tables(s_c)
    cond_rows = 16
    cond = jnp.concatenate([c, c_ctx[None, :], jnp.zeros((cond_rows - bsz - 1, dm), F32)], axis=0)
    ffn_hidden = ffn_w_out.shape[1]
    xc = ctx
    for l in range(depth):
        ctx_out = l < depth - 1
        mod = _ada(cond, ada_w[l], ada_b[l])
        mod_l = [mod[:bsz, k * dm:(k + 1) * dm].reshape(bsz, 1, dm) for k in range(6)]
        mod_c = [jnp.broadcast_to(mod[bsz, k * dm:(k + 1) * dm].reshape(1, 1, dm), (bsz, 1, dm)) for k in range(6)]
        w = _layer_weights(w_in[l], mla_w_uq[l], mla_w_ukv[l])
        p = {"mla_q_norm": mla_q_norm[l], "mla_kv_norm": mla_kv_norm[l], "ssd_conv_w": ssd_conv_w[l],
             "ssd_conv_b": ssd_conv_b[l], "ssd_dt_bias": ssd_dt_bias[l], "ssd_a_log": ssd_a_log[l],
             "ssd_d": ssd_d[l], "ssd_norm": ssd_norm[l], "swa_sink": swa_sink[l], "hy_conv_w": hy_conv_w[l],
             "hy_conv_b": hy_conv_b[l], "hy_w1": hy_w1[l], "hy_b1": hy_b1[l], "hy_w2": hy_w2[l],
             "hy_b2": hy_b2[l], "hy_w3": hy_w3[l], "hy_freq": hy_freq[l], "hy_d": hy_d[l],
             "w_branch": w_branch[l], "w_out": w_out[l], "norm_g1": norm_g[l, 1]}
        x, xc_new = _token_mixers(x, xc, mod_l, mod_c, ctx_out, rope_mla, rope_swa, norm_g[l, 0], w, p, dft_l, dft_c)
        wg = ffn_w_in[l][:, :ffn_hidden].astype(MXU_DTYPE)
        wu = ffn_w_in[l][:, ffn_hidden:].astype(MXU_DTYPE)
        wo = ffn_w_out[l].astype(MXU_DTYPE)
        x = _ffn(x, norm_g[l, 2], mod_l[3], mod_l[4], wg, wu, wo, norm_g[l, 3], mod_l[5])
        if ctx_out:
            xc = _ffn(xc_new, norm_g[l, 2], mod_c[3], mod_c[4], wg, wu, wo, norm_g[l, 3], mod_c[5])
    return x
```

```python
import functools
import math

import jax
import jax.numpy as jnp
from jax import lax
from jax.experimental import pallas as pl
from jax.experimental.pallas import tpu as pltpu

F32 = jnp.float32
MXU_DTYPE = jnp.bfloat16
HIGHEST = lax.Precision.HIGHEST

GRID_W = 64
EPS = 1e-6
ROPE_BASE = 10000.0
MLA_HEADS, MLA_NOPE, MLA_ROPE, MLA_V = 8, 64, 32, 64
MLA_SCALE = (MLA_NOPE + MLA_ROPE) ** -0.5
SSD_HEADS, SSD_HEADDIM, SSD_GROUPS, SSD_STATE, SSD_CHUNK = 8, 64, 2, 64, 128
SSD_INNER = SSD_HEADS * SSD_HEADDIM
SWA_HEADS, SWA_KV_HEADS, SWA_HEAD_DIM, SWA_WINDOW, SWA_BLOCK = 8, 2, 64, 128, 128
SWA_SCALE = SWA_HEAD_DIM ** -0.5
HY_WIDTH, HY_ORDER, HY_BANDS, HY_HIDDEN = 512, 2, 16, 64
HY_DECAY_TARGET, HY_FAST_DECAY, HY_SLOW_DECAY = 1e-2, 0.3, 1.5
N_BRANCH = 4
LANES = 128
SSD_BATCH_ROWS = 4
VT_ROWS = 80
LOG2E = math.log2(math.e)
MLA_KEY_BLOCK = 256

VMEM_LIMIT = 56 * 1024 * 1024
_WHOLE_VMEM = pl.BlockSpec(memory_space=pltpu.VMEM)


def _params(n_grid, vmem=VMEM_LIMIT):
    return pltpu.CompilerParams(dimension_semantics=("arbitrary",) * n_grid, vmem_limit_bytes=vmem)


def _silu(x):
    return x * jax.nn.sigmoid(x)


def _softplus(x):
    return jnp.maximum(x, 0.0) + jnp.log1p(jnp.exp(-jnp.abs(x)))


def _rms(x, g):
    return x * lax.rsqrt(jnp.mean(x * x, axis=-1, keepdims=True) + EPS) * g


def _ada_kernel(s_ref, w_ref, b_ref, o_ref):
    s = _silu(s_ref[...])
    o_ref[...] = jnp.dot(s, w_ref[...], preferred_element_type=F32, precision=HIGHEST) + b_ref[...]


def _ada(cond, w, b, tn=1536):
    m, k = cond.shape
    n = w.shape[1]
    return pl.pallas_call(
        _ada_kernel,
        grid=(n // tn,),
        in_specs=[pl.BlockSpec((m, k), lambda j: (0, 0)),
                  pl.BlockSpec((k, tn), lambda j: (0, j)),
                  pl.BlockSpec((1, tn), lambda j: (0, j))],
        out_specs=pl.BlockSpec((m, tn), lambda j: (0, j)),
        out_shape=jax.ShapeDtypeStruct((m, n), F32),
        compiler_params=_params(1),
        name="ada_ln",
    )(cond, w, b.reshape(1, n))


def _norm_mm_kernel(*refs, n_w, modulate):
    it = iter(refs)
    x_ref, g_ref = next(it), next(it)
    if modulate:
        sh_ref, sc_ref = next(it), next(it)
    w_refs = [next(it) for _ in range(n_w)]
    o_refs = [next(it) for _ in range(n_w)]
    a = _rms(x_ref[...].astype(F32), g_ref[...])
    if modulate:
        a = a * (1.0 + sc_ref[...]) + sh_ref[...]
    a = a.astype(MXU_DTYPE)
    for w_ref, o_ref in zip(w_refs, o_refs):
        n = w_ref.shape[1]
        for c0 in range(0, n, 1024):
            c1 = min(n, c0 + 1024)
            o_ref[:, c0:c1] = jnp.dot(a, w_ref[:, c0:c1], preferred_element_type=F32).astype(o_ref.dtype)


def _norm_matmul(x, g, ws, out_dtypes, shift=None, scale=None, tm=256, name="norm_matmul"):
    bsz, t, k = x.shape
    tm = min(tm, t)
    modulate = shift is not None
    in_specs = [pl.BlockSpec((None, tm, k), lambda b, i: (b, i, 0)),
                pl.BlockSpec((1, k), lambda b, i: (0, 0))]
    args = [x, g.reshape(1, k).astype(F32)]
    if modulate:
        in_specs += [pl.BlockSpec((None, 1, k), lambda b, i: (b, 0, 0))] * 2
        args += [shift, scale]
    in_specs += [_WHOLE_VMEM] * len(ws)
    args += list(ws)
    return pl.pallas_call(
        functools.partial(_norm_mm_kernel, n_w=len(ws), modulate=modulate),
        grid=(bsz, t // tm),
        in_specs=in_specs,
        out_specs=[pl.BlockSpec((None, tm, w.shape[1]), lambda b, i: (b, i, 0)) for w in ws],
        out_shape=[jax.ShapeDtypeStruct((bsz, t, w.shape[1]), dt) for w, dt in zip(ws, out_dtypes)],
        compiler_params=_params(2),
        name=name,
    )(*args)


def _mla_kernel(*refs, t_l, s_c):
    if s_c:
        (qp, qr, cq, sq, knp_l, krp_l, krr_l, ck, sk, v_l, knp_c, krp_c, v_c, o_ref, kcat, vt) = refs
    else:
        (qp, qr, cq, sq, knp_l, krp_l, krr_l, ck, sk, v_l, o_ref, kcat, vt) = refs

    @pl.when(pl.program_id(1) == 0)
    def _():
        kro = (krp_l[...].astype(F32) * ck[...] + krr_l[...].astype(F32) * sk[...]).astype(kcat.dtype)
        for h in range(MLA_HEADS):
            blk = slice(h * LANES, (h + 1) * LANES)
            kcat[h, 0:t_l, :] = knp_l[:, blk] + kro
            if s_c:
                kcat[h, t_l:t_l + s_c, :] = knp_c[:, blk] + krp_c[...]
            vt[h, MLA_V:, :] = jnp.ones((VT_ROWS - MLA_V, t_l + s_c), vt.dtype)
        for j in range(MLA_HEADS // 2):
            blk = slice(j * LANES, (j + 1) * LANES)
            vp = v_l[:, blk].astype(F32).T
            vt[2 * j, 0:MLA_V, 0:t_l] = vp[0:MLA_V].astype(vt.dtype)
            vt[2 * j + 1, 0:MLA_V, 0:t_l] = vp[MLA_V:].astype(vt.dtype)
            if s_c:
                vp = v_c[:, blk].astype(F32).T
                vt[2 * j, 0:MLA_V, t_l:t_l + s_c] = vp[0:MLA_V].astype(vt.dtype)
                vt[2 * j + 1, 0:MLA_V, t_l:t_l + s_c] = vp[MLA_V:].astype(vt.dtype)

    cqv, sqv = cq[...], sq[...]
    tq = qp.shape[0]
    s_tot = t_l + s_c
    nt = (((1,), (1,)), ((), ()))
    heads = range(MLA_HEADS)
    q, m, acc = {}, {}, {}
    for h in heads:
        blk = slice(h * LANES, (h + 1) * LANES)
        q[h] = (qp[:, blk].astype(F32) * cqv + qr[:, blk].astype(F32) * sqv).astype(MXU_DTYPE)
        m[h] = jnp.full((1, tq), -1e30, F32)
        acc[h] = jnp.zeros((VT_ROWS, tq), F32)
    for k0 in range(0, s_tot, MLA_KEY_BLOCK):
        k1 = min(s_tot, k0 + MLA_KEY_BLOCK)
        s = {h: lax.dot_general(kcat[h, k0:k1, :], q[h], nt, preferred_element_type=F32) for h in heads}
        m_new = {h: jnp.maximum(m[h], jnp.max(s[h], axis=0, keepdims=True)) for h in heads}
        p = {h: jnp.exp2(s[h] - m_new[h]).astype(MXU_DTYPE) for h in heads}
        pv = {h: jnp.dot(vt[h, :, k0:k1], p[h], preferred_element_type=F32) for h in heads}
        for h in heads:
            acc[h] = acc[h] * jnp.exp2(m[h] - m_new[h]) + pv[h]
            m[h] = m_new[h]
    for j in range(MLA_HEADS // 2):
        halves = [acc[h][0:MLA_V] * (1.0 / acc[h][MLA_V:MLA_V + 1]) for h in (2 * j, 2 * j + 1)]
        o_ref[:, j * LANES:(j + 1) * LANES] = jnp.concatenate(halves, axis=0).T.astype(o_ref.dtype)


def _mla_attention(qp, qr, cq, sq, knp_l, krp_l, krr_l, ck, sk, v_l, ctx=None, tq=256):
    bsz, t_q, _ = qp.shape
    t_l = knp_l.shape[1]
    s_c = 0 if ctx is None else ctx[0].shape[1]
    tq = min(tq, t_q)
    hw = MLA_HEADS * LANES
    vw = MLA_HEADS * MLA_V
    per_b = lambda rows, cols: pl.BlockSpec((None, rows, cols), lambda b, i: (b, 0, 0))
    in_specs = [pl.BlockSpec((None, tq, hw), lambda b, i: (b, i, 0)),
                pl.BlockSpec((None, tq, hw), lambda b, i: (b, i, 0)),
                pl.BlockSpec((tq, LANES), lambda b, i: (i, 0)),
                pl.BlockSpec((tq, LANES), lambda b, i: (i, 0)),
                per_b(t_l, hw), per_b(t_l, LANES), per_b(t_l, LANES),
                pl.BlockSpec((t_l, LANES), lambda b, i: (0, 0)),
                pl.BlockSpec((t_l, LANES), lambda b, i: (0, 0)),
                per_b(t_l, vw)]
    args = [qp, qr, cq, sq, knp_l, krp_l, krr_l, ck, sk, v_l]
    if s_c:
        in_specs += [per_b(s_c, hw), per_b(s_c, LANES), per_b(s_c, vw)]
        args += list(ctx)
    return pl.pallas_call(
        functools.partial(_mla_kernel, t_l=t_l, s_c=s_c),
        grid=(bsz, t_q // tq),
        in_specs=in_specs,
        out_specs=pl.BlockSpec((None, tq, vw), lambda b, i: (b, i, 0)),
        out_shape=jax.ShapeDtypeStruct((bsz, t_q, vw), MXU_DTYPE),
        scratch_shapes=[pltpu.VMEM((MLA_HEADS, t_l + s_c, LANES), MXU_DTYPE),
                        pltpu.VMEM((MLA_HEADS, VT_ROWS, t_l + s_c), MXU_DTYPE)],
        compiler_params=_params(2),
        name="mla_attention",
    )(*args)


def _swa_kernel(*refs, band, t_k):
    if band:
        (sink, q, qrot, cq, sq, k, krot, ck, sk, v, kc, vc, o_ref, kro, vt, vct) = refs
    else:
        (sink, q, qrot, cq, sq, kc, vc, o_ref, vct) = refs
    i = pl.program_id(1)
    tq = q.shape[0]
    gw = 2 * SWA_HEAD_DIM
    dh = SWA_HEAD_DIM
    kb_rows = tq + 2 * SWA_BLOCK

    @pl.when(i == 0)
    def _():
        def put_vt(dst, src):
            vtr = src[...].astype(F32).T
            for g in range(SWA_KV_HEADS):
                dst[g, 0:dh, :] = vtr[g * dh:(g + 1) * dh].astype(dst.dtype)
                dst[g, dh:, :] = jnp.ones((VT_ROWS - dh, src.shape[0]), dst.dtype)

        put_vt(vct, vc)
        if band:
            put_vt(vt, v)
            ckv, skv = ck[...], sk[...]
            for g in range(SWA_KV_HEADS):
                blk = slice(g * gw, (g + 1) * gw)
                kro[:, blk] = (k[:, blk].astype(F32) * ckv + krot[:, blk].astype(F32) * skv).astype(kro.dtype)

    if band:
        start = pl.multiple_of(jnp.clip(i * tq - SWA_BLOCK, 0, t_k - kb_rows), SWA_BLOCK)
        k_pos = start + lax.broadcasted_iota(jnp.int32, (kb_rows, tq), 0)
        q_pos = i * tq + lax.broadcasted_iota(jnp.int32, (kb_rows, tq), 1)
        in_band = jnp.abs(q_pos - k_pos) <= SWA_WINDOW

    lane_lo = lax.broadcasted_iota(jnp.int32, (tq, LANES), 1) < dh
    cqv, sqv = cq[...], sq[...]
    nt = (((1,), (1,)), ((), ()))
    heads = range(SWA_HEADS)
    group = {h: h // (SWA_HEADS // SWA_KV_HEADS) for h in heads}
    gblk = {h: slice(group[h] * gw, (group[h] + 1) * gw) for h in heads}
    qm, snk = {}, {}
    for j in range(SWA_HEADS // 2):
        blk = slice(j * LANES, (j + 1) * LANES)
        qro = q[:, blk].astype(F32) * cqv + qrot[:, blk].astype(F32) * sqv
        qm[2 * j] = jnp.where(lane_lo, qro, 0.0).astype(MXU_DTYPE)
        qm[2 * j + 1] = jnp.where(lane_lo, 0.0, qro).astype(MXU_DTYPE)
    for h in heads:
        snk[h] = sink[h] * LOG2E
    s_c = {h: lax.dot_general(kc[:, gblk[h]], qm[h], nt, preferred_element_type=F32) for h in heads}
    m = {h: jnp.maximum(jnp.max(s_c[h], axis=0, keepdims=True), snk[h]) for h in heads}
    if band:
        s_b = {h: jnp.where(in_band, lax.dot_general(kro[pl.ds(start, kb_rows), gblk[h]], qm[h], nt,
                                                     preferred_element_type=F32), -1e30) for h in heads}
        m = {h: jnp.maximum(m[h], jnp.max(s_b[h], axis=0, keepdims=True)) for h in heads}
    o = {h: jnp.dot(vct[group[h]], jnp.exp2(s_c[h] - m[h]).astype(MXU_DTYPE), preferred_element_type=F32)
         for h in heads}
    if band:
        o = {h: o[h] + jnp.dot(vt[group[h], :, pl.ds(start, kb_rows)], jnp.exp2(s_b[h] - m[h]).astype(MXU_DTYPE),
                               preferred_element_type=F32) for h in heads}
    for j in range(SWA_HEADS // 2):
        halves = []
        for h in (2 * j, 2 * j + 1):
            l = o[h][dh:dh + 1] + jnp.exp2(snk[h] - m[h])
            halves.append(o[h][0:dh] * (1.0 / l))
        o_ref[:, j * LANES:(j + 1) * LANES] = jnp.concatenate(halves, axis=0).T.astype(o_ref.dtype)


def _swa_attention(sink, q, qrot, cq, sq, kc, vc, lat=None):
    bsz, t, hw = q.shape
    s_c = kc.shape[1]
    tq = min(t, 2 * SWA_BLOCK)
    kw, vw = kc.shape[2], vc.shape[2]
    band = lat is not None
    per_b = lambda rows, cols: pl.BlockSpec((None, rows, cols), lambda b, i: (b, 0, 0))
    qspec = pl.BlockSpec((None, tq, hw), lambda b, i: (b, i, 0))
    tspec = pl.BlockSpec((tq, LANES), lambda b, i: (i, 0))
    in_specs = [pl.BlockSpec(memory_space=pltpu.SMEM), qspec, qspec, tspec, tspec]
    args = [sink.astype(F32), q, qrot, cq, sq]
    scratch = []
    if band:
        k, krot, ck, sk, v = lat
        full_t = pl.BlockSpec((t, LANES), lambda b, i: (0, 0))
        in_specs += [per_b(t, kw), per_b(t, kw), full_t, full_t, per_b(t, vw)]
        args += [k, krot, ck, sk, v]
        scratch = [pltpu.VMEM((t, kw), MXU_DTYPE), pltpu.VMEM((SWA_KV_HEADS, VT_ROWS, t), MXU_DTYPE)]
    scratch += [pltpu.VMEM((SWA_KV_HEADS, VT_ROWS, s_c), MXU_DTYPE)]
    in_specs += [per_b(s_c, kw), per_b(s_c, vw)]
    args += [kc, vc]
    return pl.pallas_call(
        functools.partial(_swa_kernel, band=band, t_k=t),
        grid=(bsz, t // tq),
        in_specs=in_specs,
        out_specs=pl.BlockSpec((None, tq, hw), lambda b, i: (b, i, 0)),
        out_shape=jax.ShapeDtypeStruct((bsz, t, hw), MXU_DTYPE),
        scratch_shapes=scratch,
        compiler_params=_params(2),
        name="swa_attention" if band else "ctx_sink_attention",
    )(*args)


def _dwconv_kernel(*refs, n_seg, act):
    x_refs = refs[:n_seg]
    w_ref, b_ref, o_ref = refs[n_seg:]
    taps = w_ref.shape[0]
    off = 0
    for x_ref in x_refs:
        t = x_ref.shape[0]
        x = x_ref[...].astype(F32)
        row = lax.broadcasted_iota(jnp.int32, x.shape, 0)
        acc = jnp.zeros_like(x) + b_ref[...]
        for kk in range(taps):
            d = kk - taps // 2
            if d == 0:
                xs = x
            else:
                xs = pltpu.roll(x, (-d) % t, axis=0)
                xs = jnp.where((row + d >= 0) & (row + d < t), xs, 0.0)
            acc = acc + xs * w_ref[kk:kk + 1, :]
        if act:
            acc = _silu(acc)
        o_ref[off:off + t, :] = acc.astype(o_ref.dtype)
        off += t


def _dwconv(xs, w, b, act, tc=256, name="dwconv"):
    bsz, _, c = xs[0].shape
    t_tot = sum(x.shape[1] for x in xs)
    taps = w.shape[0]
    in_specs = [pl.BlockSpec((None, x.shape[1], tc), lambda bb, j: (bb, 0, j)) for x in xs]
    in_specs += [pl.BlockSpec((taps, tc), lambda bb, j: (0, j)), pl.BlockSpec((1, tc), lambda bb, j: (0, j))]
    return pl.pallas_call(
        functools.partial(_dwconv_kernel, n_seg=len(xs), act=act),
        grid=(bsz, c // tc),
        in_specs=in_specs,
        out_specs=pl.BlockSpec((None, t_tot, tc), lambda bb, j: (bb, 0, j)),
        out_shape=jax.ShapeDtypeStruct((bsz, t_tot, c), F32),
        compiler_params=_params(2),
        name=name,
    )(*xs, w.astype(F32), b.reshape(1, c).astype(F32))


def _ssd_scan_kernel(xs_f, bm_f, cm_f, dt_f, dtt_f, xs_b, bm_b, cm_b, dt_b, dtt_b,
                     b_row, al_row, b_col, al_col, yf_ref, yb_ref, st_ref):
    @pl.when(pl.program_id(1) == 0)
    def _():
        st_ref[...] = jnp.zeros_like(st_ref)

    q = SSD_CHUNK
    row = lax.broadcasted_iota(jnp.int32, (q, q), 0)
    col = lax.broadcasted_iota(jnp.int32, (q, q), 1)
    lane_lo = col < SSD_HEADDIM
    a_row = -jnp.exp(al_row[...])
    a_col = -jnp.exp(al_col[...])
    nt = (((1,), (1,)), ((), ()))
    tn = (((0,), (0,)), ((), ()))
    dirs = ((xs_f, bm_f, cm_f, dt_f, dtt_f, yf_ref), (xs_b, bm_b, cm_b, dt_b, dtt_b, yb_ref))
    units = [(bi, d) for bi in range(st_ref.shape[0]) for d in range(2)]
    pairs_per_group = SSD_HEADS // SSD_GROUPS // 2
    feeds = {0: row >= col, 1: row <= col}
    last = {0: q - 1, 1: 0}
    dt, dtt, cs, cst, cg, bg, gmat, sc, ecol, wcol, dec = ({} for _ in range(11))
    for u in units:
        bi, d = u
        dt_ref, dtt_ref = dirs[d][3], dirs[d][4]
        dt[u] = _softplus(dt_ref[bi] + b_row[...])
        dtt[u] = _softplus(dtt_ref[bi] + b_col[...])
    for u in units:
        d = u[1]
        cs[u] = jnp.dot(feeds[d].astype(F32), dt[u] * a_row, preferred_element_type=F32, precision=HIGHEST)
        cst[u] = jnp.dot(dtt[u] * a_col, feeds[1 - d].astype(F32), preferred_element_type=F32, precision=HIGHEST)
    for u in units:
        bi, d = u
        bblk, cblk = dirs[d][1][bi], dirs[d][2][bi]
        for g in range(SSD_GROUPS):
            gmask = (col // SSD_STATE) == g
            cg[u, g] = jnp.where(gmask, cblk, 0.0).astype(MXU_DTYPE)
            bg[u, g] = jnp.where(gmask, bblk, 0.0)
            gmat[u, g] = lax.dot_general(cg[u, g], bg[u, g].astype(MXU_DTYPE), nt,
                                         preferred_element_type=F32)
    for u in units:
        d = u[1]
        tot = cs[u][last[d]:last[d] + 1, :]
        e_all = jnp.exp(cs[u])
        w_all = dt[u] * jnp.exp(tot - cs[u])
        d_all = jnp.exp(tot)
        for h in range(SSD_HEADS):
            c = d * SSD_HEADS + h
            ccol = cs[u][:, c:c + 1]
            crow = cst[u][c:c + 1, :]
            lmat = jnp.where(feeds[d], jnp.exp(ccol - crow), 0.0)
            sc[u, h] = (gmat[u, h // (SSD_HEADS // SSD_GROUPS)] * lmat * dtt[u][c:c + 1, :]).astype(MXU_DTYPE)
            ecol[u, h] = e_all[:, c:c + 1]
            wcol[u, h] = w_all[:, c:c + 1]
            dec[u, h] = d_all[:, c:c + 1]
    for u in units:
        bi, d = u
        xs_ref, y_ref = dirs[d][0], dirs[d][5]
        for j in range(SSD_HEADS // 2):
            g = j // pairs_per_group
            xblk = xs_ref[bi, :, j * LANES:(j + 1) * LANES]
            xb = xblk.astype(MXU_DTYPE)
            s_in = st_ref[bi, d, j]
            y_inter = jnp.dot(cg[u, g], s_in.astype(MXU_DTYPE), preferred_element_type=F32)
            ys, news = [], []
            for hh in range(2):
                h = 2 * j + hh
                ys.append(jnp.dot(sc[u, h], xb, preferred_element_type=F32) + ecol[u, h] * y_inter)
                bw = (bg[u, g] * wcol[u, h]).astype(MXU_DTYPE)
                xh = jnp.where(lane_lo if hh == 0 else ~lane_lo, xblk, 0.0).astype(MXU_DTYPE)
                news.append(lax.dot_general(bw, xh, tn, preferred_element_type=F32))
            y_ref[bi, :, j * LANES:(j + 1) * LANES] = jnp.where(lane_lo, ys[0], ys[1])
            st_ref[bi, d, j] = (s_in * jnp.where(lane_lo, dec[u, 2 * j], dec[u, 2 * j + 1])
                                + news[0] + news[1])


def _ssd_scan(xbc, dt_raw, dt_raw_t, dt_bias, a_log, nc_ctx):
    bsz, t_c, _ = xbc.shape
    q = SSD_CHUNK
    nc = t_c // q
    nh2 = 2 * SSD_HEADS

    def fwd(s):
        return s

    def bwd(s):
        return jnp.where(s < nc_ctx, nc_ctx - 1 - s, nc + nc_ctx - 1 - s)

    bt = math.gcd(bsz, SSD_BATCH_ROWS)

    def specs(order):
        return [pl.BlockSpec((bt, q, SSD_INNER), lambda b, s: (b, order(s), 0)),
                pl.BlockSpec((bt, q, LANES), lambda b, s: (b, order(s), SSD_INNER // LANES)),
                pl.BlockSpec((bt, q, LANES), lambda b, s: (b, order(s), SSD_INNER // LANES + 1)),
                pl.BlockSpec((bt, q, nh2), lambda b, s: (b, order(s), 0)),
                pl.BlockSpec((bt, nh2, q), lambda b, s: (b, 0, order(s)))]

    small = lambda r, c: pl.BlockSpec((r, c), lambda b, s: (0, 0))
    bias = dt_bias.astype(F32).reshape(1, nh2)
    alog = a_log.astype(F32).reshape(1, nh2)
    return pl.pallas_call(
        _ssd_scan_kernel,
        grid=(bsz // bt, nc),
        in_specs=specs(fwd) + specs(bwd) + [small(1, nh2), small(1, nh2), small(nh2, 1), small(nh2, 1)],
        out_specs=[pl.BlockSpec((bt, q, SSD_INNER), lambda b, s: (b, fwd(s), 0)),
                   pl.BlockSpec((bt, q, SSD_INNER), lambda b, s: (b, bwd(s), 0))],
        out_shape=[jax.ShapeDtypeStruct((bsz, t_c, SSD_INNER), F32)] * 2,
        scratch_shapes=[pltpu.VMEM((bt, 2, SSD_HEADS // 2, LANES, LANES), F32)],
        compiler_params=_params(2),
        name="ssd_scan",
    )(xbc, xbc, xbc, dt_raw, dt_raw_t, xbc, xbc, xbc, dt_raw, dt_raw_t,
      bias, alog, bias.reshape(nh2, 1), alog.reshape(nh2, 1))


def _ssd_out_kernel(yf, yb, xs, z, dexp, ng, o_ref):
    y = yf[...] + yb[...] + dexp[...] * xs[...]
    yg = y * _silu(z[...].astype(F32))
    gw = SSD_INNER // SSD_GROUPS
    for g in range(SSD_GROUPS):
        blk = slice(g * gw, (g + 1) * gw)
        o_ref[:, blk] = _rms(yg[:, blk], ng[:, blk]).astype(o_ref.dtype)


def _ssd_output(yf, yb, xbc, z, d_skip, norm_g, row0, tm=256):
    bsz, t, _ = z.shape
    tm = min(tm, t)
    off = row0 // tm
    sp_y = pl.BlockSpec((None, tm, SSD_INNER), lambda b, i: (b, i + off, 0))
    vec = pl.BlockSpec((1, SSD_INNER), lambda b, i: (0, 0))
    dexp = jnp.repeat(d_skip.astype(F32), SSD_HEADDIM).reshape(1, SSD_INNER)
    return pl.pallas_call(
        _ssd_out_kernel,
        grid=(bsz, t // tm),
        in_specs=[sp_y, sp_y, sp_y, pl.BlockSpec((None, tm, SSD_INNER), lambda b, i: (b, i, 0)), vec, vec],
        out_specs=pl.BlockSpec((None, tm, SSD_INNER), lambda b, i: (b, i, 0)),
        out_shape=jax.ShapeDtypeStruct((bsz, t, SSD_INNER), MXU_DTYPE),
        compiler_params=_params(2),
        name="ssd_output",
    )(yf, yb, xbc, z, dexp, norm_g.astype(F32).reshape(1, SSD_INNER))


def _hy_mlp_kernel(feat_ref, w1, b1, w2, b2, w3, fr, delta, h_ref, cs_ref):
    feat = feat_ref[...]
    h = jnp.sin(fr[...] * (jnp.dot(feat, w1[...], preferred_element_type=F32, precision=HIGHEST) + b1[...]))
    h = jnp.sin(fr[...] * (jnp.dot(h, w2[...], preferred_element_type=F32, precision=HIGHEST) + b2[...]))
    h = jnp.dot(h, w3[...], preferred_element_type=F32, precision=HIGHEST)
    h = h * jnp.exp(-feat[:, 0:1] * delta[...])
    h_ref[...] = h
    s = jnp.sum(jnp.abs(h), axis=0, keepdims=True)

    @pl.when(pl.program_id(0) == 0)
    def _():
        cs_ref[...] = s

    @pl.when(pl.program_id(0) != 0)
    def _():
        cs_ref[...] += s


def _hy_spectrum_kernel(h0_ref, h1_ref, c0_ref, c1_ref, cm_ref, sf_ref, ar_ref, ai_ref, br_ref):
    n = h0_ref.shape[0]
    inv = 1.0 / (c0_ref[...] + c1_ref[...])
    row = lax.broadcasted_iota(jnp.int32, h0_ref.shape, 0)
    first = row == 0
    h0 = h0_ref[...] * inv
    h1 = jnp.where(first, 0.0, h1_ref[...] * inv)
    a = h0 + h1
    kr = jnp.dot(cm_ref[...], a.astype(MXU_DTYPE), preferred_element_type=F32)
    kq = jnp.dot(sf_ref[...], (h1 - h0).astype(MXU_DTYPE), preferred_element_type=F32)
    k_nyq = jnp.sum(jnp.where(row % 2 == 0, a, -a), axis=0, keepdims=True)
    inv_n = 1.0 / (2 * n)
    ar = kr * jnp.where(first, inv_n, 2.0 * inv_n)
    ar_ref[...] = ar
    ai_ref[...] = jnp.where(first, 0.0, kq * (2.0 * inv_n))
    br_ref[...] = jnp.where(first, k_nyq * inv_n, ar)


def _hyena_filters(n, w1, b1, w2, b2, w3, freq, cm, sf):
    t = jnp.arange(n, dtype=F32)
    tnorm = t / n
    bands = jnp.linspace(1e-4, HY_BANDS - 1, HY_BANDS, dtype=F32)
    ang = 2 * math.pi * t[:, None] * bands[None, :] / n
    feat = jnp.concatenate([tnorm[:, None], jnp.cos(ang), -jnp.sin(ang)], axis=-1)
    emb = feat.shape[1]
    feat = jnp.pad(feat, ((0, 0), (0, HY_HIDDEN - emb)))
    w1p = jnp.pad(w1.astype(F32), ((0, HY_HIDDEN - emb), (0, 0)))
    deltas = jnp.abs(jnp.linspace(math.log(HY_DECAY_TARGET) / HY_SLOW_DECAY,
                                  math.log(HY_DECAY_TARGET) / HY_FAST_DECAY, HY_WIDTH, dtype=F32))
    ncol = HY_ORDER * 2 * HY_WIDTH
    delta_row = jnp.tile(deltas, HY_ORDER * 2).reshape(1, ncol)
    tt = min(n, 256)
    small = lambda r, c: pl.BlockSpec((r, c), lambda i: (0, 0))
    row = lambda v: v.astype(F32).reshape(1, -1)
    h, colsum = pl.pallas_call(
        _hy_mlp_kernel,
        grid=(n // tt,),
        in_specs=[pl.BlockSpec((tt, HY_HIDDEN), lambda i: (i, 0)),
                  small(HY_HIDDEN, HY_HIDDEN), small(1, HY_HIDDEN),
                  small(HY_HIDDEN, HY_HIDDEN), small(1, HY_HIDDEN),
                  small(HY_HIDDEN, ncol), small(1, HY_HIDDEN), small(1, ncol)],
        out_specs=[pl.BlockSpec((tt, ncol), lambda i: (i, 0)), small(1, ncol)],
        out_shape=[jax.ShapeDtypeStruct((n, ncol), F32), jax.ShapeDtypeStruct((1, ncol), F32)],
        compiler_params=_params(1),
        name="hyena_filter_mlp",
    )(feat, w1p, row(b1), w2.astype(F32), row(b2), w3.astype(F32), row(freq), delta_row)

    tc = 256
    per_o = HY_WIDTH // tc
    side0 = lambda jc: (jc // per_o) * 2 * per_o + jc % per_o
    side1 = lambda jc: (jc // per_o) * 2 * per_o + per_o + jc % per_o
    nout = HY_ORDER * HY_WIDTH
    out_spec = pl.BlockSpec((n, tc), lambda jc: (0, jc))
    return pl.pallas_call(
        _hy_spectrum_kernel,
        grid=(nout // tc,),
        in_specs=[pl.BlockSpec((n, tc), lambda jc: (0, side0(jc))),
                  pl.BlockSpec((n, tc), lambda jc: (0, side1(jc))),
                  pl.BlockSpec((1, tc), lambda jc: (0, side0(jc))),
                  pl.BlockSpec((1, tc), lambda jc: (0, side1(jc))),
                  _WHOLE_VMEM, _WHOLE_VMEM],
        out_specs=[out_spec] * 3,
        out_shape=[jax.ShapeDtypeStruct((n, nout), F32)] * 3,
        compiler_params=_params(1),
        name="hyena_filter_spectrum",
    )(h, h, colsum, colsum, cm, sf)


def _hy_fwd_kernel(u_ref, cm_ref, sf_ref, ar_ref, ai_ref, br_ref, yr_ref, yi_ref):
    u = u_ref[...].astype(MXU_DTYPE)
    p = jnp.dot(cm_ref[...], u, preferred_element_type=F32)
    q = jnp.dot(sf_ref[...], u, preferred_element_type=F32)
    ai = ai_ref[...]
    yr_ref[...] = (p * ar_ref[...] + q * ai).astype(yr_ref.dtype)
    yi_ref[...] = (q * br_ref[...] - p * ai).astype(yi_ref.dtype)


def _hy_inv_kernel(yr_ref, yi_ref, cm_ref, si_ref, u_ref, xg_ref, d_ref, o_ref):
    y = (jnp.dot(cm_ref[...], yr_ref[...], preferred_element_type=F32)
         + jnp.dot(si_ref[...], yi_ref[...], preferred_element_type=F32))
    u = u_ref[...].astype(F32)
    o_ref[...] = (xg_ref[...].astype(F32) * (y + u * d_ref[...])).astype(o_ref.dtype)


def _hyena_conv(u, u_col0, xg, xg_col0, spectra, order, d, tables, out_dtype, tc=256):
    cm, sf, si = tables
    ar, ai, br = spectra
    bsz, n, _ = u.shape
    nct = HY_WIDTH // tc
    ucol, gcol, scol = u_col0 // tc, xg_col0 // tc, order * nct
    tok = lambda c0: pl.BlockSpec((None, n, tc), lambda c, b: (b, 0, c0 + c))
    spec_sp = pl.BlockSpec((n, tc), lambda c, b: (0, scol + c))
    mid = pl.BlockSpec((None, n, tc), lambda c, b: (b, 0, c))
    yr, yi = pl.pallas_call(
        _hy_fwd_kernel,
        grid=(nct, bsz),
        in_specs=[tok(ucol), _WHOLE_VMEM, _WHOLE_VMEM, spec_sp, spec_sp, spec_sp],
        out_specs=[mid, mid],
        out_shape=[jax.ShapeDtypeStruct((bsz, n, HY_WIDTH), MXU_DTYPE)] * 2,
        compiler_params=_params(2),
        name="hyena_dft_forward",
    )(u, cm, sf, ar, ai, br)
    return pl.pallas_call(
        _hy_inv_kernel,
        grid=(nct, bsz),
        in_specs=[mid, mid, _WHOLE_VMEM, _WHOLE_VMEM, tok(ucol), tok(gcol),
                  pl.BlockSpec((1, tc), lambda c, b: (0, c))],
        out_specs=mid,
        out_shape=jax.ShapeDtypeStruct((bsz, n, HY_WIDTH), out_dtype),
        compiler_params=_params(2),
        name="hyena_dft_inverse",
    )(yr, yi, cm, si, u, xg, d.astype(F32).reshape(1, HY_WIDTH))


def _dft_tables(n):
    idx = jnp.arange(n, dtype=jnp.int32)
    prod = (idx[:, None] * idx[None, :]) % (2 * n)
    ang = prod.astype(F32) * (math.pi / n)
    alt = jnp.where(idx % 2 == 0, 1.0, -1.0).astype(F32)
    cm = jnp.cos(ang)
    sf = jnp.where(idx[:, None] == 0, alt[None, :], jnp.sin(ang))
    return cm.astype(MXU_DTYPE), sf.astype(MXU_DTYPE), sf.T.astype(MXU_DTYPE)


def _hyena(hy, conv_w, conv_b, spectra, d, tables):
    u3 = _dwconv([hy], conv_w, conv_b, act=False, name="hyena_short_conv")
    z = _hyena_conv(u3, 0, u3, HY_WIDTH, spectra, 0, d[0], tables, F32)
    return _hyena_conv(z, 0, u3, 2 * HY_WIDTH, spectra, 1, d[1], tables, MXU_DTYPE)


def _merge_kernel(b0, b1, b2, b3, gate_ref, wb_ref, wo_ref, x_ref, mg_ref, g_ref, o_ref):
    d = x_ref.shape[1]
    acc = None
    for i, br in enumerate((b0, b1, b2, b3)):
        proj = jnp.dot(br[...], wb_ref[i], preferred_element_type=F32)
        term = jax.nn.sigmoid(gate_ref[:, i * d:(i + 1) * d].astype(F32)) * proj
        acc = term if acc is None else acc + term
    y = jnp.dot(acc.astype(MXU_DTYPE), wo_ref[...], preferred_element_type=F32)
    o_ref[...] = x_ref[...] + mg_ref[...] * _rms(y, g_ref[...])


def _merge(branches, gate, w_branch, w_out, x, mod_gate, g, tm=256):
    bsz, t, d = x.shape
    tm = min(tm, t)
    bw = branches[0].shape[2]
    tok = lambda w: pl.BlockSpec((None, tm, w), lambda b, i: (b, i, 0))
    return pl.pallas_call(
        _merge_kernel,
        grid=(bsz, t // tm),
        in_specs=[tok(bw)] * N_BRANCH + [tok(N_BRANCH * d), _WHOLE_VMEM, _WHOLE_VMEM, tok(d),
                                         pl.BlockSpec((None, 1, d), lambda b, i: (b, 0, 0)),
                                         pl.BlockSpec((1, d), lambda b, i: (0, 0))],
        out_specs=tok(d),
        out_shape=jax.ShapeDtypeStruct((bsz, t, d), F32),
        compiler_params=_params(2),
        name="merge_branches",
    )(*branches, gate, w_branch, w_out, x, mod_gate, g.astype(F32).reshape(1, d))


def _ffn_kernel(x_ref, g_in, sh_ref, sc_ref, wg_ref, wu_ref, wo_ref, g_out, mg_ref, o_ref, *, chunk):
    x = x_ref[...]
    a = (_rms(x, g_in[...]) * (1.0 + sc_ref[...]) + sh_ref[...]).astype(MXU_DTYPE)
    hidden = wg_ref.shape[1]
    acc = None
    for c0 in range(0, hidden, chunk):
        gch = jnp.dot(a, wg_ref[:, c0:c0 + chunk], preferred_element_type=F32)
        uch = jnp.dot(a, wu_ref[:, c0:c0 + chunk], preferred_element_type=F32)
        act = (_silu(gch) * uch).astype(MXU_DTYPE)
        part = jnp.dot(act, wo_ref[c0:c0 + chunk, :], preferred_element_type=F32)
        acc = part if acc is None else acc + part
    o_ref[...] = x + mg_ref[...] * _rms(acc, g_out[...])


def _ffn(x, g_in, shift, scale, w_gate, w_up, w_out, g_out, mod_gate, tm=256):
    bsz, t, d = x.shape
    tm = min(tm, t)
    hidden = w_gate.shape[1]
    chunk = hidden // 2
    tok = pl.BlockSpec((None, tm, d), lambda b, i: (b, i, 0))
    per_b = pl.BlockSpec((None, 1, d), lambda b, i: (b, 0, 0))
    vec = pl.BlockSpec((1, d), lambda b, i: (0, 0))
    return pl.pallas_call(
        functools.partial(_ffn_kernel, chunk=chunk),
        grid=(bsz, t // tm),
        in_specs=[tok, vec, per_b, per_b, _WHOLE_VMEM, _WHOLE_VMEM, _WHOLE_VMEM, vec, per_b],
        out_specs=tok,
        out_shape=jax.ShapeDtypeStruct((bsz, t, d), F32),
        compiler_params=_params(2),
        name="swiglu_ffn",
    )(x, g_in.astype(F32).reshape(1, d), shift, scale, w_gate, w_up, w_out,
      g_out.astype(F32).reshape(1, d), mod_gate)


def _rot_cols(w, half):
    return jnp.concatenate([-w[:, half:], w[:, :half]], axis=1)


def _rot_heads(w, heads, dim):
    k = w.shape[0]
    w3 = w.reshape(k, heads, dim)
    return jnp.concatenate([-w3[:, :, dim // 2:], w3[:, :, :dim // 2]], axis=2).reshape(k, heads * dim)


def _dup_heads(w, heads, dim):
    k = w.shape[0]
    w3 = w.reshape(k, heads, 1, dim)
    return jnp.broadcast_to(w3, (k, heads, 2, dim)).reshape(k, heads * 2 * dim)


def _rope_tables(rows, dim):
    row = jnp.repeat(jnp.arange(rows, dtype=F32), GRID_W)
    col = jnp.tile(jnp.arange(GRID_W, dtype=F32), rows)
    n_freq = dim // 4
    inv = ROPE_BASE ** (-jnp.arange(n_freq, dtype=F32) / n_freq)
    ang = jnp.concatenate([row[:, None] * inv, col[:, None] * inv], axis=-1)
    return jnp.cos(ang), jnp.sin(ang)


def _mla_tables(t, rope):
    ones = jnp.ones((t, MLA_NOPE), F32)
    zeros_n = jnp.zeros((t, MLA_NOPE), F32)
    pad = jnp.zeros((t, LANES - MLA_NOPE - MLA_ROPE), F32)
    if rope is None:
        c = jnp.ones((t, MLA_ROPE), F32)
        s = jnp.zeros((t, MLA_ROPE), F32)
    else:
        c = jnp.concatenate([rope[0], rope[0]], axis=1)
        s = jnp.concatenate([rope[1], rope[1]], axis=1)
    plain_q = jnp.concatenate([ones, c, pad], axis=1)
    rot = jnp.concatenate([zeros_n, s, pad], axis=1)
    plain_k = jnp.concatenate([zeros_n, c, pad], axis=1)
    return plain_q, rot, plain_k


def _swa_tables(t, rope):
    if rope is None:
        return jnp.ones((t, LANES), F32), jnp.zeros((t, LANES), F32)
    c = jnp.concatenate([rope[0]] * 4, axis=1)
    s = jnp.concatenate([rope[1]] * 4, axis=1)
    return c, s


def _layer_weights(w_in, mla_w_uq, mla_w_ukv):
    dm = w_in.shape[0]
    sizes = (384, 256, MLA_ROPE, SSD_INNER, SSD_INNER + 2 * SSD_GROUPS * SSD_STATE, 2 * SSD_HEADS,
             SWA_HEADS * SWA_HEAD_DIM, 2 * SWA_KV_HEADS * SWA_HEAD_DIM, 3 * HY_WIDTH, N_BRANCH * dm)
    offs = [0]
    for s in sizes:
        offs.append(offs[-1] + s)
    seg = lambda i: w_in[:, offs[i]:offs[i + 1]]
    cast = lambda w: w.astype(MXU_DTYPE)
    pad_kr = lambda w: jnp.pad(w, ((0, 0), (MLA_NOPE, LANES - MLA_NOPE - MLA_ROPE)))
    w_kr = seg(2)
    kvw = SWA_KV_HEADS * SWA_HEAD_DIM
    w_swk, w_swv = seg(7)[:, :kvw], seg(7)[:, kvw:]
    w = {
        "qa": cast(seg(0)), "ckv": cast(seg(1)),
        "krp": cast(pad_kr(w_kr)), "krr": cast(pad_kr(_rot_cols(w_kr, MLA_ROPE // 2))),
        "z": cast(seg(3)), "xbc": cast(seg(4)), "dt": cast(seg(5)),
        "swq": cast(seg(6)), "swq_rot": cast(_rot_heads(seg(6), SWA_HEADS, SWA_HEAD_DIM)),
        "swk": cast(_dup_heads(w_swk, SWA_KV_HEADS, SWA_HEAD_DIM)),
        "swk_rot": cast(_dup_heads(_rot_heads(w_swk, SWA_KV_HEADS, SWA_HEAD_DIM), SWA_KV_HEADS, SWA_HEAD_DIM)),
        "swv": cast(w_swv),
        "hy": cast(seg(8)), "gate": cast(seg(9)),
    }
    kq = mla_w_uq.shape[0]
    dq = MLA_NOPE + MLA_ROPE
    uq = mla_w_uq.reshape(kq, MLA_HEADS, dq)
    padq = ((0, 0), (0, 0), (0, LANES - dq))
    w["uq"] = cast(jnp.pad(uq, padq).reshape(kq, MLA_HEADS * LANES))
    uq_rope = uq[:, :, MLA_NOPE:]
    uq_rot = jnp.concatenate([-uq_rope[:, :, MLA_ROPE // 2:], uq_rope[:, :, :MLA_ROPE // 2]], axis=2)
    uq_rot = jnp.pad(uq_rot, ((0, 0), (0, 0), (MLA_NOPE, LANES - dq)))
    w["uq_rot"] = cast(uq_rot.reshape(kq, MLA_HEADS * LANES))
    kk = mla_w_ukv.shape[0]
    ukv = mla_w_ukv.reshape(kk, MLA_HEADS, MLA_NOPE + MLA_V)
    w["uk"] = cast(jnp.pad(ukv[:, :, :MLA_NOPE], ((0, 0), (0, 0), (0, LANES - MLA_NOPE))).reshape(kk, MLA_HEADS * LANES))
    w["uv"] = cast(ukv[:, :, MLA_NOPE:].reshape(kk, MLA_HEADS * MLA_V))
    return w


def _token_mixers(x, xc, mod_l, mod_c, ctx_out, rope_mla, rope_swa, norm_g0, w, p, dft_l, dft_c):
    bsz, t, dm = x.shape
    s_c = xc.shape[1]
    bf = MXU_DTYPE

    lat_names = ["qa", "ckv", "krp", "krr", "z", "xbc", "dt", "swq", "swq_rot", "swk", "swk_rot", "swv", "hy", "gate"]
    lat_dt = [bf, bf, bf, bf, bf, bf, F32, bf, bf, bf, bf, bf, bf, bf]
    lat = dict(zip(lat_names, _norm_matmul(x, norm_g0, [w[k] for k in lat_names], lat_dt,
                                           shift=mod_l[0], scale=mod_l[1], name="in_proj_latent")))
    ctx_names = ["ckv", "krp", "xbc", "dt", "swk", "swv"] + (["qa", "z", "swq", "hy", "gate"] if ctx_out else [])
    ctx_dt = [F32 if k == "dt" else bf for k in ctx_names]
    ctx = dict(zip(ctx_names, _norm_matmul(xc, norm_g0, [w[k] for k in ctx_names], ctx_dt,
                                           shift=mod_c[0], scale=mod_c[1], name="in_proj_context")))

    knp_l, v_l = _norm_matmul(lat["ckv"], p["mla_kv_norm"], [w["uk"], w["uv"]], [bf, bf], name="mla_kv_up")
    knp_c, v_c = _norm_matmul(ctx["ckv"], p["mla_kv_norm"], [w["uk"], w["uv"]], [bf, bf], name="mla_kv_up_ctx")
    qp, qr = _norm_matmul(lat["qa"], p["mla_q_norm"], [w["uq"], w["uq_rot"]], [bf, bf], name="mla_q_up")
    cq, sq, ck = _mla_tables(t, rope_mla)
    mla_q_scale = MLA_SCALE * LOG2E
    swa_q_scale = SWA_SCALE * LOG2E
    mla_l = _mla_attention(qp, qr, cq * mla_q_scale, sq * mla_q_scale, knp_l, lat["krp"], lat["krr"], ck, sq, v_l,
                           ctx=(knp_c, ctx["krp"], v_c))

    xbc = _dwconv([ctx["xbc"], lat["xbc"]], p["ssd_conv_w"], p["ssd_conv_b"], act=True, name="ssd_conv")
    dt_raw = jnp.concatenate([ctx["dt"], lat["dt"]], axis=1)
    yf, yb = _ssd_scan(xbc, dt_raw, jnp.swapaxes(dt_raw, 1, 2), p["ssd_dt_bias"], p["ssd_a_log"], s_c // SSD_CHUNK)
    ssd_l = _ssd_output(yf, yb, xbc, lat["z"], p["ssd_d"], p["ssd_norm"], s_c)

    csw, ssw = _swa_tables(t, rope_swa)
    swa_l = _swa_attention(p["swa_sink"], lat["swq"], lat["swq_rot"], csw * swa_q_scale, ssw * swa_q_scale,
                           ctx["swk"], ctx["swv"], lat=(lat["swk"], lat["swk_rot"], csw, ssw, lat["swv"]))

    hy_args = (p["hy_w1"], p["hy_b1"], p["hy_w2"], p["hy_b2"], p["hy_w3"], p["hy_freq"])
    spec_l = _hyena_filters(t, *hy_args, dft_l[0], dft_l[1])
    hy_l = _hyena(lat["hy"], p["hy_conv_w"], p["hy_conv_b"], spec_l, p["hy_d"], dft_l)

    w_branch = p["w_branch"].astype(bf)
    w_out = p["w_out"].astype(bf)
    x_new = _merge([mla_l, ssd_l, swa_l, hy_l], lat["gate"], w_branch, w_out, x, mod_l[2], p["norm_g1"])
    if not ctx_out:
        return x_new, None

    qp_c, = _norm_matmul(ctx["qa"], p["mla_q_norm"], [w["uq"]], [bf], name="mla_q_up_ctx")
    cq_c, sq_c, ck_c = _mla_tables(s_c, None)
    mla_c = _mla_attention(qp_c, qp_c, cq_c * mla_q_scale, sq_c, knp_c, ctx["krp"], ctx["krp"], ck_c, sq_c, v_c)
    ssd_c = _ssd_output(yf, yb, xbc, ctx["z"], p["ssd_d"], p["ssd_norm"], 0)
    c1, s0 = _swa_tables(s_c, None)
    swa_c = _swa_attention(p["swa_sink"], ctx["swq"], ctx["swq"], c1 * swa_q_scale, s0, ctx["swk"], ctx["swv"])
    spec_c = _hyena_filters(s_c, *hy_args, dft_c[0], dft_c[1])
    hy_c = _hyena(ctx["hy"], p["hy_conv_w"], p["hy_conv_b"], spec_c, p["hy_d"], dft_c)
    xc_new = _merge([mla_c, ssd_c, swa_c, hy_c], ctx["gate"], w_branch, w_out, xc, mod_c[2], p["norm_g1"])
    return x_new, xc_new


def kernel(x, c, ctx, c_ctx, ada_w, ada_b, norm_g, w_in, mla_q_norm, mla_w_uq, mla_kv_norm, mla_w_ukv, ssd_conv_w, ssd_conv_b, ssd_dt_bias, ssd_a_log, ssd_d, ssd_norm, swa_sink, hy_conv_w, hy_conv_b, hy_w1, hy_b1, hy_w2, hy_b2, hy_w3, hy_freq, hy_d, w_branch, w_out, ffn_w_in, ffn_w_out):
    bsz, t, dm = x.shape
    s_c = ctx.shape[1]
    depth = ada_w.shape[0]
    rows = t // GRID_W
    rope_mla = _rope_tables(rows, MLA_ROPE)
    rope_swa = _rope_tables(rows, SWA_HEAD_DIM)
    dft_l = _dft_tables(t)
    dft_c = _dft_tables(s_c)
    cond_rows = 16
    cond = jnp.concatenate([c, c_ctx[None, :], jnp.zeros((cond_rows - bsz - 1, dm), F32)], axis=0)
    ffn_hidden = ffn_w_out.shape[1]
    xc = ctx
    for l in range(depth):
        ctx_out = l < depth - 1
        mod = _ada(cond, ada_w[l], ada_b[l])
        mod_l = [mod[:bsz, k * dm:(k + 1) * dm].reshape(bsz, 1, dm) for k in range(6)]
        mod_c = [jnp.broadcast_to(mod[bsz, k * dm:(k + 1) * dm].reshape(1, 1, dm), (bsz, 1, dm)) for k in range(6)]
        w = _layer_weights(w_in[l], mla_w_uq[l], mla_w_ukv[l])
        p = {"mla_q_norm": mla_q_norm[l], "mla_kv_norm": mla_kv_norm[l], "ssd_conv_w": ssd_conv_w[l],
             "ssd_conv_b": ssd_conv_b[l], "ssd_dt_bias": ssd_dt_bias[l], "ssd_a_log": ssd_a_log[l],
             "ssd_d": ssd_d[l], "ssd_norm": ssd_norm[l], "swa_sink": swa_sink[l], "hy_conv_w": hy_conv_w[l],
             "hy_conv_b": hy_conv_b[l], "hy_w1": hy_w1[l], "hy_b1": hy_b1[l], "hy_w2": hy_w2[l],
             "hy_b2": hy_b2[l], "hy_w3": hy_w3[l], "hy_freq": hy_freq[l], "hy_d": hy_d[l],
             "w_branch": w_branch[l], "w_out": w_out[l], "norm_g1": norm_g[l, 1]}
        x, xc_new = _token_mixers(x, xc, mod_l, mod_c, ctx_out, rope_mla, rope_swa, norm_g[l, 0], w, p, dft_l, dft_c)
        wg = ffn_w_in[l][:, :ffn_hidden].astype(MXU_DTYPE)
        wu = ffn_w_in[l][:, ffn_hidden:].astype(MXU_DTYPE)
        wo = ffn_w_out[l].astype(MXU_DTYPE)
        x = _ffn(x, norm_g[l, 2], mod_l[3], mod_l[4], wg, wu, wo, norm_g[l, 3], mod_l[5])
        if ctx_out:
            xc = _ffn(xc_new, norm_g[l, 2], mod_c[3], mod_c[4], wg, wu, wo, norm_g[l, 3], mod_c[5])
    return x
```

```python
import functools
import math

import jax
import jax.numpy as jnp
from jax import lax
from jax.experimental import pallas as pl
from jax.experimental.pallas import tpu as pltpu

F32 = jnp.float32
MXU_DTYPE = jnp.bfloat16
HIGHEST = lax.Precision.HIGHEST

GRID_W = 64
EPS = 1e-6
ROPE_BASE = 10000.0
MLA_HEADS, MLA_NOPE, MLA_ROPE, MLA_V = 8, 64, 32, 64
MLA_SCALE = (MLA_NOPE + MLA_ROPE) ** -0.5
SSD_HEADS, SSD_HEADDIM, SSD_GROUPS, SSD_STATE, SSD_CHUNK = 8, 64, 2, 64, 128
SSD_INNER = SSD_HEADS * SSD_HEADDIM
SWA_HEADS, SWA_KV_HEADS, SWA_HEAD_DIM, SWA_WINDOW, SWA_BLOCK = 8, 2, 64, 128, 128
SWA_SCALE = SWA_HEAD_DIM ** -0.5
HY_WIDTH, HY_ORDER, HY_BANDS, HY_HIDDEN = 512, 2, 16, 64
HY_DECAY_TARGET, HY_FAST_DECAY, HY_SLOW_DECAY = 1e-2, 0.3, 1.5
N_BRANCH = 4
LANES = 128
SSD_BATCH_ROWS = 4
VT_ROWS = 80
LOG2E = math.log2(math.e)
MLA_KEY_BLOCK = 256

VMEM_LIMIT = 56 * 1024 * 1024
_WHOLE_VMEM = pl.BlockSpec(memory_space=pltpu.VMEM)


def _params(n_grid, vmem=VMEM_LIMIT):
    return pltpu.CompilerParams(dimension_semantics=("arbitrary",) * n_grid, vmem_limit_bytes=vmem)


def _silu(x):
    return x * jax.nn.sigmoid(x)


def _softplus(x):
    return jnp.maximum(x, 0.0) + jnp.log1p(jnp.exp(-jnp.abs(x)))


def _rms(x, g):
    return x * lax.rsqrt(jnp.mean(x * x, axis=-1, keepdims=True) + EPS) * g


def _ada_kernel(s_ref, w_ref, b_ref, o_ref):
    s = _silu(s_ref[...])
    o_ref[...] = jnp.dot(s, w_ref[...], preferred_element_type=F32, precision=HIGHEST) + b_ref[...]


def _ada(cond, w, b, tn=1536):
    m, k = cond.shape
    n = w.shape[1]
    return pl.pallas_call(
        _ada_kernel,
        grid=(n // tn,),
        in_specs=[pl.BlockSpec((m, k), lambda j: (0, 0)),
                  pl.BlockSpec((k, tn), lambda j: (0, j)),
                  pl.BlockSpec((1, tn), lambda j: (0, j))],
        out_specs=pl.BlockSpec((m, tn), lambda j: (0, j)),
        out_shape=jax.ShapeDtypeStruct((m, n), F32),
        compiler_params=_params(1),
        name="ada_ln",
    )(cond, w, b.reshape(1, n))


def _norm_mm_kernel(*refs, n_w, modulate):
    it = iter(refs)
    x_ref, g_ref = next(it), next(it)
    if modulate:
        sh_ref, sc_ref = next(it), next(it)
    w_refs = [next(it) for _ in range(n_w)]
    o_refs = [next(it) for _ in range(n_w)]
    a = _rms(x_ref[...].astype(F32), g_ref[...])
    if modulate:
        a = a * (1.0 + sc_ref[...]) + sh_ref[...]
    a = a.astype(MXU_DTYPE)
    for w_ref, o_ref in zip(w_refs, o_refs):
        n = w_ref.shape[1]
        for c0 in range(0, n, 1024):
            c1 = min(n, c0 + 1024)
            o_ref[:, c0:c1] = jnp.dot(a, w_ref[:, c0:c1], preferred_element_type=F32).astype(o_ref.dtype)


def _norm_matmul(x, g, ws, out_dtypes, shift=None, scale=None, tm=512, name="norm_matmul"):
    bsz, t, k = x.shape
    tm = min(tm, t)
    modulate = shift is not None
    in_specs = [pl.BlockSpec((None, tm, k), lambda b, i: (b, i, 0)),
                pl.BlockSpec((1, k), lambda b, i: (0, 0))]
    args = [x, g.reshape(1, k).astype(F32)]
    if modulate:
        in_specs += [pl.BlockSpec((None, 1, k), lambda b, i: (b, 0, 0))] * 2
        args += [shift, scale]
    in_specs += [_WHOLE_VMEM] * len(ws)
    args += list(ws)
    return pl.pallas_call(
        functools.partial(_norm_mm_kernel, n_w=len(ws), modulate=modulate),
        grid=(bsz, t // tm),
        in_specs=in_specs,
        out_specs=[pl.BlockSpec((None, tm, w.shape[1]), lambda b, i: (b, i, 0)) for w in ws],
        out_shape=[jax.ShapeDtypeStruct((bsz, t, w.shape[1]), dt) for w, dt in zip(ws, out_dtypes)],
        compiler_params=_params(2),
        name=name,
    )(*args)


def _mla_kernel(*refs, t_l, s_c):
    if s_c:
        (qp, qr, cq, sq, knp_l, krp_l, krr_l, ck, sk, v_l, knp_c, krp_c, v_c, o_ref, kcat, vt) = refs
    else:
        (qp, qr, cq, sq, knp_l, krp_l, krr_l, ck, sk, v_l, o_ref, kcat, vt) = refs

    @pl.when(pl.program_id(1) == 0)
    def _():
        kro = (krp_l[...].astype(F32) * ck[...] + krr_l[...].astype(F32) * sk[...]).astype(kcat.dtype)
        for h in range(MLA_HEADS):
            blk = slice(h * LANES, (h + 1) * LANES)
            kcat[h, 0:t_l, :] = knp_l[:, blk] + kro
            if s_c:
                kcat[h, t_l:t_l + s_c, :] = knp_c[:, blk] + krp_c[...]
            vt[h, MLA_V:, :] = jnp.ones((VT_ROWS - MLA_V, t_l + s_c), vt.dtype)
        for j in range(MLA_HEADS // 2):
            blk = slice(j * LANES, (j + 1) * LANES)
            vp = v_l[:, blk].astype(F32).T
            vt[2 * j, 0:MLA_V, 0:t_l] = vp[0:MLA_V].astype(vt.dtype)
            vt[2 * j + 1, 0:MLA_V, 0:t_l] = vp[MLA_V:].astype(vt.dtype)
            if s_c:
                vp = v_c[:, blk].astype(F32).T
                vt[2 * j, 0:MLA_V, t_l:t_l + s_c] = vp[0:MLA_V].astype(vt.dtype)
                vt[2 * j + 1, 0:MLA_V, t_l:t_l + s_c] = vp[MLA_V:].astype(vt.dtype)

    cqv, sqv = cq[...], sq[...]
    tq = qp.shape[0]
    s_tot = t_l + s_c
    nt = (((1,), (1,)), ((), ()))
    heads = range(MLA_HEADS)
    q, m, acc = {}, {}, {}
    for h in heads:
        blk = slice(h * LANES, (h + 1) * LANES)
        q[h] = (qp[:, blk].astype(F32) * cqv + qr[:, blk].astype(F32) * sqv).astype(MXU_DTYPE)
        m[h] = jnp.full((1, tq), -1e30, F32)
        acc[h] = jnp.zeros((VT_ROWS, tq), F32)
    for k0 in range(0, s_tot, MLA_KEY_BLOCK):
        k1 = min(s_tot, k0 + MLA_KEY_BLOCK)
        s = {h: lax.dot_general(kcat[h, k0:k1, :], q[h], nt, preferred_element_type=F32) for h in heads}
        m_new = {h: jnp.maximum(m[h], jnp.max(s[h], axis=0, keepdims=True)) for h in heads}
        p = {h: jnp.exp2(s[h] - m_new[h]).astype(MXU_DTYPE) for h in heads}
        pv = {h: jnp.dot(vt[h, :, k0:k1], p[h], preferred_element_type=F32) for h in heads}
        for h in heads:
            acc[h] = acc[h] * jnp.exp2(m[h] - m_new[h]) + pv[h]
            m[h] = m_new[h]
    for j in range(MLA_HEADS // 2):
        halves = [acc[h][0:MLA_V] * (1.0 / acc[h][MLA_V:MLA_V + 1]) for h in (2 * j, 2 * j + 1)]
        o_ref[:, j * LANES:(j + 1) * LANES] = jnp.concatenate(halves, axis=0).T.astype(o_ref.dtype)


def _mla_attention(qp, qr, cq, sq, knp_l, krp_l, krr_l, ck, sk, v_l, ctx=None, tq=512):
    bsz, t_q, _ = qp.shape
    t_l = knp_l.shape[1]
    s_c = 0 if ctx is None else ctx[0].shape[1]
    tq = min(tq, t_q)
    hw = MLA_HEADS * LANES
    vw = MLA_HEADS * MLA_V
    per_b = lambda rows, cols: pl.BlockSpec((None, rows, cols), lambda b, i: (b, 0, 0))
    in_specs = [pl.BlockSpec((None, tq, hw), lambda b, i: (b, i, 0)),
                pl.BlockSpec((None, tq, hw), lambda b, i: (b, i, 0)),
                pl.BlockSpec((tq, LANES), lambda b, i: (i, 0)),
                pl.BlockSpec((tq, LANES), lambda b, i: (i, 0)),
                per_b(t_l, hw), per_b(t_l, LANES), per_b(t_l, LANES),
                pl.BlockSpec((t_l, LANES), lambda b, i: (0, 0)),
                pl.BlockSpec((t_l, LANES), lambda b, i: (0, 0)),
                per_b(t_l, vw)]
    args = [qp, qr, cq, sq, knp_l, krp_l, krr_l, ck, sk, v_l]
    if s_c:
        in_specs += [per_b(s_c, hw), per_b(s_c, LANES), per_b(s_c, vw)]
        args += list(ctx)
    return pl.pallas_call(
        functools.partial(_mla_kernel, t_l=t_l, s_c=s_c),
        grid=(bsz, t_q // tq),
        in_specs=in_specs,
        out_specs=pl.BlockSpec((None, tq, vw), lambda b, i: (b, i, 0)),
        out_shape=jax.ShapeDtypeStruct((bsz, t_q, vw), MXU_DTYPE),
        scratch_shapes=[pltpu.VMEM((MLA_HEADS, t_l + s_c, LANES), MXU_DTYPE),
                        pltpu.VMEM((MLA_HEADS, VT_ROWS, t_l + s_c), MXU_DTYPE)],
        compiler_params=_params(2),
        name="mla_attention",
    )(*args)


def _swa_kernel(*refs, band, t_k):
    if band:
        (sink, q, qrot, cq, sq, k, krot, ck, sk, v, kc, vc, o_ref, kro, vt, vct) = refs
    else:
        (sink, q, qrot, cq, sq, kc, vc, o_ref, vct) = refs
    i = pl.program_id(1)
    tq = q.shape[0]
    gw = 2 * SWA_HEAD_DIM
    dh = SWA_HEAD_DIM
    kb_rows = tq + 2 * SWA_BLOCK

    @pl.when(i == 0)
    def _():
        def put_vt(dst, src):
            vtr = src[...].astype(F32).T
            for g in range(SWA_KV_HEADS):
                dst[g, 0:dh, :] = vtr[g * dh:(g + 1) * dh].astype(dst.dtype)
                dst[g, dh:, :] = jnp.ones((VT_ROWS - dh, src.shape[0]), dst.dtype)

        put_vt(vct, vc)
        if band:
            put_vt(vt, v)
            ckv, skv = ck[...], sk[...]
            for g in range(SWA_KV_HEADS):
                blk = slice(g * gw, (g + 1) * gw)
                kro[:, blk] = (k[:, blk].astype(F32) * ckv + krot[:, blk].astype(F32) * skv).astype(kro.dtype)

    if band:
        start = pl.multiple_of(jnp.clip(i * tq - SWA_BLOCK, 0, t_k - kb_rows), SWA_BLOCK)
        k_pos = start + lax.broadcasted_iota(jnp.int32, (kb_rows, tq), 0)
        q_pos = i * tq + lax.broadcasted_iota(jnp.int32, (kb_rows, tq), 1)
        in_band = jnp.abs(q_pos - k_pos) <= SWA_WINDOW

    lane_lo = lax.broadcasted_iota(jnp.int32, (tq, LANES), 1) < dh
    cqv, sqv = cq[...], sq[...]
    nt = (((1,), (1,)), ((), ()))
    heads = range(SWA_HEADS)
    group = {h: h // (SWA_HEADS // SWA_KV_HEADS) for h in heads}
    gblk = {h: slice(group[h] * gw, (group[h] + 1) * gw) for h in heads}
    qm, snk = {}, {}
    for j in range(SWA_HEADS // 2):
        blk = slice(j * LANES, (j + 1) * LANES)
        qro = q[:, blk].astype(F32) * cqv + qrot[:, blk].astype(F32) * sqv
        qm[2 * j] = jnp.where(lane_lo, qro, 0.0).astype(MXU_DTYPE)
        qm[2 * j + 1] = jnp.where(lane_lo, 0.0, qro).astype(MXU_DTYPE)
    for h in heads:
        snk[h] = sink[h] * LOG2E
    s_c = {h: lax.dot_general(kc[:, gblk[h]], qm[h], nt, preferred_element_type=F32) for h in heads}
    m = {h: jnp.maximum(jnp.max(s_c[h], axis=0, keepdims=True), snk[h]) for h in heads}
    if band:
        s_b = {h: jnp.where(in_band, lax.dot_general(kro[pl.ds(start, kb_rows), gblk[h]], qm[h], nt,
                                                     preferred_element_type=F32), -1e30) for h in heads}
        m = {h: jnp.maximum(m[h], jnp.max(s_b[h], axis=0, keepdims=True)) for h in heads}
    o = {h: jnp.dot(vct[group[h]], jnp.exp2(s_c[h] - m[h]).astype(MXU_DTYPE), preferred_element_type=F32)
         for h in heads}
    if band:
        o = {h: o[h] + jnp.dot(vt[group[h], :, pl.ds(start, kb_rows)], jnp.exp2(s_b[h] - m[h]).astype(MXU_DTYPE),
                               preferred_element_type=F32) for h in heads}
    for j in range(SWA_HEADS // 2):
        halves = []
        for h in (2 * j, 2 * j + 1):
            l = o[h][dh:dh + 1] + jnp.exp2(snk[h] - m[h])
            halves.append(o[h][0:dh] * (1.0 / l))
        o_ref[:, j * LANES:(j + 1) * LANES] = jnp.concatenate(halves, axis=0).T.astype(o_ref.dtype)


def _swa_attention(sink, q, qrot, cq, sq, kc, vc, lat=None):
    bsz, t, hw = q.shape
    s_c = kc.shape[1]
    tq = min(t, 2 * SWA_BLOCK)
    kw, vw = kc.shape[2], vc.shape[2]
    band = lat is not None
    per_b = lambda rows, cols: pl.BlockSpec((None, rows, cols), lambda b, i: (b, 0, 0))
    qspec = pl.BlockSpec((None, tq, hw), lambda b, i: (b, i, 0))
    tspec = pl.BlockSpec((tq, LANES), lambda b, i: (i, 0))
    in_specs = [pl.BlockSpec(memory_space=pltpu.SMEM), qspec, qspec, tspec, tspec]
    args = [sink.astype(F32), q, qrot, cq, sq]
    scratch = []
    if band:
        k, krot, ck, sk, v = lat
        full_t = pl.BlockSpec((t, LANES), lambda b, i: (0, 0))
        in_specs += [per_b(t, kw), per_b(t, kw), full_t, full_t, per_b(t, vw)]
        args += [k, krot, ck, sk, v]
        scratch = [pltpu.VMEM((t, kw), MXU_DTYPE), pltpu.VMEM((SWA_KV_HEADS, VT_ROWS, t), MXU_DTYPE)]
    scratch += [pltpu.VMEM((SWA_KV_HEADS, VT_ROWS, s_c), MXU_DTYPE)]
    in_specs += [per_b(s_c, kw), per_b(s_c, vw)]
    args += [kc, vc]
    return pl.pallas_call(
        functools.partial(_swa_kernel, band=band, t_k=t),
        grid=(bsz, t // tq),
        in_specs=in_specs,
        out_specs=pl.BlockSpec((None, tq, hw), lambda b, i: (b, i, 0)),
        out_shape=jax.ShapeDtypeStruct((bsz, t, hw), MXU_DTYPE),
        scratch_shapes=scratch,
        compiler_params=_params(2),
        name="swa_attention" if band else "ctx_sink_attention",
    )(*args)


def _dwconv_kernel(*refs, n_seg, act):
    x_refs = refs[:n_seg]
    w_ref, b_ref, o_ref = refs[n_seg:]
    taps = w_ref.shape[0]
    off = 0
    for x_ref in x_refs:
        t = x_ref.shape[0]
        x = x_ref[...].astype(F32)
        row = lax.broadcasted_iota(jnp.int32, x.shape, 0)
        acc = jnp.zeros_like(x) + b_ref[...]
        for kk in range(taps):
            d = kk - taps // 2
            if d == 0:
                xs = x
            else:
                xs = pltpu.roll(x, (-d) % t, axis=0)
                xs = jnp.where((row + d >= 0) & (row + d < t), xs, 0.0)
            acc = acc + xs * w_ref[kk:kk + 1, :]
        if act:
            acc = _silu(acc)
        o_ref[off:off + t, :] = acc.astype(o_ref.dtype)
        off += t


def _dwconv(xs, w, b, act, tc=256, name="dwconv"):
    bsz, _, c = xs[0].shape
    t_tot = sum(x.shape[1] for x in xs)
    taps = w.shape[0]
    in_specs = [pl.BlockSpec((None, x.shape[1], tc), lambda bb, j: (bb, 0, j)) for x in xs]
    in_specs += [pl.BlockSpec((taps, tc), lambda bb, j: (0, j)), pl.BlockSpec((1, tc), lambda bb, j: (0, j))]
    return pl.pallas_call(
        functools.partial(_dwconv_kernel, n_seg=len(xs), act=act),
        grid=(bsz, c // tc),
        in_specs=in_specs,
        out_specs=pl.BlockSpec((None, t_tot, tc), lambda bb, j: (bb, 0, j)),
        out_shape=jax.ShapeDtypeStruct((bsz, t_tot, c), MXU_DTYPE),
        compiler_params=_params(2),
        name=name,
    )(*xs, w.astype(F32), b.reshape(1, c).astype(F32))


def _ssd_scan_kernel(xs_f, bm_f, cm_f, dt_f, dtt_f, xs_b, bm_b, cm_b, dt_b, dtt_b,
                     b_row, al_row, b_col, al_col, yf_ref, yb_ref, st_ref):
    @pl.when(pl.program_id(1) == 0)
    def _():
        st_ref[...] = jnp.zeros_like(st_ref)

    q = SSD_CHUNK
    row = lax.broadcasted_iota(jnp.int32, (q, q), 0)
    col = lax.broadcasted_iota(jnp.int32, (q, q), 1)
    lane_lo = col < SSD_HEADDIM
    a_row = -jnp.exp(al_row[...])
    a_col = -jnp.exp(al_col[...])
    nt = (((1,), (1,)), ((), ()))
    tn = (((0,), (0,)), ((), ()))
    dirs = ((xs_f, bm_f, cm_f, dt_f, dtt_f, yf_ref), (xs_b, bm_b, cm_b, dt_b, dtt_b, yb_ref))
    units = [(bi, d) for bi in range(st_ref.shape[0]) for d in range(2)]
    pairs_per_group = SSD_HEADS // SSD_GROUPS // 2
    feeds = {0: row >= col, 1: row <= col}
    last = {0: q - 1, 1: 0}
    dt, dtt, cs, cst, cg, bg, gmat, sc, ecol, wcol, dec = ({} for _ in range(11))
    for u in units:
        bi, d = u
        dt_ref, dtt_ref = dirs[d][3], dirs[d][4]
        dt[u] = _softplus(dt_ref[bi] + b_row[...])
        dtt[u] = _softplus(dtt_ref[bi] + b_col[...])
    for u in units:
        d = u[1]
        cs[u] = jnp.dot(feeds[d].astype(F32), dt[u] * a_row, preferred_element_type=F32, precision=HIGHEST)
        cst[u] = jnp.dot(dtt[u] * a_col, feeds[1 - d].astype(F32), preferred_element_type=F32, precision=HIGHEST)
    for u in units:
        bi, d = u
        bblk, cblk = dirs[d][1][bi].astype(F32), dirs[d][2][bi].astype(F32)
        for g in range(SSD_GROUPS):
            gmask = (col // SSD_STATE) == g
            cg[u, g] = jnp.where(gmask, cblk, 0.0).astype(MXU_DTYPE)
            bg[u, g] = jnp.where(gmask, bblk, 0.0)
            gmat[u, g] = lax.dot_general(cg[u, g], bg[u, g].astype(MXU_DTYPE), nt,
                                         preferred_element_type=F32)
    for u in units:
        d = u[1]
        tot = cs[u][last[d]:last[d] + 1, :]
        e_all = jnp.exp(cs[u])
        w_all = dt[u] * jnp.exp(tot - cs[u])
        d_all = jnp.exp(tot)
        for h in range(SSD_HEADS):
            c = d * SSD_HEADS + h
            ccol = cs[u][:, c:c + 1]
            crow = cst[u][c:c + 1, :]
            lmat = jnp.where(feeds[d], jnp.exp(ccol - crow), 0.0)
            sc[u, h] = (gmat[u, h // (SSD_HEADS // SSD_GROUPS)] * lmat * dtt[u][c:c + 1, :]).astype(MXU_DTYPE)
            ecol[u, h] = e_all[:, c:c + 1]
            wcol[u, h] = w_all[:, c:c + 1]
            dec[u, h] = d_all[:, c:c + 1]
    for u in units:
        bi, d = u
        xs_ref, y_ref = dirs[d][0], dirs[d][5]
        for j in range(SSD_HEADS // 2):
            g = j // pairs_per_group
            xb = xs_ref[bi, :, j * LANES:(j + 1) * LANES]
            xblk = xb.astype(F32)
            s_in = st_ref[bi, d, j]
            y_inter = jnp.dot(cg[u, g], s_in.astype(MXU_DTYPE), preferred_element_type=F32)
            ys, news = [], []
            for hh in range(2):
                h = 2 * j + hh
                ys.append(jnp.dot(sc[u, h], xb, preferred_element_type=F32) + ecol[u, h] * y_inter)
                bw = (bg[u, g] * wcol[u, h]).astype(MXU_DTYPE)
                xh = jnp.where(lane_lo if hh == 0 else ~lane_lo, xblk, 0.0).astype(MXU_DTYPE)
                news.append(lax.dot_general(bw, xh, tn, preferred_element_type=F32))
            y_ref[bi, :, j * LANES:(j + 1) * LANES] = jnp.where(lane_lo, ys[0], ys[1]).astype(y_ref.dtype)
            st_ref[bi, d, j] = (s_in * jnp.where(lane_lo, dec[u, 2 * j], dec[u, 2 * j + 1])
                                + news[0] + news[1])


def _ssd_scan(xbc, dt_raw, dt_raw_t, dt_bias, a_log, nc_ctx):
    bsz, t_c, _ = xbc.shape
    q = SSD_CHUNK
    nc = t_c // q
    nh2 = 2 * SSD_HEADS

    def fwd(s):
        return s

    def bwd(s):
        return jnp.where(s < nc_ctx, nc_ctx - 1 - s, nc + nc_ctx - 1 - s)

    bt = math.gcd(bsz, SSD_BATCH_ROWS)

    def specs(order):
        return [pl.BlockSpec((bt, q, SSD_INNER), lambda b, s: (b, order(s), 0)),
                pl.BlockSpec((bt, q, LANES), lambda b, s: (b, order(s), SSD_INNER // LANES)),
                pl.BlockSpec((bt, q, LANES), lambda b, s: (b, order(s), SSD_INNER // LANES + 1)),
                pl.BlockSpec((bt, q, nh2), lambda b, s: (b, order(s), 0)),
                pl.BlockSpec((bt, nh2, q), lambda b, s: (b, 0, order(s)))]

    small = lambda r, c: pl.BlockSpec((r, c), lambda b, s: (0, 0))
    bias = dt_bias.astype(F32).reshape(1, nh2)
    alog = a_log.astype(F32).reshape(1, nh2)
    return pl.pallas_call(
        _ssd_scan_kernel,
        grid=(bsz // bt, nc),
        in_specs=specs(fwd) + specs(bwd) + [small(1, nh2), small(1, nh2), small(nh2, 1), small(nh2, 1)],
        out_specs=[pl.BlockSpec((bt, q, SSD_INNER), lambda b, s: (b, fwd(s), 0)),
                   pl.BlockSpec((bt, q, SSD_INNER), lambda b, s: (b, bwd(s), 0))],
        out_shape=[jax.ShapeDtypeStruct((bsz, t_c, SSD_INNER), MXU_DTYPE)] * 2,
        scratch_shapes=[pltpu.VMEM((bt, 2, SSD_HEADS // 2, LANES, LANES), F32)],
        compiler_params=_params(2),
        name="ssd_scan",
    )(xbc, xbc, xbc, dt_raw, dt_raw_t, xbc, xbc, xbc, dt_raw, dt_raw_t,
      bias, alog, bias.reshape(nh2, 1), alog.reshape(nh2, 1))


def _ssd_out_kernel(yf, yb, xs, z, dexp, ng, o_ref):
    y = yf[...].astype(F32) + yb[...].astype(F32) + dexp[...] * xs[...].astype(F32)
    yg = y * _silu(z[...].astype(F32))
    gw = SSD_INNER // SSD_GROUPS
    for g in range(SSD_GROUPS):
        blk = slice(g * gw, (g + 1) * gw)
        o_ref[:, blk] = _rms(yg[:, blk], ng[:, blk]).astype(o_ref.dtype)


def _ssd_output(yf, yb, xbc, z, d_skip, norm_g, row0, tm=256):
    bsz, t, _ = z.shape
    tm = min(tm, t)
    off = row0 // tm
    sp_y = pl.BlockSpec((None, tm, SSD_INNER), lambda b, i: (b, i + off, 0))
    vec = pl.BlockSpec((1, SSD_INNER), lambda b, i: (0, 0))
    dexp = jnp.repeat(d_skip.astype(F32), SSD_HEADDIM).reshape(1, SSD_INNER)
    return pl.pallas_call(
        _ssd_out_kernel,
        grid=(bsz, t // tm),
        in_specs=[sp_y, sp_y, sp_y, pl.BlockSpec((None, tm, SSD_INNER), lambda b, i: (b, i, 0)), vec, vec],
        out_specs=pl.BlockSpec((None, tm, SSD_INNER), lambda b, i: (b, i, 0)),
        out_shape=jax.ShapeDtypeStruct((bsz, t, SSD_INNER), MXU_DTYPE),
        compiler_params=_params(2),
        name="ssd_output",
    )(yf, yb, xbc, z, dexp, norm_g.astype(F32).reshape(1, SSD_INNER))


def _hy_mlp_kernel(feat_ref, w1, b1, w2, b2, w3, fr, delta, h_ref, cs_ref):
    feat = feat_ref[...]
    h = jnp.sin(fr[...] * (jnp.dot(feat, w1[...], preferred_element_type=F32, precision=HIGHEST) + b1[...]))
    h = jnp.sin(fr[...] * (jnp.dot(h, w2[...], preferred_element_type=F32, precision=HIGHEST) + b2[...]))
    h = jnp.dot(h, w3[...], preferred_element_type=F32, precision=HIGHEST)
    h = h * jnp.exp(-feat[:, 0:1] * delta[...])
    h_ref[...] = h
    s = jnp.sum(jnp.abs(h), axis=0, keepdims=True)

    @pl.when(pl.program_id(0) == 0)
    def _():
        cs_ref[...] = s

    @pl.when(pl.program_id(0) != 0)
    def _():
        cs_ref[...] += s


def _hy_spectrum_kernel(h0_ref, h1_ref, c0_ref, c1_ref, cm_ref, sf_ref, ar_ref, ai_ref, br_ref):
    n = h0_ref.shape[0]
    inv = 1.0 / (c0_ref[...] + c1_ref[...])
    row = lax.broadcasted_iota(jnp.int32, h0_ref.shape, 0)
    first = row == 0
    h0 = h0_ref[...] * inv
    h1 = jnp.where(first, 0.0, h1_ref[...] * inv)
    a = h0 + h1
    kr = jnp.dot(cm_ref[...], a.astype(MXU_DTYPE), preferred_element_type=F32)
    kq = jnp.dot(sf_ref[...], (h1 - h0).astype(MXU_DTYPE), preferred_element_type=F32)
    k_nyq = jnp.sum(jnp.where(row % 2 == 0, a, -a), axis=0, keepdims=True)
    inv_n = 1.0 / (2 * n)
    ar = kr * jnp.where(first, inv_n, 2.0 * inv_n)
    ar_ref[...] = ar
    ai_ref[...] = jnp.where(first, 0.0, kq * (2.0 * inv_n))
    br_ref[...] = jnp.where(first, k_nyq * inv_n, ar)


def _hyena_filters(n, w1, b1, w2, b2, w3, freq, cm, sf):
    t = jnp.arange(n, dtype=F32)
    tnorm = t / n
    bands = jnp.linspace(1e-4, HY_BANDS - 1, HY_BANDS, dtype=F32)
    ang = 2 * math.pi * t[:, None] * bands[None, :] / n
    feat = jnp.concatenate([tnorm[:, None], jnp.cos(ang), -jnp.sin(ang)], axis=-1)
    emb = feat.shape[1]
    feat = jnp.pad(feat, ((0, 0), (0, HY_HIDDEN - emb)))
    w1p = jnp.pad(w1.astype(F32), ((0, HY_HIDDEN - emb), (0, 0)))
    deltas = jnp.abs(jnp.linspace(math.log(HY_DECAY_TARGET) / HY_SLOW_DECAY,
                                  math.log(HY_DECAY_TARGET) / HY_FAST_DECAY, HY_WIDTH, dtype=F32))
    ncol = HY_ORDER * 2 * HY_WIDTH
    delta_row = jnp.tile(deltas, HY_ORDER * 2).reshape(1, ncol)
    tt = min(n, 256)
    small = lambda r, c: pl.BlockSpec((r, c), lambda i: (0, 0))
    row = lambda v: v.astype(F32).reshape(1, -1)
    h, colsum = pl.pallas_call(
        _hy_mlp_kernel,
        grid=(n // tt,),
        in_specs=[pl.BlockSpec((tt, HY_HIDDEN), lambda i: (i, 0)),
                  small(HY_HIDDEN, HY_HIDDEN), small(1, HY_HIDDEN),
                  small(HY_HIDDEN, HY_HIDDEN), small(1, HY_HIDDEN),
                  small(HY_HIDDEN, ncol), small(1, HY_HIDDEN), small(1, ncol)],
        out_specs=[pl.BlockSpec((tt, ncol), lambda i: (i, 0)), small(1, ncol)],
        out_shape=[jax.ShapeDtypeStruct((n, ncol), F32), jax.ShapeDtypeStruct((1, ncol), F32)],
        compiler_params=_params(1),
        name="hyena_filter_mlp",
    )(feat, w1p, row(b1), w2.astype(F32), row(b2), w3.astype(F32), row(freq), delta_row)

    tc = 256
    per_o = HY_WIDTH // tc
    side0 = lambda jc: (jc // per_o) * 2 * per_o + jc % per_o
    side1 = lambda jc: (jc // per_o) * 2 * per_o + per_o + jc % per_o
    nout = HY_ORDER * HY_WIDTH
    out_spec = pl.BlockSpec((n, tc), lambda jc: (0, jc))
    return pl.pallas_call(
        _hy_spectrum_kernel,
        grid=(nout // tc,),
        in_specs=[pl.BlockSpec((n, tc), lambda jc: (0, side0(jc))),
                  pl.BlockSpec((n, tc), lambda jc: (0, side1(jc))),
                  pl.BlockSpec((1, tc), lambda jc: (0, side0(jc))),
                  pl.BlockSpec((1, tc), lambda jc: (0, side1(jc))),
                  _WHOLE_VMEM, _WHOLE_VMEM],
        out_specs=[out_spec] * 3,
        out_shape=[jax.ShapeDtypeStruct((n, nout), F32)] * 3,
        compiler_params=_params(1),
        name="hyena_filter_spectrum",
    )(h, h, colsum, colsum, cm, sf)


def _hy_fwd_kernel(u_ref, cm_ref, sf_ref, ar_ref, ai_ref, br_ref, yr_ref, yi_ref):
    u = u_ref[...].astype(MXU_DTYPE)
    p = jnp.dot(cm_ref[...], u, preferred_element_type=F32)
    q = jnp.dot(sf_ref[...], u, preferred_element_type=F32)
    ai = ai_ref[...]
    yr_ref[...] = (p * ar_ref[...] + q * ai).astype(yr_ref.dtype)
    yi_ref[...] = (q * br_ref[...] - p * ai).astype(yi_ref.dtype)


def _hy_inv_kernel(yr_ref, yi_ref, cm_ref, si_ref, u_ref, xg_ref, d_ref, o_ref):
    y = (jnp.dot(cm_ref[...], yr_ref[...], preferred_element_type=F32)
         + jnp.dot(si_ref[...], yi_ref[...], preferred_element_type=F32))
    u = u_ref[...].astype(F32)
    o_ref[...] = (xg_ref[...].astype(F32) * (y + u * d_ref[...])).astype(o_ref.dtype)


def _hyena_conv(u, u_col0, xg, xg_col0, spectra, order, d, tables, out_dtype, tc=256):
    cm, sf, si = tables
    ar, ai, br = spectra
    bsz, n, _ = u.shape
    nct = HY_WIDTH // tc
    ucol, gcol, scol = u_col0 // tc, xg_col0 // tc, order * nct
    tok = lambda c0: pl.BlockSpec((None, n, tc), lambda c, b: (b, 0, c0 + c))
    spec_sp = pl.BlockSpec((n, tc), lambda c, b: (0, scol + c))
    mid = pl.BlockSpec((None, n, tc), lambda c, b: (b, 0, c))
    yr, yi = pl.pallas_call(
        _hy_fwd_kernel,
        grid=(nct, bsz),
        in_specs=[tok(ucol), _WHOLE_VMEM, _WHOLE_VMEM, spec_sp, spec_sp, spec_sp],
        out_specs=[mid, mid],
        out_shape=[jax.ShapeDtypeStruct((bsz, n, HY_WIDTH), MXU_DTYPE)] * 2,
        compiler_params=_params(2),
        name="hyena_dft_forward",
    )(u, cm, sf, ar, ai, br)
    return pl.pallas_call(
        _hy_inv_kernel,
        grid=(nct, bsz),
        in_specs=[mid, mid, _WHOLE_VMEM, _WHOLE_VMEM, tok(ucol), tok(gcol),
                  pl.BlockSpec((1, tc), lambda c, b: (0, c))],
        out_specs=mid,
        out_shape=jax.ShapeDtypeStruct((bsz, n, HY_WIDTH), out_dtype),
        compiler_params=_params(2),
        name="hyena_dft_inverse",
    )(yr, yi, cm, si, u, xg, d.astype(F32).reshape(1, HY_WIDTH))


def _dft_tables(n):
    idx = jnp.arange(n, dtype=jnp.int32)
    prod = (idx[:, None] * idx[None, :]) % (2 * n)
    ang = prod.astype(F32) * (math.pi / n)
    alt = jnp.where(idx % 2 == 0, 1.0, -1.0).astype(F32)
    cm = jnp.cos(ang)
    sm = jnp.sin(ang)
    sf = jnp.where(idx[:, None] == 0, alt[None, :], sm)
    si = jnp.where(idx[None, :] == 0, alt[:, None], sm)
    return cm.astype(MXU_DTYPE), sf.astype(MXU_DTYPE), si.astype(MXU_DTYPE)


def _hyena(hy, conv_w, conv_b, spectra, d, tables):
    u3 = _dwconv([hy], conv_w, conv_b, act=False, name="hyena_short_conv")
    z = _hyena_conv(u3, 0, u3, HY_WIDTH, spectra, 0, d[0], tables, MXU_DTYPE)
    return _hyena_conv(z, 0, u3, 2 * HY_WIDTH, spectra, 1, d[1], tables, MXU_DTYPE)


def _merge_kernel(b0, b1, b2, b3, gate_ref, wb_ref, wo_ref, x_ref, mg_ref, g_ref, o_ref):
    d = x_ref.shape[1]
    acc = None
    for i, br in enumerate((b0, b1, b2, b3)):
        proj = jnp.dot(br[...], wb_ref[i], preferred_element_type=F32)
        term = jax.nn.sigmoid(gate_ref[:, i * d:(i + 1) * d].astype(F32)) * proj
        acc = term if acc is None else acc + term
    y = jnp.dot(acc.astype(MXU_DTYPE), wo_ref[...], preferred_element_type=F32)
    o_ref[...] = x_ref[...] + mg_ref[...] * _rms(y, g_ref[...])


def _merge(branches, gate, w_branch, w_out, x, mod_gate, g, tm=512):
    bsz, t, d = x.shape
    tm = min(tm, t)
    bw = branches[0].shape[2]
    tok = lambda w: pl.BlockSpec((None, tm, w), lambda b, i: (b, i, 0))
    return pl.pallas_call(
        _merge_kernel,
        grid=(bsz, t // tm),
        in_specs=[tok(bw)] * N_BRANCH + [tok(N_BRANCH * d), _WHOLE_VMEM, _WHOLE_VMEM, tok(d),
                                         pl.BlockSpec((None, 1, d), lambda b, i: (b, 0, 0)),
                                         pl.BlockSpec((1, d), lambda b, i: (0, 0))],
        out_specs=tok(d),
        out_shape=jax.ShapeDtypeStruct((bsz, t, d), F32),
        compiler_params=_params(2),
        name="merge_branches",
    )(*branches, gate, w_branch, w_out, x, mod_gate, g.astype(F32).reshape(1, d))


def _ffn_kernel(x_ref, g_in, sh_ref, sc_ref, wg_ref, wu_ref, wo_ref, g_out, mg_ref, o_ref, *, chunk):
    x = x_ref[...]
    a = (_rms(x, g_in[...]) * (1.0 + sc_ref[...]) + sh_ref[...]).astype(MXU_DTYPE)
    hidden = wg_ref.shape[1]
    acc = None
    for c0 in range(0, hidden, chunk):
        gch = jnp.dot(a, wg_ref[:, c0:c0 + chunk], preferred_element_type=F32)
        uch = jnp.dot(a, wu_ref[:, c0:c0 + chunk], preferred_element_type=F32)
        act = (_silu(gch) * uch).astype(MXU_DTYPE)
        part = jnp.dot(act, wo_ref[c0:c0 + chunk, :], preferred_element_type=F32)
        acc = part if acc is None else acc + part
    o_ref[...] = x + mg_ref[...] * _rms(acc, g_out[...])


def _ffn(x, g_in, shift, scale, w_gate, w_up, w_out, g_out, mod_gate, tm=512):
    bsz, t, d = x.shape
    tm = min(tm, t)
    hidden = w_gate.shape[1]
    chunk = hidden // 2
    tok = pl.BlockSpec((None, tm, d), lambda b, i: (b, i, 0))
    per_b = pl.BlockSpec((None, 1, d), lambda b, i: (b, 0, 0))
    vec = pl.BlockSpec((1, d), lambda b, i: (0, 0))
    return pl.pallas_call(
        functools.partial(_ffn_kernel, chunk=chunk),
        grid=(bsz, t // tm),
        in_specs=[tok, vec, per_b, per_b, _WHOLE_VMEM, _WHOLE_VMEM, _WHOLE_VMEM, vec, per_b],
        out_specs=tok,
        out_shape=jax.ShapeDtypeStruct((bsz, t, d), F32),
        compiler_params=_params(2),
        name="swiglu_ffn",
    )(x, g_in.astype(F32).reshape(1, d), shift, scale, w_gate, w_up, w_out,
      g_out.astype(F32).reshape(1, d), mod_gate)


def _rot_cols(w, half):
    return jnp.concatenate([-w[:, half:], w[:, :half]], axis=1)


def _rot_heads(w, heads, dim):
    k = w.shape[0]
    w3 = w.reshape(k, heads, dim)
    return jnp.concatenate([-w3[:, :, dim // 2:], w3[:, :, :dim // 2]], axis=2).reshape(k, heads * dim)


def _dup_heads(w, heads, dim):
    k = w.shape[0]
    w3 = w.reshape(k, heads, 1, dim)
    return jnp.broadcast_to(w3, (k, heads, 2, dim)).reshape(k, heads * 2 * dim)


def _rope_tables(rows, dim):
    row = jnp.repeat(jnp.arange(rows, dtype=F32), GRID_W)
    col = jnp.tile(jnp.arange(GRID_W, dtype=F32), rows)
    n_freq = dim // 4
    inv = ROPE_BASE ** (-jnp.arange(n_freq, dtype=F32) / n_freq)
    ang = jnp.concatenate([row[:, None] * inv, col[:, None] * inv], axis=-1)
    return jnp.cos(ang), jnp.sin(ang)


def _mla_tables(t, rope):
    ones = jnp.ones((t, MLA_NOPE), F32)
    zeros_n = jnp.zeros((t, MLA_NOPE), F32)
    pad = jnp.zeros((t, LANES - MLA_NOPE - MLA_ROPE), F32)
    if rope is None:
        c = jnp.ones((t, MLA_ROPE), F32)
        s = jnp.zeros((t, MLA_ROPE), F32)
    else:
        c = jnp.concatenate([rope[0], rope[0]], axis=1)
        s = jnp.concatenate([rope[1], rope[1]], axis=1)
    plain_q = jnp.concatenate([ones, c, pad], axis=1)
    rot = jnp.concatenate([zeros_n, s, pad], axis=1)
    plain_k = jnp.concatenate([zeros_n, c, pad], axis=1)
    return plain_q, rot, plain_k


def _swa_tables(t, rope):
    if rope is None:
        return jnp.ones((t, LANES), F32), jnp.zeros((t, LANES), F32)
    c = jnp.concatenate([rope[0]] * 4, axis=1)
    s = jnp.concatenate([rope[1]] * 4, axis=1)
    return c, s


def _layer_weights(w_in, mla_w_uq, mla_w_ukv):
    dm = w_in.shape[0]
    sizes = (384, 256, MLA_ROPE, SSD_INNER, SSD_INNER + 2 * SSD_GROUPS * SSD_STATE, 2 * SSD_HEADS,
             SWA_HEADS * SWA_HEAD_DIM, 2 * SWA_KV_HEADS * SWA_HEAD_DIM, 3 * HY_WIDTH, N_BRANCH * dm)
    offs = [0]
    for s in sizes:
        offs.append(offs[-1] + s)
    seg = lambda i: w_in[:, offs[i]:offs[i + 1]]
    cast = lambda w: w.astype(MXU_DTYPE)
    pad_kr = lambda w: jnp.pad(w, ((0, 0), (MLA_NOPE, LANES - MLA_NOPE - MLA_ROPE)))
    w_kr = seg(2)
    kvw = SWA_KV_HEADS * SWA_HEAD_DIM
    w_swk, w_swv = seg(7)[:, :kvw], seg(7)[:, kvw:]
    w = {
        "qa": cast(seg(0)), "ckv": cast(seg(1)),
        "krp": cast(pad_kr(w_kr)), "krr": cast(pad_kr(_rot_cols(w_kr, MLA_ROPE // 2))),
        "z": cast(seg(3)), "xbc": cast(seg(4)), "dt": cast(seg(5)),
        "swq": cast(seg(6)), "swq_rot": cast(_rot_heads(seg(6), SWA_HEADS, SWA_HEAD_DIM)),
        "swk": cast(_dup_heads(w_swk, SWA_KV_HEADS, SWA_HEAD_DIM)),
        "swk_rot": cast(_dup_heads(_rot_heads(w_swk, SWA_KV_HEADS, SWA_HEAD_DIM), SWA_KV_HEADS, SWA_HEAD_DIM)),
        "swv": cast(w_swv),
        "hy": cast(seg(8)), "gate": cast(seg(9)),
    }
    kq = mla_w_uq.shape[0]
    dq = MLA_NOPE + MLA_ROPE
    uq = mla_w_uq.reshape(kq, MLA_HEADS, dq)
    padq = ((0, 0), (0, 0), (0, LANES - dq))
    w["uq"] = cast(jnp.pad(uq, padq).reshape(kq, MLA_HEADS * LANES))
    uq_rope = uq[:, :, MLA_NOPE:]
    uq_rot = jnp.concatenate([-uq_rope[:, :, MLA_ROPE // 2:], uq_rope[:, :, :MLA_ROPE // 2]], axis=2)
    uq_rot = jnp.pad(uq_rot, ((0, 0), (0, 0), (MLA_NOPE, LANES - dq)))
    w["uq_rot"] = cast(uq_rot.reshape(kq, MLA_HEADS * LANES))
    kk = mla_w_ukv.shape[0]
    ukv = mla_w_ukv.reshape(kk, MLA_HEADS, MLA_NOPE + MLA_V)
    w["uk"] = cast(jnp.pad(ukv[:, :, :MLA_NOPE], ((0, 0), (0, 0), (0, LANES - MLA_NOPE))).reshape(kk, MLA_HEADS * LANES))
    w["uv"] = cast(ukv[:, :, MLA_NOPE:].reshape(kk, MLA_HEADS * MLA_V))
    return w


def _token_mixers(x, xc, mod_l, mod_c, ctx_out, rope_mla, rope_swa, norm_g0, w, p, dft_l, dft_c):
    bsz, t, dm = x.shape
    s_c = xc.shape[1]
    bf = MXU_DTYPE

    lat_names = ["qa", "ckv", "krp", "krr", "z", "xbc", "dt", "swq", "swq_rot", "swk", "swk_rot", "swv", "hy", "gate"]
    lat_dt = [bf, bf, bf, bf, bf, bf, F32, bf, bf, bf, bf, bf, bf, bf]
    lat = dict(zip(lat_names, _norm_matmul(x, norm_g0, [w[k] for k in lat_names], lat_dt,
                                           shift=mod_l[0], scale=mod_l[1], name="in_proj_latent")))
    ctx_names = ["ckv", "krp", "xbc", "dt", "swk", "swv"] + (["qa", "z", "swq", "hy", "gate"] if ctx_out else [])
    ctx_dt = [F32 if k == "dt" else bf for k in ctx_names]
    ctx = dict(zip(ctx_names, _norm_matmul(xc, norm_g0, [w[k] for k in ctx_names], ctx_dt,
                                           shift=mod_c[0], scale=mod_c[1], name="in_proj_context")))

    knp_l, v_l = _norm_matmul(lat["ckv"], p["mla_kv_norm"], [w["uk"], w["uv"]], [bf, bf], name="mla_kv_up")
    knp_c, v_c = _norm_matmul(ctx["ckv"], p["mla_kv_norm"], [w["uk"], w["uv"]], [bf, bf], name="mla_kv_up_ctx")
    qp, qr = _norm_matmul(lat["qa"], p["mla_q_norm"], [w["uq"], w["uq_rot"]], [bf, bf], name="mla_q_up")
    cq, sq, ck = _mla_tables(t, rope_mla)
    mla_q_scale = MLA_SCALE * LOG2E
    swa_q_scale = SWA_SCALE * LOG2E
    mla_l = _mla_attention(qp, qr, cq * mla_q_scale, sq * mla_q_scale, knp_l, lat["krp"], lat["krr"], ck, sq, v_l,
                           ctx=(knp_c, ctx["krp"], v_c))

    xbc = _dwconv([ctx["xbc"], lat["xbc"]], p["ssd_conv_w"], p["ssd_conv_b"], act=True, name="ssd_conv")
    dt_raw = jnp.concatenate([ctx["dt"], lat["dt"]], axis=1)
    yf, yb = _ssd_scan(xbc, dt_raw, jnp.swapaxes(dt_raw, 1, 2), p["ssd_dt_bias"], p["ssd_a_log"], s_c // SSD_CHUNK)
    ssd_l = _ssd_output(yf, yb, xbc, lat["z"], p["ssd_d"], p["ssd_norm"], s_c)

    csw, ssw = _swa_tables(t, rope_swa)
    swa_l = _swa_attention(p["swa_sink"], lat["swq"], lat["swq_rot"], csw * swa_q_scale, ssw * swa_q_scale,
                           ctx["swk"], ctx["swv"], lat=(lat["swk"], lat["swk_rot"], csw, ssw, lat["swv"]))

    hy_args = (p["hy_w1"], p["hy_b1"], p["hy_w2"], p["hy_b2"], p["hy_w3"], p["hy_freq"])
    spec_l = _hyena_filters(t, *hy_args, dft_l[0], dft_l[1])
    hy_l = _hyena(lat["hy"], p["hy_conv_w"], p["hy_conv_b"], spec_l, p["hy_d"], dft_l)

    w_branch = p["w_branch"].astype(bf)
    w_out = p["w_out"].astype(bf)
    x_new = _merge([mla_l, ssd_l, swa_l, hy_l], lat["gate"], w_branch, w_out, x, mod_l[2], p["norm_g1"])
    if not ctx_out:
        return x_new, None

    qp_c, = _norm_matmul(ctx["qa"], p["mla_q_norm"], [w["uq"]], [bf], name="mla_q_up_ctx")
    cq_c, sq_c, ck_c = _mla_tables(s_c, None)
    mla_c = _mla_attention(qp_c, qp_c, cq_c * mla_q_scale, sq_c, knp_c, ctx["krp"], ctx["krp"], ck_c, sq_c, v_c)
    ssd_c = _ssd_output(yf, yb, xbc, ctx["z"], p["ssd_d"], p["ssd_norm"], 0)
    c1, s0 = _swa_tables(s_c, None)
    swa_c = _swa_attention(p["swa_sink"], ctx["swq"], ctx["swq"], c1 * swa_q_scale, s0, ctx["swk"], ctx["swv"])
    spec_c = _hyena_filters(s_c, *hy_args, dft_c[0], dft_c[1])
    hy_c = _hyena(ctx["hy"], p["hy_conv_w"], p["hy_conv_b"], spec_c, p["hy_d"], dft_c)
    xc_new = _merge([mla_c, ssd_c, swa_c, hy_c], ctx["gate"], w_branch, w_out, xc, mod_c[2], p["norm_g1"])
    return x_new, xc_new


def kernel(x, c, ctx, c_ctx, ada_w, ada_b, norm_g, w_in, mla_q_norm, mla_w_uq, mla_kv_norm, mla_w_ukv, ssd_conv_w, ssd_conv_b, ssd_dt_bias, ssd_a_log, ssd_d, ssd_norm, swa_sink, hy_conv_w, hy_conv_b, hy_w1, hy_b1, hy_w2, hy_b2, hy_w3, hy_freq, hy_d, w_branch, w_out, ffn_w_in, ffn_w_out):
    bsz, t, dm = x.shape
    s_c = ctx.shape[1]
    depth = ada_w.shape[0]
    rows = t // GRID_W
    rope_mla = _rope_tables(rows, MLA_ROPE)
    rope_swa = _rope_tables(rows, SWA_HEAD_DIM)
    dft_l = _dft_tables(t)
    dft_c = _dft_tables(s_c)
    cond_rows = 16
    cond = jnp.concatenate([c, c_ctx[None, :], jnp.zeros((cond_rows - bsz - 1, dm), F32)], axis=0)
    ffn_hidden = ffn_w_out.shape[1]
    xc = ctx
    for l in range(depth):
        ctx_out = l < depth - 1
        mod = _ada(cond, ada_w[l], ada_b[l])
        mod_l = [mod[:bsz, k * dm:(k + 1) * dm].reshape(bsz, 1, dm) for k in range(6)]
        mod_c = [jnp.broadcast_to(mod[bsz, k * dm:(k + 1) * dm].reshape(1, 1, dm), (bsz, 1, dm)) for k in range(6)]
        w = _layer_weights(w_in[l], mla_w_uq[l], mla_w_ukv[l])
        p = {"mla_q_norm": mla_q_norm[l], "mla_kv_norm": mla_kv_norm[l], "ssd_conv_w": ssd_conv_w[l],
             "ssd_conv_b": ssd_conv_b[l], "ssd_dt_bias": ssd_dt_bias[l], "ssd_a_log": ssd_a_log[l],
             "ssd_d": ssd_d[l], "ssd_norm": ssd_norm[l], "swa_sink": swa_sink[l], "hy_conv_w": hy_conv_w[l],
             "hy_conv_b": hy_conv_b[l], "hy_w1": hy_w1[l], "hy_b1": hy_b1[l], "hy_w2": hy_w2[l],
             "hy_b2": hy_b2[l], "hy_w3": hy_w3[l], "hy_freq": hy_freq[l], "hy_d": hy_d[l],
             "w_branch": w_branch[l], "w_out": w_out[l], "norm_g1": norm_g[l, 1]}
        x, xc_new = _token_mixers(x, xc, mod_l, mod_c, ctx_out, rope_mla, rope_swa, norm_g[l, 0], w, p, dft_l, dft_c)
        wg = ffn_w_in[l][:, :ffn_hidden].astype(MXU_DTYPE)
        wu = ffn_w_in[l][:, ffn_hidden:].astype(MXU_DTYPE)
        wo = ffn_w_out[l].astype(MXU_DTYPE)
        x = _ffn(x, norm_g[l, 2], mod_l[3], mod_l[4], wg, wu, wo, norm_g[l, 3], mod_l[5])
        if ctx_out:
            xc = _ffn(xc_new, norm_g[l, 2], mod_c[3], mod_c[4], wg, wu, wo, norm_g[l, 3], mod_c[5])
    return x
```

```python
import functools
import math

import jax
import jax.numpy as jnp
from jax import lax
from jax.experimental import pallas as pl
from jax.experimental.pallas import tpu as pltpu

F32 = jnp.float32
MXU_DTYPE = jnp.bfloat16
HIGHEST = lax.Precision.HIGHEST

GRID_W = 64
EPS = 1e-6
ROPE_BASE = 10000.0
MLA_HEADS, MLA_NOPE, MLA_ROPE, MLA_V = 8, 64, 32, 64
MLA_SCALE = (MLA_NOPE + MLA_ROPE) ** -0.5
SSD_HEADS, SSD_HEADDIM, SSD_GROUPS, SSD_STATE, SSD_CHUNK = 8, 64, 2, 64, 128
SSD_INNER = SSD_HEADS * SSD_HEADDIM
SWA_HEADS, SWA_KV_HEADS, SWA_HEAD_DIM, SWA_WINDOW, SWA_BLOCK = 8, 2, 64, 128, 128
SWA_SCALE = SWA_HEAD_DIM ** -0.5
HY_WIDTH, HY_ORDER, HY_BANDS, HY_HIDDEN = 512, 2, 16, 64
HY_DECAY_TARGET, HY_FAST_DECAY, HY_SLOW_DECAY = 1e-2, 0.3, 1.5
N_BRANCH = 4
LANES = 128
SSD_BATCH_ROWS = 4
VT_ROWS = 80
LOG2E = math.log2(math.e)
DFT_SPLIT = 8
CONV_EDGE = 16
MLA_KEY_BLOCK = 256

VMEM_LIMIT = 56 * 1024 * 1024
_WHOLE_VMEM = pl.BlockSpec(memory_space=pltpu.VMEM)


def _params(n_grid, vmem=VMEM_LIMIT):
    return pltpu.CompilerParams(dimension_semantics=("arbitrary",) * n_grid, vmem_limit_bytes=vmem)


def _silu(x):
    return x * jax.nn.sigmoid(x)


def _softplus(x):
    return jnp.maximum(x, 0.0) + jnp.log1p(jnp.exp(-jnp.abs(x)))


def _rms(x, g):
    return x * lax.rsqrt(jnp.mean(x * x, axis=-1, keepdims=True) + EPS) * g


def _ada_kernel(s_ref, w_ref, b_ref, o_ref):
    s = _silu(s_ref[...])
    o_ref[...] = jnp.dot(s, w_ref[...], preferred_element_type=F32, precision=HIGHEST) + b_ref[...]


def _ada(cond, w, b, tn=1536):
    m, k = cond.shape
    n = w.shape[1]
    return pl.pallas_call(
        _ada_kernel,
        grid=(n // tn,),
        in_specs=[pl.BlockSpec((m, k), lambda j: (0, 0)),
                  pl.BlockSpec((k, tn), lambda j: (0, j)),
                  pl.BlockSpec((1, tn), lambda j: (0, j))],
        out_specs=pl.BlockSpec((m, tn), lambda j: (0, j)),
        out_shape=jax.ShapeDtypeStruct((m, n), F32),
        compiler_params=_params(1),
        name="ada_ln",
    )(cond, w, b.reshape(1, n))


def _norm_mm_kernel(*refs, n_w, modulate):
    it = iter(refs)
    x_ref, g_ref = next(it), next(it)
    if modulate:
        sh_ref, sc_ref = next(it), next(it)
    w_refs = [next(it) for _ in range(n_w)]
    o_refs = [next(it) for _ in range(n_w)]
    a = _rms(x_ref[...].astype(F32), g_ref[...])
    if modulate:
        a = a * (1.0 + sc_ref[...]) + sh_ref[...]
    a = a.astype(MXU_DTYPE)
    for w_ref, o_ref in zip(w_refs, o_refs):
        n = w_ref.shape[1]
        for c0 in range(0, n, 1024):
            c1 = min(n, c0 + 1024)
            o_ref[:, c0:c1] = jnp.dot(a, w_ref[:, c0:c1], preferred_element_type=F32).astype(o_ref.dtype)


def _norm_matmul(x, g, ws, out_dtypes, shift=None, scale=None, tm=512, name="norm_matmul"):
    bsz, t, k = x.shape
    tm = min(tm, t)
    modulate = shift is not None
    in_specs = [pl.BlockSpec((None, tm, k), lambda b, i: (b, i, 0)),
                pl.BlockSpec((1, k), lambda b, i: (0, 0))]
    args = [x, g.reshape(1, k).astype(F32)]
    if modulate:
        in_specs += [pl.BlockSpec((None, 1, k), lambda b, i: (b, 0, 0))] * 2
        args += [shift, scale]
    in_specs += [_WHOLE_VMEM] * len(ws)
    args += list(ws)
    return pl.pallas_call(
        functools.partial(_norm_mm_kernel, n_w=len(ws), modulate=modulate),
        grid=(bsz, t // tm),
        in_specs=in_specs,
        out_specs=[pl.BlockSpec((None, tm, w.shape[1]), lambda b, i: (b, i, 0)) for w in ws],
        out_shape=[jax.ShapeDtypeStruct((bsz, t, w.shape[1]), dt) for w, dt in zip(ws, out_dtypes)],
        compiler_params=_params(2),
        name=name,
    )(*args)


def _mla_kernel(*refs, t_l, s_c):
    if s_c:
        (qp, qr, cq, sq, knp_l, krp_l, krr_l, ck, sk, v_l, knp_c, krp_c, v_c, o_ref, kcat, vt) = refs
    else:
        (qp, qr, cq, sq, knp_l, krp_l, krr_l, ck, sk, v_l, o_ref, kcat, vt) = refs

    @pl.when(pl.program_id(1) == 0)
    def _():
        kro = (krp_l[...].astype(F32) * ck[...] + krr_l[...].astype(F32) * sk[...]).astype(kcat.dtype)
        for h in range(MLA_HEADS):
            blk = slice(h * LANES, (h + 1) * LANES)
            kcat[h, 0:t_l, :] = knp_l[:, blk] + kro
            if s_c:
                kcat[h, t_l:t_l + s_c, :] = knp_c[:, blk] + krp_c[...]
            vt[h, MLA_V:, :] = jnp.ones((VT_ROWS - MLA_V, t_l + s_c), vt.dtype)
        for j in range(MLA_HEADS // 2):
            blk = slice(j * LANES, (j + 1) * LANES)
            vp = v_l[:, blk].astype(F32).T
            vt[2 * j, 0:MLA_V, 0:t_l] = vp[0:MLA_V].astype(vt.dtype)
            vt[2 * j + 1, 0:MLA_V, 0:t_l] = vp[MLA_V:].astype(vt.dtype)
            if s_c:
                vp = v_c[:, blk].astype(F32).T
                vt[2 * j, 0:MLA_V, t_l:t_l + s_c] = vp[0:MLA_V].astype(vt.dtype)
                vt[2 * j + 1, 0:MLA_V, t_l:t_l + s_c] = vp[MLA_V:].astype(vt.dtype)

    cqv, sqv = cq[...], sq[...]
    tq = qp.shape[0]
    s_tot = t_l + s_c
    nt = (((1,), (1,)), ((), ()))
    heads = range(MLA_HEADS)
    q, m, acc = {}, {}, {}
    for h in heads:
        blk = slice(h * LANES, (h + 1) * LANES)
        q[h] = (qp[:, blk].astype(F32) * cqv + qr[:, blk].astype(F32) * sqv).astype(MXU_DTYPE)
        m[h] = jnp.full((1, tq), -1e30, F32)
        acc[h] = jnp.zeros((VT_ROWS, tq), F32)
    for k0 in range(0, s_tot, MLA_KEY_BLOCK):
        k1 = min(s_tot, k0 + MLA_KEY_BLOCK)
        s = {h: lax.dot_general(kcat[h, k0:k1, :], q[h], nt, preferred_element_type=F32) for h in heads}
        m_new = {h: jnp.maximum(m[h], jnp.max(s[h], axis=0, keepdims=True)) for h in heads}
        p = {h: jnp.exp2(s[h] - m_new[h]).astype(MXU_DTYPE) for h in heads}
        pv = {h: jnp.dot(vt[h, :, k0:k1], p[h], preferred_element_type=F32) for h in heads}
        for h in heads:
            acc[h] = acc[h] * jnp.exp2(m[h] - m_new[h]) + pv[h]
            m[h] = m_new[h]
    for j in range(MLA_HEADS // 2):
        halves = [acc[h][0:MLA_V] * (1.0 / acc[h][MLA_V:MLA_V + 1]) for h in (2 * j, 2 * j + 1)]
        o_ref[:, j * LANES:(j + 1) * LANES] = jnp.concatenate(halves, axis=0).T.astype(o_ref.dtype)


def _mla_attention(qp, qr, cq, sq, knp_l, krp_l, krr_l, ck, sk, v_l, ctx=None, tq=512):
    bsz, t_q, _ = qp.shape
    t_l = knp_l.shape[1]
    s_c = 0 if ctx is None else ctx[0].shape[1]
    tq = min(tq, t_q)
    hw = MLA_HEADS * LANES
    vw = MLA_HEADS * MLA_V
    per_b = lambda rows, cols: pl.BlockSpec((None, rows, cols), lambda b, i: (b, 0, 0))
    in_specs = [pl.BlockSpec((None, tq, hw), lambda b, i: (b, i, 0)),
                pl.BlockSpec((None, tq, hw), lambda b, i: (b, i, 0)),
                pl.BlockSpec((tq, LANES), lambda b, i: (i, 0)),
                pl.BlockSpec((tq, LANES), lambda b, i: (i, 0)),
                per_b(t_l, hw), per_b(t_l, LANES), per_b(t_l, LANES),
                pl.BlockSpec((t_l, LANES), lambda b, i: (0, 0)),
                pl.BlockSpec((t_l, LANES), lambda b, i: (0, 0)),
                per_b(t_l, vw)]
    args = [qp, qr, cq, sq, knp_l, krp_l, krr_l, ck, sk, v_l]
    if s_c:
        in_specs += [per_b(s_c, hw), per_b(s_c, LANES), per_b(s_c, vw)]
        args += list(ctx)
    return pl.pallas_call(
        functools.partial(_mla_kernel, t_l=t_l, s_c=s_c),
        grid=(bsz, t_q // tq),
        in_specs=in_specs,
        out_specs=pl.BlockSpec((None, tq, vw), lambda b, i: (b, i, 0)),
        out_shape=jax.ShapeDtypeStruct((bsz, t_q, vw), MXU_DTYPE),
        scratch_shapes=[pltpu.VMEM((MLA_HEADS, t_l + s_c, LANES), MXU_DTYPE),
                        pltpu.VMEM((MLA_HEADS, VT_ROWS, t_l + s_c), MXU_DTYPE)],
        compiler_params=_params(2),
        name="mla_attention",
    )(*args)


def _swa_kernel(*refs, band, t_k):
    if band:
        (sink, q, qrot, cq, sq, k, krot, ck, sk, v, kc, vc, o_ref, kro, vt, vct) = refs
    else:
        (sink, q, qrot, cq, sq, kc, vc, o_ref, vct) = refs
    i = pl.program_id(1)
    tq = q.shape[0]
    gw = 2 * SWA_HEAD_DIM
    dh = SWA_HEAD_DIM
    kb_rows = min(tq + 2 * SWA_BLOCK, t_k)

    @pl.when(i == 0)
    def _():
        def put_vt(dst, src):
            vtr = src[...].astype(F32).T
            for g in range(SWA_KV_HEADS):
                dst[g, 0:dh, :] = vtr[g * dh:(g + 1) * dh].astype(dst.dtype)
                dst[g, dh:, :] = jnp.ones((VT_ROWS - dh, src.shape[0]), dst.dtype)

        put_vt(vct, vc)
        if band:
            put_vt(vt, v)
            ckv, skv = ck[...], sk[...]
            for g in range(SWA_KV_HEADS):
                blk = slice(g * gw, (g + 1) * gw)
                kro[:, blk] = (k[:, blk].astype(F32) * ckv + krot[:, blk].astype(F32) * skv).astype(kro.dtype)

    if band:
        start = pl.multiple_of(jnp.clip(i * tq - SWA_BLOCK, 0, t_k - kb_rows), SWA_BLOCK)
        k_pos = start + lax.broadcasted_iota(jnp.int32, (kb_rows, tq), 0)
        q_pos = i * tq + lax.broadcasted_iota(jnp.int32, (kb_rows, tq), 1)
        in_band = jnp.abs(q_pos - k_pos) <= SWA_WINDOW

    lane_lo = lax.broadcasted_iota(jnp.int32, (tq, LANES), 1) < dh
    cqv, sqv = cq[...], sq[...]
    nt = (((1,), (1,)), ((), ()))
    heads = range(SWA_HEADS)
    group = {h: h // (SWA_HEADS // SWA_KV_HEADS) for h in heads}
    gblk = {h: slice(group[h] * gw, (group[h] + 1) * gw) for h in heads}
    qm, snk = {}, {}
    for j in range(SWA_HEADS // 2):
        blk = slice(j * LANES, (j + 1) * LANES)
        qro = q[:, blk].astype(F32) * cqv + qrot[:, blk].astype(F32) * sqv
        qm[2 * j] = jnp.where(lane_lo, qro, 0.0).astype(MXU_DTYPE)
        qm[2 * j + 1] = jnp.where(lane_lo, 0.0, qro).astype(MXU_DTYPE)
    for h in heads:
        snk[h] = sink[h] * LOG2E
    s_c = {h: lax.dot_general(kc[:, gblk[h]], qm[h], nt, preferred_element_type=F32) for h in heads}
    m = {h: jnp.maximum(jnp.max(s_c[h], axis=0, keepdims=True), snk[h]) for h in heads}
    if band:
        s_b = {h: jnp.where(in_band, lax.dot_general(kro[pl.ds(start, kb_rows), gblk[h]], qm[h], nt,
                                                     preferred_element_type=F32), -1e30) for h in heads}
        m = {h: jnp.maximum(m[h], jnp.max(s_b[h], axis=0, keepdims=True)) for h in heads}
    o = {h: jnp.dot(vct[group[h]], jnp.exp2(s_c[h] - m[h]).astype(MXU_DTYPE), preferred_element_type=F32)
         for h in heads}
    if band:
        o = {h: o[h] + jnp.dot(vt[group[h], :, pl.ds(start, kb_rows)], jnp.exp2(s_b[h] - m[h]).astype(MXU_DTYPE),
                               preferred_element_type=F32) for h in heads}
    for j in range(SWA_HEADS // 2):
        halves = []
        for h in (2 * j, 2 * j + 1):
            l = o[h][dh:dh + 1] + jnp.exp2(snk[h] - m[h])
            halves.append(o[h][0:dh] * (1.0 / l))
        o_ref[:, j * LANES:(j + 1) * LANES] = jnp.concatenate(halves, axis=0).T.astype(o_ref.dtype)


def _swa_attention(sink, q, qrot, cq, sq, kc, vc, lat=None):
    bsz, t, hw = q.shape
    s_c = kc.shape[1]
    tq = min(t, 2 * SWA_BLOCK)
    kw, vw = kc.shape[2], vc.shape[2]
    band = lat is not None
    per_b = lambda rows, cols: pl.BlockSpec((None, rows, cols), lambda b, i: (b, 0, 0))
    qspec = pl.BlockSpec((None, tq, hw), lambda b, i: (b, i, 0))
    tspec = pl.BlockSpec((tq, LANES), lambda b, i: (i, 0))
    in_specs = [pl.BlockSpec(memory_space=pltpu.SMEM), qspec, qspec, tspec, tspec]
    args = [sink.astype(F32), q, qrot, cq, sq]
    scratch = []
    if band:
        k, krot, ck, sk, v = lat
        full_t = pl.BlockSpec((t, LANES), lambda b, i: (0, 0))
        in_specs += [per_b(t, kw), per_b(t, kw), full_t, full_t, per_b(t, vw)]
        args += [k, krot, ck, sk, v]
        scratch = [pltpu.VMEM((t, kw), MXU_DTYPE), pltpu.VMEM((SWA_KV_HEADS, VT_ROWS, t), MXU_DTYPE)]
    scratch += [pltpu.VMEM((SWA_KV_HEADS, VT_ROWS, s_c), MXU_DTYPE)]
    in_specs += [per_b(s_c, kw), per_b(s_c, vw)]
    args += [kc, vc]
    return pl.pallas_call(
        functools.partial(_swa_kernel, band=band, t_k=t),
        grid=(bsz, t // tq),
        in_specs=in_specs,
        out_specs=pl.BlockSpec((None, tq, hw), lambda b, i: (b, i, 0)),
        out_shape=jax.ShapeDtypeStruct((bsz, t, hw), MXU_DTYPE),
        scratch_shapes=scratch,
        compiler_params=_params(2),
        name="swa_attention" if band else "ctx_sink_attention",
    )(*args)


def _dwconv_kernel(*refs, n_seg, act):
    x_refs = refs[:n_seg]
    w_ref, b_ref, o_ref = refs[n_seg:]
    taps = w_ref.shape[0]
    off = 0
    for x_ref in x_refs:
        t = x_ref.shape[0]
        x = x_ref[...].astype(F32)
        e = CONV_EDGE
        row = lax.broadcasted_iota(jnp.int32, (e, x.shape[1]), 0)
        acc = jnp.zeros_like(x) + b_ref[...]
        top = jnp.zeros((e, x.shape[1]), F32) + b_ref[...]
        bot = top
        for kk in range(taps):
            d = kk - taps // 2
            wk = w_ref[kk:kk + 1, :]
            xs = x if d == 0 else pltpu.roll(x, (-d) % t, axis=0)
            acc = acc + xs * wk
            top = top + jnp.where(row + d >= 0, xs[0:e], 0.0) * wk
            bot = bot + jnp.where(row + d < e, xs[t - e:t], 0.0) * wk
        if act:
            acc, top, bot = _silu(acc), _silu(top), _silu(bot)
        o_ref[off:off + t, :] = acc.astype(o_ref.dtype)
        o_ref[off:off + e, :] = top.astype(o_ref.dtype)
        o_ref[off + t - e:off + t, :] = bot.astype(o_ref.dtype)
        off += t


def _dwconv(xs, w, b, act, tc=256, name="dwconv"):
    bsz, _, c = xs[0].shape
    t_tot = sum(x.shape[1] for x in xs)
    taps = w.shape[0]
    in_specs = [pl.BlockSpec((None, x.shape[1], tc), lambda bb, j: (bb, 0, j)) for x in xs]
    in_specs += [pl.BlockSpec((taps, tc), lambda bb, j: (0, j)), pl.BlockSpec((1, tc), lambda bb, j: (0, j))]
    return pl.pallas_call(
        functools.partial(_dwconv_kernel, n_seg=len(xs), act=act),
        grid=(bsz, c // tc),
        in_specs=in_specs,
        out_specs=pl.BlockSpec((None, t_tot, tc), lambda bb, j: (bb, 0, j)),
        out_shape=jax.ShapeDtypeStruct((bsz, t_tot, c), MXU_DTYPE),
        compiler_params=_params(2),
        name=name,
    )(*xs, w.astype(F32), b.reshape(1, c).astype(F32))


def _ssd_scan_kernel(xs_f, bm_f, cm_f, dt_f, dtt_f, xs_b, bm_b, cm_b, dt_b, dtt_b,
                     b_row, al_row, b_col, al_col, yf_ref, yb_ref, st_ref):
    @pl.when(pl.program_id(1) == 0)
    def _():
        st_ref[...] = jnp.zeros_like(st_ref)

    q = SSD_CHUNK
    row = lax.broadcasted_iota(jnp.int32, (q, q), 0)
    col = lax.broadcasted_iota(jnp.int32, (q, q), 1)
    lane_lo = col < SSD_HEADDIM
    a_row = -jnp.exp(al_row[...])
    a_col = -jnp.exp(al_col[...])
    nt = (((1,), (1,)), ((), ()))
    tn = (((0,), (0,)), ((), ()))
    dirs = ((xs_f, bm_f, cm_f, dt_f, dtt_f, yf_ref), (xs_b, bm_b, cm_b, dt_b, dtt_b, yb_ref))
    units = [(bi, d) for bi in range(st_ref.shape[0]) for d in range(2)]
    pairs_per_group = SSD_HEADS // SSD_GROUPS // 2
    feeds = {0: row >= col, 1: row <= col}
    last = {0: q - 1, 1: 0}
    dt, dtt, cs, cst, cg, bg, gmat, sc, ecol, wcol, dec = ({} for _ in range(11))
    for u in units:
        bi, d = u
        dt_ref, dtt_ref = dirs[d][3], dirs[d][4]
        dt[u] = _softplus(dt_ref[bi] + b_row[...])
        dtt[u] = _softplus(dtt_ref[bi] + b_col[...])
    for u in units:
        d = u[1]
        cs[u] = jnp.dot(feeds[d].astype(F32), dt[u] * a_row, preferred_element_type=F32, precision=HIGHEST)
        cst[u] = jnp.dot(dtt[u] * a_col, feeds[1 - d].astype(F32), preferred_element_type=F32, precision=HIGHEST)
    for u in units:
        bi, d = u
        bblk, cblk = dirs[d][1][bi].astype(F32), dirs[d][2][bi].astype(F32)
        for g in range(SSD_GROUPS):
            gmask = (col // SSD_STATE) == g
            cg[u, g] = jnp.where(gmask, cblk, 0.0).astype(MXU_DTYPE)
            bg[u, g] = jnp.where(gmask, bblk, 0.0)
            gmat[u, g] = lax.dot_general(cg[u, g], bg[u, g].astype(MXU_DTYPE), nt,
                                         preferred_element_type=F32)
    for u in units:
        d = u[1]
        tot = cs[u][last[d]:last[d] + 1, :]
        e_all = jnp.exp(cs[u])
        w_all = dt[u] * jnp.exp(tot - cs[u])
        d_all = jnp.exp(tot)
        for h in range(SSD_HEADS):
            c = d * SSD_HEADS + h
            ccol = cs[u][:, c:c + 1]
            crow = cst[u][c:c + 1, :]
            lmat = jnp.where(feeds[d], jnp.exp(ccol - crow), 0.0)
            sc[u, h] = (gmat[u, h // (SSD_HEADS // SSD_GROUPS)] * lmat * dtt[u][c:c + 1, :]).astype(MXU_DTYPE)
            ecol[u, h] = e_all[:, c:c + 1]
            wcol[u, h] = w_all[:, c:c + 1]
            dec[u, h] = d_all[:, c:c + 1]
    for u in units:
        bi, d = u
        xs_ref, y_ref = dirs[d][0], dirs[d][5]
        for j in range(SSD_HEADS // 2):
            g = j // pairs_per_group
            xb = xs_ref[bi, :, j * LANES:(j + 1) * LANES]
            xblk = xb.astype(F32)
            s_in = st_ref[bi, d, j]
            y_inter = jnp.dot(cg[u, g], s_in.astype(MXU_DTYPE), preferred_element_type=F32)
            ys, news = [], []
            for hh in range(2):
                h = 2 * j + hh
                ys.append(jnp.dot(sc[u, h], xb, preferred_element_type=F32) + ecol[u, h] * y_inter)
                bw = (bg[u, g] * wcol[u, h]).astype(MXU_DTYPE)
                xh = jnp.where(lane_lo if hh == 0 else ~lane_lo, xblk, 0.0).astype(MXU_DTYPE)
                news.append(lax.dot_general(bw, xh, tn, preferred_element_type=F32))
            y_ref[bi, :, j * LANES:(j + 1) * LANES] = jnp.where(lane_lo, ys[0], ys[1]).astype(y_ref.dtype)
            st_ref[bi, d, j] = (s_in * jnp.where(lane_lo, dec[u, 2 * j], dec[u, 2 * j + 1])
                                + news[0] + news[1])


def _ssd_scan(xbc, dt_raw, dt_raw_t, dt_bias, a_log, nc_ctx):
    bsz, t_c, _ = xbc.shape
    q = SSD_CHUNK
    nc = t_c // q
    nh2 = 2 * SSD_HEADS

    def fwd(s):
        return jnp.where(s < nc_ctx, nc - nc_ctx + s, s - nc_ctx)

    def bwd(s):
        return nc - 1 - s

    bt = math.gcd(bsz, SSD_BATCH_ROWS)

    def specs(order):
        return [pl.BlockSpec((bt, q, SSD_INNER), lambda b, s: (b, order(s), 0)),
                pl.BlockSpec((bt, q, LANES), lambda b, s: (b, order(s), SSD_INNER // LANES)),
                pl.BlockSpec((bt, q, LANES), lambda b, s: (b, order(s), SSD_INNER // LANES + 1)),
                pl.BlockSpec((bt, q, nh2), lambda b, s: (b, order(s), 0)),
                pl.BlockSpec((bt, nh2, q), lambda b, s: (b, 0, order(s)))]

    small = lambda r, c: pl.BlockSpec((r, c), lambda b, s: (0, 0))
    bias = dt_bias.astype(F32).reshape(1, nh2)
    alog = a_log.astype(F32).reshape(1, nh2)
    return pl.pallas_call(
        _ssd_scan_kernel,
        grid=(bsz // bt, nc),
        in_specs=specs(fwd) + specs(bwd) + [small(1, nh2), small(1, nh2), small(nh2, 1), small(nh2, 1)],
        out_specs=[pl.BlockSpec((bt, q, SSD_INNER), lambda b, s: (b, fwd(s), 0)),
                   pl.BlockSpec((bt, q, SSD_INNER), lambda b, s: (b, bwd(s), 0))],
        out_shape=[jax.ShapeDtypeStruct((bsz, t_c, SSD_INNER), MXU_DTYPE)] * 2,
        scratch_shapes=[pltpu.VMEM((bt, 2, SSD_HEADS // 2, LANES, LANES), F32)],
        compiler_params=_params(2),
        name="ssd_scan",
    )(xbc, xbc, xbc, dt_raw, dt_raw_t, xbc, xbc, xbc, dt_raw, dt_raw_t,
      bias, alog, bias.reshape(nh2, 1), alog.reshape(nh2, 1))


def _ssd_out_kernel(yf, yb, xs, z, dexp, ng, o_ref):
    y = yf[...].astype(F32) + yb[...].astype(F32) + dexp[...] * xs[...].astype(F32)
    yg = y * _silu(z[...].astype(F32))
    gw = SSD_INNER // SSD_GROUPS
    for g in range(SSD_GROUPS):
        blk = slice(g * gw, (g + 1) * gw)
        o_ref[:, blk] = _rms(yg[:, blk], ng[:, blk]).astype(o_ref.dtype)


def _ssd_output(yf, yb, xbc, z, d_skip, norm_g, row0, tm=256):
    bsz, t, _ = z.shape
    tm = min(tm, t)
    off = row0 // tm
    sp_y = pl.BlockSpec((None, tm, SSD_INNER), lambda b, i: (b, i + off, 0))
    vec = pl.BlockSpec((1, SSD_INNER), lambda b, i: (0, 0))
    dexp = jnp.repeat(d_skip.astype(F32), SSD_HEADDIM).reshape(1, SSD_INNER)
    return pl.pallas_call(
        _ssd_out_kernel,
        grid=(bsz, t // tm),
        in_specs=[sp_y, sp_y, sp_y, pl.BlockSpec((None, tm, SSD_INNER), lambda b, i: (b, i, 0)), vec, vec],
        out_specs=pl.BlockSpec((None, tm, SSD_INNER), lambda b, i: (b, i, 0)),
        out_shape=jax.ShapeDtypeStruct((bsz, t, SSD_INNER), MXU_DTYPE),
        compiler_params=_params(2),
        name="ssd_output",
    )(yf, yb, xbc, z, dexp, norm_g.astype(F32).reshape(1, SSD_INNER))


def _hy_mlp_kernel(feat_ref, w1, b1, w2, b2, w3, fr, delta, h_ref, cs_ref):
    feat = feat_ref[...]
    h = jnp.sin(fr[...] * (jnp.dot(feat, w1[...], preferred_element_type=F32, precision=HIGHEST) + b1[...]))
    h = jnp.sin(fr[...] * (jnp.dot(h, w2[...], preferred_element_type=F32, precision=HIGHEST) + b2[...]))
    h = jnp.dot(h, w3[...], preferred_element_type=F32, precision=HIGHEST)
    h = h * jnp.exp(-feat[:, 0:1] * delta[...])
    h_ref[...] = h
    s = jnp.sum(jnp.abs(h), axis=0, keepdims=True)

    @pl.when(pl.program_id(0) == 0)
    def _():
        cs_ref[...] = s

    @pl.when(pl.program_id(0) != 0)
    def _():
        cs_ref[...] += s


def _hy_spectrum_kernel(h0_ref, h1_ref, c0_ref, c1_ref, cm_ref, sf_ref, ar_ref, ai_ref, br_ref):
    n = h0_ref.shape[0]
    inv = 1.0 / (c0_ref[...] + c1_ref[...])
    row = lax.broadcasted_iota(jnp.int32, h0_ref.shape, 0)
    first = row == 0
    h0 = h0_ref[...] * inv
    h1 = jnp.where(first, 0.0, h1_ref[...] * inv)
    a = h0 + h1
    kr = jnp.dot(cm_ref[...], a.astype(MXU_DTYPE), preferred_element_type=F32)
    kq = jnp.dot(sf_ref[...], (h1 - h0).astype(MXU_DTYPE), preferred_element_type=F32)
    k_nyq = jnp.sum(jnp.where(row % 2 == 0, a, -a), axis=0, keepdims=True)
    inv_n = 1.0 / (2 * n)
    ar = kr * jnp.where(first, inv_n, 2.0 * inv_n)
    ar_ref[...] = ar
    ai_ref[...] = jnp.where(first, 0.0, kq * (2.0 * inv_n))
    br_ref[...] = jnp.where(first, k_nyq * inv_n, ar)


def _hyena_filters(n, w1, b1, w2, b2, w3, freq, cm, sf):
    t = jnp.arange(n, dtype=F32)
    tnorm = t / n
    bands = jnp.linspace(1e-4, HY_BANDS - 1, HY_BANDS, dtype=F32)
    ang = 2 * math.pi * t[:, None] * bands[None, :] / n
    feat = jnp.concatenate([tnorm[:, None], jnp.cos(ang), -jnp.sin(ang)], axis=-1)
    emb = feat.shape[1]
    feat = jnp.pad(feat, ((0, 0), (0, HY_HIDDEN - emb)))
    w1p = jnp.pad(w1.astype(F32), ((0, HY_HIDDEN - emb), (0, 0)))
    deltas = jnp.abs(jnp.linspace(math.log(HY_DECAY_TARGET) / HY_SLOW_DECAY,
                                  math.log(HY_DECAY_TARGET) / HY_FAST_DECAY, HY_WIDTH, dtype=F32))
    ncol = HY_ORDER * 2 * HY_WIDTH
    delta_row = jnp.tile(deltas, HY_ORDER * 2).reshape(1, ncol)
    tt = min(n, 256)
    small = lambda r, c: pl.BlockSpec((r, c), lambda i: (0, 0))
    row = lambda v: v.astype(F32).reshape(1, -1)
    h, colsum = pl.pallas_call(
        _hy_mlp_kernel,
        grid=(n // tt,),
        in_specs=[pl.BlockSpec((tt, HY_HIDDEN), lambda i: (i, 0)),
                  small(HY_HIDDEN, HY_HIDDEN), small(1, HY_HIDDEN),
                  small(HY_HIDDEN, HY_HIDDEN), small(1, HY_HIDDEN),
                  small(HY_HIDDEN, ncol), small(1, HY_HIDDEN), small(1, ncol)],
        out_specs=[pl.BlockSpec((tt, ncol), lambda i: (i, 0)), small(1, ncol)],
        out_shape=[jax.ShapeDtypeStruct((n, ncol), F32), jax.ShapeDtypeStruct((1, ncol), F32)],
        compiler_params=_params(1),
        name="hyena_filter_mlp",
    )(feat, w1p, row(b1), w2.astype(F32), row(b2), w3.astype(F32), row(freq), delta_row)

    tc = 256
    per_o = HY_WIDTH // tc
    side0 = lambda jc: (jc // per_o) * 2 * per_o + jc % per_o
    side1 = lambda jc: (jc // per_o) * 2 * per_o + per_o + jc % per_o
    nout = HY_ORDER * HY_WIDTH
    out_spec = pl.BlockSpec((n, tc), lambda jc: (0, jc))
    return pl.pallas_call(
        _hy_spectrum_kernel,
        grid=(nout // tc,),
        in_specs=[pl.BlockSpec((n, tc), lambda jc: (0, side0(jc))),
                  pl.BlockSpec((n, tc), lambda jc: (0, side1(jc))),
                  pl.BlockSpec((1, tc), lambda jc: (0, side0(jc))),
                  pl.BlockSpec((1, tc), lambda jc: (0, side1(jc))),
                  _WHOLE_VMEM, _WHOLE_VMEM],
        out_specs=[out_spec] * 3,
        out_shape=[jax.ShapeDtypeStruct((n, nout), F32)] * 3,
        compiler_params=_params(1),
        name="hyena_filter_spectrum",
    )(h, h, colsum, colsum, cm, sf)


def _hy_fwd_kernel(u_ref, cm_ref, sf_ref, ar_ref, ai_ref, br_ref, yr_ref, yi_ref):
    u = u_ref[...].astype(MXU_DTYPE)
    p = jnp.dot(cm_ref[...], u, preferred_element_type=F32)
    q = jnp.dot(sf_ref[...], u, preferred_element_type=F32)
    ai = ai_ref[...]
    yr_ref[...] = (p * ar_ref[...] + q * ai).astype(yr_ref.dtype)
    yi_ref[...] = (q * br_ref[...] - p * ai).astype(yi_ref.dtype)


def _hy_inv_kernel(yr_ref, yi_ref, cm_ref, si_ref, u_ref, xg_ref, d_ref, o_ref):
    y = (jnp.dot(cm_ref[...], yr_ref[...], preferred_element_type=F32)
         + jnp.dot(si_ref[...], yi_ref[...], preferred_element_type=F32))
    u = u_ref[...].astype(F32)
    o_ref[...] = (xg_ref[...].astype(F32) * (y + u * d_ref[...])).astype(o_ref.dtype)


def _hyena_conv(u, u_col0, xg, xg_col0, spectra, order, d, tables, out_dtype, tc=256):
    cm, sf, si = tables
    ar, ai, br = spectra
    bsz, n, _ = u.shape
    nct = HY_WIDTH // tc
    ucol, gcol, scol = u_col0 // tc, xg_col0 // tc, order * nct
    tok = lambda c0: pl.BlockSpec((None, n, tc), lambda c, b: (b, 0, c0 + c))
    spec_sp = pl.BlockSpec((n, tc), lambda c, b: (0, scol + c))
    mid = pl.BlockSpec((None, n, tc), lambda c, b: (b, 0, c))
    yr, yi = pl.pallas_call(
        _hy_fwd_kernel,
        grid=(nct, bsz),
        in_specs=[tok(ucol), _WHOLE_VMEM, _WHOLE_VMEM, spec_sp, spec_sp, spec_sp],
        out_specs=[mid, mid],
        out_shape=[jax.ShapeDtypeStruct((bsz, n, HY_WIDTH), MXU_DTYPE)] * 2,
        compiler_params=_params(2),
        name="hyena_dft_forward",
    )(u, cm, sf, ar, ai, br)
    return pl.pallas_call(
        _hy_inv_kernel,
        grid=(nct, bsz),
        in_specs=[mid, mid, _WHOLE_VMEM, _WHOLE_VMEM, tok(ucol), tok(gcol),
                  pl.BlockSpec((1, tc), lambda c, b: (0, c))],
        out_specs=mid,
        out_shape=jax.ShapeDtypeStruct((bsz, n, HY_WIDTH), out_dtype),
        compiler_params=_params(2),
        name="hyena_dft_inverse",
    )(yr, yi, cm, si, u, xg, d.astype(F32).reshape(1, HY_WIDTH))


def _dft_tables(n):
    idx = jnp.arange(n, dtype=jnp.int32)
    alt = jnp.where(idx % 2 == 0, 1.0, -1.0).astype(F32)
    def rows(freqs):
        ang = ((freqs[:, None] * idx[None, :]) % (2 * n)).astype(F32) * (math.pi / n)
        return jnp.cos(ang), jnp.sin(ang)
    c_hi, s_hi = rows(idx[::DFT_SPLIT])
    c_lo, s_lo = rows(idx[:DFT_SPLIT])
    cm = (c_hi[:, None, :] * c_lo[None, :, :] - s_hi[:, None, :] * s_lo[None, :, :]).reshape(n, n)
    sm = (s_hi[:, None, :] * c_lo[None, :, :] + c_hi[:, None, :] * s_lo[None, :, :]).reshape(n, n)
    sf = jnp.where(idx[:, None] == 0, alt[None, :], sm)
    si = jnp.where(idx[None, :] == 0, alt[:, None], sm)
    return cm.astype(MXU_DTYPE), sf.astype(MXU_DTYPE), si.astype(MXU_DTYPE)


def _hyena(hy, conv_w, conv_b, spectra, d, tables):
    u3 = _dwconv([hy], conv_w, conv_b, act=False, tc=512, name="hyena_short_conv")
    z = _hyena_conv(u3, 0, u3, HY_WIDTH, spectra, 0, d[0], tables, MXU_DTYPE)
    return _hyena_conv(z, 0, u3, 2 * HY_WIDTH, spectra, 1, d[1], tables, MXU_DTYPE)


def _merge_kernel(b0, b1, b2, b3, gate_ref, wb_ref, wo_ref, x_ref, mg_ref, g_ref, o_ref):
    d = x_ref.shape[1]
    acc = None
    for i, br in enumerate((b0, b1, b2, b3)):
        proj = jnp.dot(br[...], wb_ref[i], preferred_element_type=F32)
        term = jax.nn.sigmoid(gate_ref[:, i * d:(i + 1) * d].astype(F32)) * proj
        acc = term if acc is None else acc + term
    y = jnp.dot(acc.astype(MXU_DTYPE), wo_ref[...], preferred_element_type=F32)
    o_ref[...] = x_ref[...] + mg_ref[...] * _rms(y, g_ref[...])


def _merge(branches, gate, w_branch, w_out, x, mod_gate, g, tm=512):
    bsz, t, d = x.shape
    tm = min(tm, t)
    bw = branches[0].shape[2]
    tok = lambda w: pl.BlockSpec((None, tm, w), lambda b, i: (b, i, 0))
    return pl.pallas_call(
        _merge_kernel,
        grid=(bsz, t // tm),
        in_specs=[tok(bw)] * N_BRANCH + [tok(N_BRANCH * d), _WHOLE_VMEM, _WHOLE_VMEM, tok(d),
                                         pl.BlockSpec((None, 1, d), lambda b, i: (b, 0, 0)),
                                         pl.BlockSpec((1, d), lambda b, i: (0, 0))],
        out_specs=tok(d),
        out_shape=jax.ShapeDtypeStruct((bsz, t, d), F32),
        compiler_params=_params(2),
        name="merge_branches",
    )(*branches, gate, w_branch, w_out, x, mod_gate, g.astype(F32).reshape(1, d))


def _ffn_kernel(x_ref, g_in, sh_ref, sc_ref, wg_ref, wu_ref, wo_ref, g_out, mg_ref, o_ref, *, chunk):
    x = x_ref[...]
    a = (_rms(x, g_in[...]) * (1.0 + sc_ref[...]) + sh_ref[...]).astype(MXU_DTYPE)
    hidden = wg_ref.shape[1]
    acc = None
    for c0 in range(0, hidden, chunk):
        gch = jnp.dot(a, wg_ref[:, c0:c0 + chunk], preferred_element_type=F32)
        uch = jnp.dot(a, wu_ref[:, c0:c0 + chunk], preferred_element_type=F32)
        act = (_silu(gch) * uch).astype(MXU_DTYPE)
        part = jnp.dot(act, wo_ref[c0:c0 + chunk, :], preferred_element_type=F32)
        acc = part if acc is None else acc + part
    o_ref[...] = x + mg_ref[...] * _rms(acc, g_out[...])


def _ffn(x, g_in, shift, scale, w_gate, w_up, w_out, g_out, mod_gate, tm=512):
    bsz, t, d = x.shape
    tm = min(tm, t)
    hidden = w_gate.shape[1]
    chunk = hidden // 2
    tok = pl.BlockSpec((None, tm, d), lambda b, i: (b, i, 0))
    per_b = pl.BlockSpec((None, 1, d), lambda b, i: (b, 0, 0))
    vec = pl.BlockSpec((1, d), lambda b, i: (0, 0))
    return pl.pallas_call(
        functools.partial(_ffn_kernel, chunk=chunk),
        grid=(bsz, t // tm),
        in_specs=[tok, vec, per_b, per_b, _WHOLE_VMEM, _WHOLE_VMEM, _WHOLE_VMEM, vec, per_b],
        out_specs=tok,
        out_shape=jax.ShapeDtypeStruct((bsz, t, d), F32),
        compiler_params=_params(2),
        name="swiglu_ffn",
    )(x, g_in.astype(F32).reshape(1, d), shift, scale, w_gate, w_up, w_out,
      g_out.astype(F32).reshape(1, d), mod_gate)


def _rot_cols(w, half):
    return jnp.concatenate([-w[:, half:], w[:, :half]], axis=1)


def _rot_heads(w, heads, dim):
    k = w.shape[0]
    w3 = w.reshape(k, heads, dim)
    return jnp.concatenate([-w3[:, :, dim // 2:], w3[:, :, :dim // 2]], axis=2).reshape(k, heads * dim)


def _dup_heads(w, heads, dim):
    k = w.shape[0]
    w3 = w.reshape(k, heads, 1, dim)
    return jnp.broadcast_to(w3, (k, heads, 2, dim)).reshape(k, heads * 2 * dim)


def _rope_tables(rows, dim):
    row = jnp.repeat(jnp.arange(rows, dtype=F32), GRID_W)
    col = jnp.tile(jnp.arange(GRID_W, dtype=F32), rows)
    n_freq = dim // 4
    inv = ROPE_BASE ** (-jnp.arange(n_freq, dtype=F32) / n_freq)
    ang = jnp.concatenate([row[:, None] * inv, col[:, None] * inv], axis=-1)
    return jnp.cos(ang), jnp.sin(ang)


def _mla_tables(t, rope):
    ones = jnp.ones((t, MLA_NOPE), F32)
    zeros_n = jnp.zeros((t, MLA_NOPE), F32)
    pad = jnp.zeros((t, LANES - MLA_NOPE - MLA_ROPE), F32)
    if rope is None:
        c = jnp.ones((t, MLA_ROPE), F32)
        s = jnp.zeros((t, MLA_ROPE), F32)
    else:
        c = jnp.concatenate([rope[0], rope[0]], axis=1)
        s = jnp.concatenate([rope[1], rope[1]], axis=1)
    plain_q = jnp.concatenate([ones, c, pad], axis=1)
    rot = jnp.concatenate([zeros_n, s, pad], axis=1)
    plain_k = jnp.concatenate([zeros_n, c, pad], axis=1)
    return plain_q, rot, plain_k


def _swa_tables(t, rope):
    if rope is None:
        return jnp.ones((t, LANES), F32), jnp.zeros((t, LANES), F32)
    c = jnp.concatenate([rope[0]] * 4, axis=1)
    s = jnp.concatenate([rope[1]] * 4, axis=1)
    return c, s


def _layer_weights(w_in, mla_w_uq, mla_w_ukv):
    dm = w_in.shape[0]
    sizes = (384, 256, MLA_ROPE, SSD_INNER, SSD_INNER + 2 * SSD_GROUPS * SSD_STATE, 2 * SSD_HEADS,
             SWA_HEADS * SWA_HEAD_DIM, 2 * SWA_KV_HEADS * SWA_HEAD_DIM, 3 * HY_WIDTH, N_BRANCH * dm)
    offs = [0]
    for s in sizes:
        offs.append(offs[-1] + s)
    seg = lambda i: w_in[:, offs[i]:offs[i + 1]]
    cast = lambda w: w.astype(MXU_DTYPE)
    pad_kr = lambda w: jnp.pad(w, ((0, 0), (MLA_NOPE, LANES - MLA_NOPE - MLA_ROPE)))
    w_kr = seg(2)
    kvw = SWA_KV_HEADS * SWA_HEAD_DIM
    w_swk, w_swv = seg(7)[:, :kvw], seg(7)[:, kvw:]
    w = {
        "qa": cast(seg(0)), "ckv": cast(seg(1)),
        "krp": cast(pad_kr(w_kr)), "krr": cast(pad_kr(_rot_cols(w_kr, MLA_ROPE // 2))),
        "z": cast(seg(3)), "xbc": cast(seg(4)), "dt": cast(seg(5)),
        "swq": cast(seg(6)), "swq_rot": cast(_rot_heads(seg(6), SWA_HEADS, SWA_HEAD_DIM)),
        "swk": cast(_dup_heads(w_swk, SWA_KV_HEADS, SWA_HEAD_DIM)),
        "swk_rot": cast(_dup_heads(_rot_heads(w_swk, SWA_KV_HEADS, SWA_HEAD_DIM), SWA_KV_HEADS, SWA_HEAD_DIM)),
        "swv": cast(w_swv),
        "hy": cast(seg(8)), "gate": cast(seg(9)),
    }
    kq = mla_w_uq.shape[0]
    dq = MLA_NOPE + MLA_ROPE
    uq = mla_w_uq.reshape(kq, MLA_HEADS, dq)
    padq = ((0, 0), (0, 0), (0, LANES - dq))
    w["uq"] = cast(jnp.pad(uq, padq).reshape(kq, MLA_HEADS * LANES))
    uq_rope = uq[:, :, MLA_NOPE:]
    uq_rot = jnp.concatenate([-uq_rope[:, :, MLA_ROPE // 2:], uq_rope[:, :, :MLA_ROPE // 2]], axis=2)
    uq_rot = jnp.pad(uq_rot, ((0, 0), (0, 0), (MLA_NOPE, LANES - dq)))
    w["uq_rot"] = cast(uq_rot.reshape(kq, MLA_HEADS * LANES))
    kk = mla_w_ukv.shape[0]
    ukv = mla_w_ukv.reshape(kk, MLA_HEADS, MLA_NOPE + MLA_V)
    w["uk"] = cast(jnp.pad(ukv[:, :, :MLA_NOPE], ((0, 0), (0, 0), (0, LANES - MLA_NOPE))).reshape(kk, MLA_HEADS * LANES))
    w["uv"] = cast(ukv[:, :, MLA_NOPE:].reshape(kk, MLA_HEADS * MLA_V))
    return w


def _token_mixers(x, xc, mod_l, mod_c, ctx_out, rope_mla, rope_swa, norm_g0, w, p, dft_l, dft_c):
    bsz, t, dm = x.shape
    s_c = xc.shape[1]
    bf = MXU_DTYPE

    lat_names = ["qa", "ckv", "krp", "krr", "z", "xbc", "dt", "swq", "swq_rot", "swk", "swk_rot", "swv", "hy", "gate"]
    lat_dt = [bf, bf, bf, bf, bf, bf, F32, bf, bf, bf, bf, bf, bf, bf]
    lat = dict(zip(lat_names, _norm_matmul(x, norm_g0, [w[k] for k in lat_names], lat_dt,
                                           shift=mod_l[0], scale=mod_l[1], name="in_proj_latent")))
    ctx_names = ["ckv", "krp", "xbc", "dt", "swk", "swv"] + (["qa", "z", "swq", "hy", "gate"] if ctx_out else [])
    ctx_dt = [F32 if k == "dt" else bf for k in ctx_names]
    ctx = dict(zip(ctx_names, _norm_matmul(xc, norm_g0, [w[k] for k in ctx_names], ctx_dt,
                                           shift=mod_c[0], scale=mod_c[1], name="in_proj_context")))

    knp_l, v_l = _norm_matmul(lat["ckv"], p["mla_kv_norm"], [w["uk"], w["uv"]], [bf, bf], name="mla_kv_up")
    knp_c, v_c = _norm_matmul(ctx["ckv"], p["mla_kv_norm"], [w["uk"], w["uv"]], [bf, bf], name="mla_kv_up_ctx")
    qp, qr = _norm_matmul(lat["qa"], p["mla_q_norm"], [w["uq"], w["uq_rot"]], [bf, bf], name="mla_q_up")
    cq, sq, ck = _mla_tables(t, rope_mla)
    mla_q_scale = MLA_SCALE * LOG2E
    swa_q_scale = SWA_SCALE * LOG2E
    mla_l = _mla_attention(qp, qr, cq * mla_q_scale, sq * mla_q_scale, knp_l, lat["krp"], lat["krr"], ck, sq, v_l,
                           ctx=(knp_c, ctx["krp"], v_c))

    xbc = _dwconv([lat["xbc"], ctx["xbc"]], p["ssd_conv_w"], p["ssd_conv_b"], act=True, tc=384, name="ssd_conv")
    dt_raw = jnp.concatenate([lat["dt"], ctx["dt"]], axis=1)
    yf, yb = _ssd_scan(xbc, dt_raw, jnp.swapaxes(dt_raw, 1, 2), p["ssd_dt_bias"], p["ssd_a_log"], s_c // SSD_CHUNK)
    ssd_l = _ssd_output(yf, yb, xbc, lat["z"], p["ssd_d"], p["ssd_norm"], 0, tm=1024)

    csw, ssw = _swa_tables(t, rope_swa)
    swa_l = _swa_attention(p["swa_sink"], lat["swq"], lat["swq_rot"], csw * swa_q_scale, ssw * swa_q_scale,
                           ctx["swk"], ctx["swv"], lat=(lat["swk"], lat["swk_rot"], csw, ssw, lat["swv"]))

    hy_args = (p["hy_w1"], p["hy_b1"], p["hy_w2"], p["hy_b2"], p["hy_w3"], p["hy_freq"])
    spec_l = _hyena_filters(t, *hy_args, dft_l[0], dft_l[1])
    hy_l = _hyena(lat["hy"], p["hy_conv_w"], p["hy_conv_b"], spec_l, p["hy_d"], dft_l)

    w_branch = p["w_branch"].astype(bf)
    w_out = p["w_out"].astype(bf)
    x_new = _merge([mla_l, ssd_l, swa_l, hy_l], lat["gate"], w_branch, w_out, x, mod_l[2], p["norm_g1"])
    if not ctx_out:
        return x_new, None

    qp_c, = _norm_matmul(ctx["qa"], p["mla_q_norm"], [w["uq"]], [bf], name="mla_q_up_ctx")
    cq_c, sq_c, ck_c = _mla_tables(s_c, None)
    mla_c = _mla_attention(qp_c, qp_c, cq_c * mla_q_scale, sq_c, knp_c, ctx["krp"], ctx["krp"], ck_c, sq_c, v_c)
    ssd_c = _ssd_output(yf, yb, xbc, ctx["z"], p["ssd_d"], p["ssd_norm"], t)
    c1, s0 = _swa_tables(s_c, None)
    swa_c = _swa_attention(p["swa_sink"], ctx["swq"], ctx["swq"], c1 * swa_q_scale, s0, ctx["swk"], ctx["swv"])
    spec_c = _hyena_filters(s_c, *hy_args, dft_c[0], dft_c[1])
    hy_c = _hyena(ctx["hy"], p["hy_conv_w"], p["hy_conv_b"], spec_c, p["hy_d"], dft_c)
    xc_new = _merge([mla_c, ssd_c, swa_c, hy_c], ctx["gate"], w_branch, w_out, xc, mod_c[2], p["norm_g1"])
    return x_new, xc_new


def kernel(x, c, ctx, c_ctx, ada_w, ada_b, norm_g, w_in, mla_q_norm, mla_w_uq, mla_kv_norm, mla_w_ukv, ssd_conv_w, ssd_conv_b, ssd_dt_bias, ssd_a_log, ssd_d, ssd_norm, swa_sink, hy_conv_w, hy_conv_b, hy_w1, hy_b1, hy_w2, hy_b2, hy_w3, hy_freq, hy_d, w_branch, w_out, ffn_w_in, ffn_w_out):
    bsz, t, dm = x.shape
    s_c = ctx.shape[1]
    depth = ada_w.shape[0]
    rows = t // GRID_W
    rope_mla = _rope_tables(rows, MLA_ROPE)
    rope_swa = _rope_tables(rows, SWA_HEAD_DIM)
    dft_l = _dft_tables(t)
    dft_c = _dft_tables(s_c)
    cond_rows = 16
    cond = jnp.concatenate([c, c_ctx[None, :], jnp.zeros((cond_rows - bsz - 1, dm), F32)], axis=0)
    ffn_hidden = ffn_w_out.shape[1]
    xc = ctx
    for l in range(depth):
        ctx_out = l < depth - 1
        mod = _ada(cond, ada_w[l], ada_b[l])
        mod_l = [mod[:bsz, k * dm:(k + 1) * dm].reshape(bsz, 1, dm) for k in range(6)]
        mod_c = [jnp.broadcast_to(mod[bsz, k * dm:(k + 1) * dm].reshape(1, 1, dm), (bsz, 1, dm)) for k in range(6)]
        w = _layer_weights(w_in[l], mla_w_uq[l], mla_w_ukv[l])
        p = {"mla_q_norm": mla_q_norm[l], "mla_kv_norm": mla_kv_norm[l], "ssd_conv_w": ssd_conv_w[l],
             "ssd_conv_b": ssd_conv_b[l], "ssd_dt_bias": ssd_dt_bias[l], "ssd_a_log": ssd_a_log[l],
             "ssd_d": ssd_d[l], "ssd_norm": ssd_norm[l], "swa_sink": swa_sink[l], "hy_conv_w": hy_conv_w[l],
             "hy_conv_b": hy_conv_b[l], "hy_w1": hy_w1[l], "hy_b1": hy_b1[l], "hy_w2": hy_w2[l],
             "hy_b2": hy_b2[l], "hy_w3": hy_w3[l], "hy_freq": hy_freq[l], "hy_d": hy_d[l],
             "w_branch": w_branch[l], "w_out": w_out[l], "norm_g1": norm_g[l, 1]}
        x, xc_new = _token_mixers(x, xc, mod_l, mod_c, ctx_out, rope_mla, rope_swa, norm_g[l, 0], w, p, dft_l, dft_c)
        wg = ffn_w_in[l][:, :ffn_hidden].astype(MXU_DTYPE)
        wu = ffn_w_in[l][:, ffn_hidden:].astype(MXU_DTYPE)
        wo = ffn_w_out[l].astype(MXU_DTYPE)
        x = _ffn(x, norm_g[l, 2], mod_l[3], mod_l[4], wg, wu, wo, norm_g[l, 3], mod_l[5])
        if ctx_out:
            xc = _ffn(xc_new, norm_g[l, 2], mod_c[3], mod_c[4], wg, wu, wo, norm_g[l, 3], mod_c[5])
    return x
```

```python
import functools
import math

import jax
import jax.numpy as jnp
from jax import lax
from jax.experimental import pallas as pl
from jax.experimental.pallas import tpu as pltpu

F32 = jnp.float32
MXU_DTYPE = jnp.bfloat16
HIGHEST = lax.Precision.HIGHEST

GRID_W = 64
EPS = 1e-6
ROPE_BASE = 10000.0
MLA_HEADS, MLA_NOPE, MLA_ROPE, MLA_V = 8, 64, 32, 64
MLA_SCALE = (MLA_NOPE + MLA_ROPE) ** -0.5
SSD_HEADS, SSD_HEADDIM, SSD_GROUPS, SSD_STATE, SSD_CHUNK = 8, 64, 2, 64, 128
SSD_INNER = SSD_HEADS * SSD_HEADDIM
SWA_HEADS, SWA_KV_HEADS, SWA_HEAD_DIM, SWA_WINDOW, SWA_BLOCK = 8, 2, 64, 128, 128
SWA_SCALE = SWA_HEAD_DIM ** -0.5
HY_WIDTH, HY_ORDER, HY_BANDS, HY_HIDDEN = 512, 2, 16, 64
HY_DECAY_TARGET, HY_FAST_DECAY, HY_SLOW_DECAY = 1e-2, 0.3, 1.5
N_BRANCH = 4
LANES = 128
SSD_BATCH_ROWS = 8
VT_ROWS = 80
LOG2E = math.log2(math.e)
DFT_SPLIT = 8
CONV_EDGE = 16
MLA_KEY_BLOCK = 256

VMEM_LIMIT = 56 * 1024 * 1024
_WHOLE_VMEM = pl.BlockSpec(memory_space=pltpu.VMEM)


def _params(n_grid, vmem=VMEM_LIMIT):
    return pltpu.CompilerParams(dimension_semantics=("arbitrary",) * n_grid, vmem_limit_bytes=vmem)


def _silu(x):
    return x * jax.nn.sigmoid(x)


def _softplus(x):
    return jnp.maximum(x, 0.0) + jnp.log1p(jnp.exp(-jnp.abs(x)))


def _rms(x, g):
    return x * lax.rsqrt(jnp.mean(x * x, axis=-1, keepdims=True) + EPS) * g


def _ada_kernel(s_ref, w_ref, b_ref, o_ref):
    s = _silu(s_ref[...])
    o_ref[...] = jnp.dot(s, w_ref[...], preferred_element_type=F32, precision=HIGHEST) + b_ref[...]


def _ada(cond, w, b, tn=1536):
    m, k = cond.shape
    n = w.shape[1]
    return pl.pallas_call(
        _ada_kernel,
        grid=(n // tn,),
        in_specs=[pl.BlockSpec((m, k), lambda j: (0, 0)),
                  pl.BlockSpec((k, tn), lambda j: (0, j)),
                  pl.BlockSpec((1, tn), lambda j: (0, j))],
        out_specs=pl.BlockSpec((m, tn), lambda j: (0, j)),
        out_shape=jax.ShapeDtypeStruct((m, n), F32),
        compiler_params=_params(1),
        name="ada_ln",
    )(cond, w, b.reshape(1, n))


def _norm_mm_kernel(*refs, modulate):
    it = iter(refs)
    x_ref, g_ref = next(it), next(it)
    if modulate:
        sh_ref, sc_ref = next(it), next(it)
    w_ref = next(it)
    o_refs = list(it)
    a = _rms(x_ref[...].astype(F32), g_ref[...])
    if modulate:
        a = a * (1.0 + sc_ref[...]) + sh_ref[...]
    a = a.astype(MXU_DTYPE)
    base = 0
    for o_ref in o_refs:
        n = o_ref.shape[1]
        for c0 in range(0, n, 1024):
            c1 = min(n, c0 + 1024)
            o_ref[:, c0:c1] = jnp.dot(a, w_ref[:, base + c0:base + c1],
                                      preferred_element_type=F32).astype(o_ref.dtype)
        base += n


def _norm_matmul(x, g, ws, out_dtypes, shift=None, scale=None, tm=512, name="norm_matmul"):
    bsz, t, k = x.shape
    tm = min(tm, t)
    modulate = shift is not None
    in_specs = [pl.BlockSpec((None, tm, k), lambda b, i: (b, i, 0)),
                pl.BlockSpec((1, k), lambda b, i: (0, 0))]
    args = [x, g.reshape(1, k).astype(F32)]
    if modulate:
        in_specs += [pl.BlockSpec((None, 1, k), lambda b, i: (b, 0, 0))] * 2
        args += [shift, scale]
    assert all(w.shape[1] % LANES == 0 for w in ws[:-1])
    in_specs += [_WHOLE_VMEM]
    args += [jnp.concatenate(ws, axis=1) if len(ws) > 1 else ws[0]]
    return pl.pallas_call(
        functools.partial(_norm_mm_kernel, modulate=modulate),
        grid=(bsz, t // tm),
        in_specs=in_specs,
        out_specs=[pl.BlockSpec((None, tm, w.shape[1]), lambda b, i: (b, i, 0)) for w in ws],
        out_shape=[jax.ShapeDtypeStruct((bsz, t, w.shape[1]), dt) for w, dt in zip(ws, out_dtypes)],
        compiler_params=_params(2),
        name=name,
    )(*args)


def _mla_kernel(*refs, t_l, s_c):
    if s_c:
        (qp, qr, cq, sq, knp_l, krp_l, krr_l, ck, sk, v_l, knp_c, krp_c, v_c, o_ref, kcat, vt) = refs
    else:
        (qp, qr, cq, sq, knp_l, krp_l, krr_l, ck, sk, v_l, o_ref, kcat, vt) = refs

    @pl.when(pl.program_id(1) == 0)
    def _():
        kro = (krp_l[...].astype(F32) * ck[...] + krr_l[...].astype(F32) * sk[...]).astype(kcat.dtype)
        for h in range(MLA_HEADS):
            blk = slice(h * LANES, (h + 1) * LANES)
            kcat[h, 0:t_l, :] = knp_l[:, blk] + kro
            if s_c:
                kcat[h, t_l:t_l + s_c, :] = knp_c[:, blk] + krp_c[...]
            vt[h, MLA_V:, :] = jnp.ones((VT_ROWS - MLA_V, t_l + s_c), vt.dtype)
        for j in range(MLA_HEADS // 2):
            blk = slice(j * LANES, (j + 1) * LANES)
            vp = v_l[:, blk].astype(F32).T
            vt[2 * j, 0:MLA_V, 0:t_l] = vp[0:MLA_V].astype(vt.dtype)
            vt[2 * j + 1, 0:MLA_V, 0:t_l] = vp[MLA_V:].astype(vt.dtype)
            if s_c:
                vp = v_c[:, blk].astype(F32).T
                vt[2 * j, 0:MLA_V, t_l:t_l + s_c] = vp[0:MLA_V].astype(vt.dtype)
                vt[2 * j + 1, 0:MLA_V, t_l:t_l + s_c] = vp[MLA_V:].astype(vt.dtype)

    cqv, sqv = cq[...], sq[...]
    tq = qp.shape[0]
    s_tot = t_l + s_c
    nt = (((1,), (1,)), ((), ()))
    heads = range(MLA_HEADS)
    q, m, acc = {}, {}, {}
    for h in heads:
        blk = slice(h * LANES, (h + 1) * LANES)
        q[h] = (qp[:, blk].astype(F32) * cqv + qr[:, blk].astype(F32) * sqv).astype(MXU_DTYPE)
        m[h] = jnp.full((1, tq), -1e30, F32)
        acc[h] = jnp.zeros((VT_ROWS, tq), F32)
    for k0 in range(0, s_tot, MLA_KEY_BLOCK):
        k1 = min(s_tot, k0 + MLA_KEY_BLOCK)
        s = {h: lax.dot_general(kcat[h, k0:k1, :], q[h], nt, preferred_element_type=F32) for h in heads}
        m_new = {h: jnp.maximum(m[h], jnp.max(s[h], axis=0, keepdims=True)) for h in heads}
        p = {h: jnp.exp2(s[h] - m_new[h]).astype(MXU_DTYPE) for h in heads}
        pv = {h: jnp.dot(vt[h, :, k0:k1], p[h], preferred_element_type=F32) for h in heads}
        for h in heads:
            acc[h] = acc[h] * jnp.exp2(m[h] - m_new[h]) + pv[h]
            m[h] = m_new[h]
    for j in range(MLA_HEADS // 2):
        halves = [acc[h][0:MLA_V] * (1.0 / acc[h][MLA_V:MLA_V + 1]) for h in (2 * j, 2 * j + 1)]
        o_ref[:, j * LANES:(j + 1) * LANES] = jnp.concatenate(halves, axis=0).T.astype(o_ref.dtype)


def _mla_attention(qp, qr, cq, sq, knp_l, krp_l, krr_l, ck, sk, v_l, ctx=None, tq=512):
    bsz, t_q, _ = qp.shape
    t_l = knp_l.shape[1]
    s_c = 0 if ctx is None else ctx[0].shape[1]
    tq = min(tq, t_q)
    hw = MLA_HEADS * LANES
    vw = MLA_HEADS * MLA_V
    per_b = lambda rows, cols: pl.BlockSpec((None, rows, cols), lambda b, i: (b, 0, 0))
    in_specs = [pl.BlockSpec((None, tq, hw), lambda b, i: (b, i, 0)),
                pl.BlockSpec((None, tq, hw), lambda b, i: (b, i, 0)),
                pl.BlockSpec((tq, LANES), lambda b, i: (i, 0)),
                pl.BlockSpec((tq, LANES), lambda b, i: (i, 0)),
                per_b(t_l, hw), per_b(t_l, LANES), per_b(t_l, LANES),
                pl.BlockSpec((t_l, LANES), lambda b, i: (0, 0)),
                pl.BlockSpec((t_l, LANES), lambda b, i: (0, 0)),
                per_b(t_l, vw)]
    args = [qp, qr, cq, sq, knp_l, krp_l, krr_l, ck, sk, v_l]
    if s_c:
        in_specs += [per_b(s_c, hw), per_b(s_c, LANES), per_b(s_c, vw)]
        args += list(ctx)
    return pl.pallas_call(
        functools.partial(_mla_kernel, t_l=t_l, s_c=s_c),
        grid=(bsz, t_q // tq),
        in_specs=in_specs,
        out_specs=pl.BlockSpec((None, tq, vw), lambda b, i: (b, i, 0)),
        out_shape=jax.ShapeDtypeStruct((bsz, t_q, vw), MXU_DTYPE),
        scratch_shapes=[pltpu.VMEM((MLA_HEADS, t_l + s_c, LANES), MXU_DTYPE),
                        pltpu.VMEM((MLA_HEADS, VT_ROWS, t_l + s_c), MXU_DTYPE)],
        compiler_params=_params(2),
        name="mla_attention",
    )(*args)


def _swa_kernel(*refs, band, t_k):
    if band:
        (sink, q, qrot, cq, sq, k, krot, ck, sk, v, kc, vc, o_ref, kro, vt, vct) = refs
    else:
        (sink, q, qrot, cq, sq, kc, vc, o_ref, vct) = refs
    i = pl.program_id(1)
    tq = q.shape[0]
    gw = 2 * SWA_HEAD_DIM
    dh = SWA_HEAD_DIM
    kb_rows = min(tq + 2 * SWA_BLOCK, t_k)

    @pl.when(i == 0)
    def _():
        def put_vt(dst, src):
            vtr = src[...].astype(F32).T
            for g in range(SWA_KV_HEADS):
                dst[g, 0:dh, :] = vtr[g * dh:(g + 1) * dh].astype(dst.dtype)
                dst[g, dh:, :] = jnp.ones((VT_ROWS - dh, src.shape[0]), dst.dtype)

        put_vt(vct, vc)
        if band:
            put_vt(vt, v)
            ckv, skv = ck[...], sk[...]
            for g in range(SWA_KV_HEADS):
                blk = slice(g * gw, (g + 1) * gw)
                kro[:, blk] = (k[:, blk].astype(F32) * ckv + krot[:, blk].astype(F32) * skv).astype(kro.dtype)

    if band:
        start = pl.multiple_of(jnp.clip(i * tq - SWA_BLOCK, 0, t_k - kb_rows), SWA_BLOCK)
        k_pos = start + lax.broadcasted_iota(jnp.int32, (kb_rows, tq), 0)
        q_pos = i * tq + lax.broadcasted_iota(jnp.int32, (kb_rows, tq), 1)
        in_band = jnp.abs(q_pos - k_pos) <= SWA_WINDOW

    lane_lo = lax.broadcasted_iota(jnp.int32, (tq, LANES), 1) < dh
    cqv, sqv = cq[...], sq[...]
    nt = (((1,), (1,)), ((), ()))
    heads = range(SWA_HEADS)
    group = {h: h // (SWA_HEADS // SWA_KV_HEADS) for h in heads}
    gblk = {h: slice(group[h] * gw, (group[h] + 1) * gw) for h in heads}
    qm, snk = {}, {}
    for j in range(SWA_HEADS // 2):
        blk = slice(j * LANES, (j + 1) * LANES)
        qro = q[:, blk].astype(F32) * cqv + qrot[:, blk].astype(F32) * sqv
        qm[2 * j] = jnp.where(lane_lo, qro, 0.0).astype(MXU_DTYPE)
        qm[2 * j + 1] = jnp.where(lane_lo, 0.0, qro).astype(MXU_DTYPE)
    for h in heads:
        snk[h] = sink[h] * LOG2E
    s_c = {h: lax.dot_general(kc[:, gblk[h]], qm[h], nt, preferred_element_type=F32) for h in heads}
    m = {h: jnp.maximum(jnp.max(s_c[h], axis=0, keepdims=True), snk[h]) for h in heads}
    if band:
        s_b = {h: jnp.where(in_band, lax.dot_general(kro[pl.ds(start, kb_rows), gblk[h]], qm[h], nt,
                                                     preferred_element_type=F32), -1e30) for h in heads}
        m = {h: jnp.maximum(m[h], jnp.max(s_b[h], axis=0, keepdims=True)) for h in heads}
    o = {h: jnp.dot(vct[group[h]], jnp.exp2(s_c[h] - m[h]).astype(MXU_DTYPE), preferred_element_type=F32)
         for h in heads}
    if band:
        o = {h: o[h] + jnp.dot(vt[group[h], :, pl.ds(start, kb_rows)], jnp.exp2(s_b[h] - m[h]).astype(MXU_DTYPE),
                               preferred_element_type=F32) for h in heads}
    for j in range(SWA_HEADS // 2):
        halves = []
        for h in (2 * j, 2 * j + 1):
            l = o[h][dh:dh + 1] + jnp.exp2(snk[h] - m[h])
            halves.append(o[h][0:dh] * (1.0 / l))
        o_ref[:, j * LANES:(j + 1) * LANES] = jnp.concatenate(halves, axis=0).T.astype(o_ref.dtype)


def _swa_attention(sink, q, qrot, cq, sq, kc, vc, lat=None):
    bsz, t, hw = q.shape
    s_c = kc.shape[1]
    tq = min(t, 2 * SWA_BLOCK)
    kw, vw = kc.shape[2], vc.shape[2]
    band = lat is not None
    per_b = lambda rows, cols: pl.BlockSpec((None, rows, cols), lambda b, i: (b, 0, 0))
    qspec = pl.BlockSpec((None, tq, hw), lambda b, i: (b, i, 0))
    tspec = pl.BlockSpec((tq, LANES), lambda b, i: (i, 0))
    in_specs = [pl.BlockSpec(memory_space=pltpu.SMEM), qspec, qspec, tspec, tspec]
    args = [sink.astype(F32), q, qrot, cq, sq]
    scratch = []
    if band:
        k, krot, ck, sk, v = lat
        full_t = pl.BlockSpec((t, LANES), lambda b, i: (0, 0))
        in_specs += [per_b(t, kw), per_b(t, kw), full_t, full_t, per_b(t, vw)]
        args += [k, krot, ck, sk, v]
        scratch = [pltpu.VMEM((t, kw), MXU_DTYPE), pltpu.VMEM((SWA_KV_HEADS, VT_ROWS, t), MXU_DTYPE)]
    scratch += [pltpu.VMEM((SWA_KV_HEADS, VT_ROWS, s_c), MXU_DTYPE)]
    in_specs += [per_b(s_c, kw), per_b(s_c, vw)]
    args += [kc, vc]
    return pl.pallas_call(
        functools.partial(_swa_kernel, band=band, t_k=t),
        grid=(bsz, t // tq),
        in_specs=in_specs,
        out_specs=pl.BlockSpec((None, tq, hw), lambda b, i: (b, i, 0)),
        out_shape=jax.ShapeDtypeStruct((bsz, t, hw), MXU_DTYPE),
        scratch_shapes=scratch,
        compiler_params=_params(2),
        name="swa_attention" if band else "ctx_sink_attention",
    )(*args)


def _dwconv_kernel(*refs, n_seg, act):
    x_refs = refs[:n_seg]
    w_ref, b_ref, o_ref = refs[n_seg:]
    taps = w_ref.shape[0]
    off = 0
    for x_ref in x_refs:
        t = x_ref.shape[0]
        x = x_ref[...].astype(F32)
        e = CONV_EDGE
        row = lax.broadcasted_iota(jnp.int32, (e, x.shape[1]), 0)
        acc = jnp.zeros_like(x) + b_ref[...]
        top = jnp.zeros((e, x.shape[1]), F32) + b_ref[...]
        bot = top
        for kk in range(taps):
            d = kk - taps // 2
            wk = w_ref[kk:kk + 1, :]
            xs = x if d == 0 else pltpu.roll(x, (-d) % t, axis=0)
            acc = acc + xs * wk
            top = top + jnp.where(row + d >= 0, xs[0:e], 0.0) * wk
            bot = bot + jnp.where(row + d < e, xs[t - e:t], 0.0) * wk
        if act:
            acc, top, bot = _silu(acc), _silu(top), _silu(bot)
        o_ref[off:off + t, :] = acc.astype(o_ref.dtype)
        o_ref[off:off + e, :] = top.astype(o_ref.dtype)
        o_ref[off + t - e:off + t, :] = bot.astype(o_ref.dtype)
        off += t


def _dwconv(xs, w, b, act, tc=256, name="dwconv"):
    bsz, _, c = xs[0].shape
    t_tot = sum(x.shape[1] for x in xs)
    taps = w.shape[0]
    in_specs = [pl.BlockSpec((None, x.shape[1], tc), lambda bb, j: (bb, 0, j)) for x in xs]
    in_specs += [pl.BlockSpec((taps, tc), lambda bb, j: (0, j)), pl.BlockSpec((1, tc), lambda bb, j: (0, j))]
    return pl.pallas_call(
        functools.partial(_dwconv_kernel, n_seg=len(xs), act=act),
        grid=(bsz, c // tc),
        in_specs=in_specs,
        out_specs=pl.BlockSpec((None, t_tot, tc), lambda bb, j: (bb, 0, j)),
        out_shape=jax.ShapeDtypeStruct((bsz, t_tot, c), MXU_DTYPE),
        compiler_params=_params(2),
        name=name,
    )(*xs, w.astype(F32), b.reshape(1, c).astype(F32))


def _ssd_scan_kernel(xs_f, bm_f, cm_f, dt_f, dtt_f, xs_b, bm_b, cm_b, dt_b, dtt_b,
                     b_row, al_row, b_col, al_col, yf_ref, yb_ref, st_ref):
    @pl.when(pl.program_id(1) == 0)
    def _():
        st_ref[...] = jnp.zeros_like(st_ref)

    q = SSD_CHUNK
    row = lax.broadcasted_iota(jnp.int32, (q, q), 0)
    col = lax.broadcasted_iota(jnp.int32, (q, q), 1)
    lane_lo = col < SSD_HEADDIM
    a_row = -jnp.exp(al_row[...])
    a_col = -jnp.exp(al_col[...])
    nt = (((1,), (1,)), ((), ()))
    tn = (((0,), (0,)), ((), ()))
    dirs = ((xs_f, bm_f, cm_f, dt_f, dtt_f, yf_ref), (xs_b, bm_b, cm_b, dt_b, dtt_b, yb_ref))
    units = [(bi, d) for bi in range(st_ref.shape[0]) for d in range(2)]
    pairs_per_group = SSD_HEADS // SSD_GROUPS // 2
    feeds = {0: row >= col, 1: row <= col}
    last = {0: q - 1, 1: 0}
    dt, dtt, cs, cst, cg, bg, gmat, sc, ecol, wcol, dec = ({} for _ in range(11))
    for u in units:
        bi, d = u
        dt_ref, dtt_ref = dirs[d][3], dirs[d][4]
        dt[u] = _softplus(dt_ref[bi] + b_row[...])
        dtt[u] = _softplus(dtt_ref[bi] + b_col[...])
    for u in units:
        d = u[1]
        cs[u] = jnp.dot(feeds[d].astype(F32), dt[u] * a_row, preferred_element_type=F32, precision=HIGHEST)
        cst[u] = jnp.dot(dtt[u] * a_col, feeds[1 - d].astype(F32), preferred_element_type=F32, precision=HIGHEST)
    for u in units:
        bi, d = u
        bblk, cblk = dirs[d][1][bi].astype(F32), dirs[d][2][bi].astype(F32)
        for g in range(SSD_GROUPS):
            gmask = (col // SSD_STATE) == g
            cg[u, g] = jnp.where(gmask, cblk, 0.0).astype(MXU_DTYPE)
            bg[u, g] = jnp.where(gmask, bblk, 0.0)
            gmat[u, g] = lax.dot_general(cg[u, g], bg[u, g].astype(MXU_DTYPE), nt,
                                         preferred_element_type=F32)
    for u in units:
        d = u[1]
        tot = cs[u][last[d]:last[d] + 1, :]
        e_all = jnp.exp(cs[u])
        w_all = dt[u] * jnp.exp(tot - cs[u])
        d_all = jnp.exp(tot)
        for h in range(SSD_HEADS):
            c = d * SSD_HEADS + h
            ccol = cs[u][:, c:c + 1]
            crow = cst[u][c:c + 1, :]
            lmat = jnp.where(feeds[d], jnp.exp(ccol - crow), 0.0)
            sc[u, h] = (gmat[u, h // (SSD_HEADS // SSD_GROUPS)] * lmat * dtt[u][c:c + 1, :]).astype(MXU_DTYPE)
            ecol[u, h] = e_all[:, c:c + 1]
            wcol[u, h] = w_all[:, c:c + 1]
            dec[u, h] = d_all[:, c:c + 1]
    for u in units:
        bi, d = u
        xs_ref, y_ref = dirs[d][0], dirs[d][5]
        for j in range(SSD_HEADS // 2):
            g = j // pairs_per_group
            xb = xs_ref[bi, :, j * LANES:(j + 1) * LANES]
            xblk = xb.astype(F32)
            s_in = st_ref[bi, d, j]
            y_inter = jnp.dot(cg[u, g], s_in.astype(MXU_DTYPE), preferred_element_type=F32)
            ys, news = [], []
            for hh in range(2):
                h = 2 * j + hh
                ys.append(jnp.dot(sc[u, h], xb, preferred_element_type=F32) + ecol[u, h] * y_inter)
                bw = (bg[u, g] * wcol[u, h]).astype(MXU_DTYPE)
                xh = jnp.where(lane_lo if hh == 0 else ~lane_lo, xblk, 0.0).astype(MXU_DTYPE)
                news.append(lax.dot_general(bw, xh, tn, preferred_element_type=F32))
            y_ref[bi, :, j * LANES:(j + 1) * LANES] = jnp.where(lane_lo, ys[0], ys[1]).astype(y_ref.dtype)
            st_ref[bi, d, j] = (s_in * jnp.where(lane_lo, dec[u, 2 * j], dec[u, 2 * j + 1])
                                + news[0] + news[1])


def _ssd_scan(xbc, dt_raw, dt_raw_t, dt_bias, a_log, nc_ctx):
    bsz, t_c, _ = xbc.shape
    q = SSD_CHUNK
    nc = t_c // q
    nh2 = 2 * SSD_HEADS

    def fwd(s):
        return jnp.where(s < nc_ctx, nc - nc_ctx + s, s - nc_ctx)

    def bwd(s):
        return nc - 1 - s

    bt = math.gcd(bsz, SSD_BATCH_ROWS)

    def specs(order):
        return [pl.BlockSpec((bt, q, SSD_INNER), lambda b, s: (b, order(s), 0)),
                pl.BlockSpec((bt, q, LANES), lambda b, s: (b, order(s), SSD_INNER // LANES)),
                pl.BlockSpec((bt, q, LANES), lambda b, s: (b, order(s), SSD_INNER // LANES + 1)),
                pl.BlockSpec((bt, q, nh2), lambda b, s: (b, order(s), 0)),
                pl.BlockSpec((bt, nh2, q), lambda b, s: (b, 0, order(s)))]

    small = lambda r, c: pl.BlockSpec((r, c), lambda b, s: (0, 0))
    bias = dt_bias.astype(F32).reshape(1, nh2)
    alog = a_log.astype(F32).reshape(1, nh2)
    return pl.pallas_call(
        _ssd_scan_kernel,
        grid=(bsz // bt, nc),
        in_specs=specs(fwd) + specs(bwd) + [small(1, nh2), small(1, nh2), small(nh2, 1), small(nh2, 1)],
        out_specs=[pl.BlockSpec((bt, q, SSD_INNER), lambda b, s: (b, fwd(s), 0)),
                   pl.BlockSpec((bt, q, SSD_INNER), lambda b, s: (b, bwd(s), 0))],
        out_shape=[jax.ShapeDtypeStruct((bsz, t_c, SSD_INNER), MXU_DTYPE)] * 2,
        scratch_shapes=[pltpu.VMEM((bt, 2, SSD_HEADS // 2, LANES, LANES), F32)],
        compiler_params=_params(2),
        name="ssd_scan",
    )(xbc, xbc, xbc, dt_raw, dt_raw_t, xbc, xbc, xbc, dt_raw, dt_raw_t,
      bias, alog, bias.reshape(nh2, 1), alog.reshape(nh2, 1))


def _ssd_out_kernel(yf, yb, xs, z, dexp, ng, o_ref):
    y = yf[...].astype(F32) + yb[...].astype(F32) + dexp[...] * xs[...].astype(F32)
    yg = y * _silu(z[...].astype(F32))
    gw = SSD_INNER // SSD_GROUPS
    for g in range(SSD_GROUPS):
        blk = slice(g * gw, (g + 1) * gw)
        o_ref[:, blk] = _rms(yg[:, blk], ng[:, blk]).astype(o_ref.dtype)


def _ssd_output(yf, yb, xbc, z, d_skip, norm_g, row0, tm=256):
    bsz, t, _ = z.shape
    tm = min(tm, t)
    off = row0 // tm
    sp_y = pl.BlockSpec((None, tm, SSD_INNER), lambda b, i: (b, i + off, 0))
    vec = pl.BlockSpec((1, SSD_INNER), lambda b, i: (0, 0))
    dexp = jnp.repeat(d_skip.astype(F32), SSD_HEADDIM).reshape(1, SSD_INNER)
    return pl.pallas_call(
        _ssd_out_kernel,
        grid=(bsz, t // tm),
        in_specs=[sp_y, sp_y, sp_y, pl.BlockSpec((None, tm, SSD_INNER), lambda b, i: (b, i, 0)), vec, vec],
        out_specs=pl.BlockSpec((None, tm, SSD_INNER), lambda b, i: (b, i, 0)),
        out_shape=jax.ShapeDtypeStruct((bsz, t, SSD_INNER), MXU_DTYPE),
        compiler_params=_params(2),
        name="ssd_output",
    )(yf, yb, xbc, z, dexp, norm_g.astype(F32).reshape(1, SSD_INNER))


def _hy_mlp_kernel(feat_ref, w1, b1, w2, b2, w3, fr, delta, h_ref, cs_ref):
    feat = feat_ref[...]
    h = jnp.sin(fr[...] * (jnp.dot(feat, w1[...], preferred_element_type=F32, precision=HIGHEST) + b1[...]))
    h = jnp.sin(fr[...] * (jnp.dot(h, w2[...], preferred_element_type=F32, precision=HIGHEST) + b2[...]))
    h = jnp.dot(h, w3[...], preferred_element_type=F32, precision=HIGHEST)
    h = h * jnp.exp(-feat[:, 0:1] * delta[...])
    h_ref[...] = h
    s = jnp.sum(jnp.abs(h), axis=0, keepdims=True)

    @pl.when(pl.program_id(0) == 0)
    def _():
        cs_ref[...] = s

    @pl.when(pl.program_id(0) != 0)
    def _():
        cs_ref[...] += s


def _hy_spectrum_kernel(h0_ref, h1_ref, c0_ref, c1_ref, cm_ref, sf_ref, ar_ref, ai_ref, br_ref):
    n = h0_ref.shape[0]
    inv = 1.0 / (c0_ref[...] + c1_ref[...])
    row = lax.broadcasted_iota(jnp.int32, h0_ref.shape, 0)
    first = row == 0
    h0 = h0_ref[...] * inv
    h1 = jnp.where(first, 0.0, h1_ref[...] * inv)
    a = h0 + h1
    kr = jnp.dot(cm_ref[...], a.astype(MXU_DTYPE), preferred_element_type=F32)
    kq = jnp.dot(sf_ref[...], (h1 - h0).astype(MXU_DTYPE), preferred_element_type=F32)
    k_nyq = jnp.sum(jnp.where(row % 2 == 0, a, -a), axis=0, keepdims=True)
    inv_n = 1.0 / (2 * n)
    ar = kr * jnp.where(first, inv_n, 2.0 * inv_n)
    ar_ref[...] = ar
    ai_ref[...] = jnp.where(first, 0.0, kq * (2.0 * inv_n))
    br_ref[...] = jnp.where(first, k_nyq * inv_n, ar)


def _hyena_filters(n, w1, b1, w2, b2, w3, freq, cm, sf):
    t = jnp.arange(n, dtype=F32)
    tnorm = t / n
    bands = jnp.linspace(1e-4, HY_BANDS - 1, HY_BANDS, dtype=F32)
    ang = 2 * math.pi * t[:, None] * bands[None, :] / n
    feat = jnp.concatenate([tnorm[:, None], jnp.cos(ang), -jnp.sin(ang)], axis=-1)
    emb = feat.shape[1]
    feat = jnp.pad(feat, ((0, 0), (0, HY_HIDDEN - emb)))
    w1p = jnp.pad(w1.astype(F32), ((0, HY_HIDDEN - emb), (0, 0)))
    deltas = jnp.abs(jnp.linspace(math.log(HY_DECAY_TARGET) / HY_SLOW_DECAY,
                                  math.log(HY_DECAY_TARGET) / HY_FAST_DECAY, HY_WIDTH, dtype=F32))
    ncol = HY_ORDER * 2 * HY_WIDTH
    delta_row = jnp.tile(deltas, HY_ORDER * 2).reshape(1, ncol)
    tt = min(n, 256)
    small = lambda r, c: pl.BlockSpec((r, c), lambda i: (0, 0))
    row = lambda v: v.astype(F32).reshape(1, -1)
    h, colsum = pl.pallas_call(
        _hy_mlp_kernel,
        grid=(n // tt,),
        in_specs=[pl.BlockSpec((tt, HY_HIDDEN), lambda i: (i, 0)),
                  small(HY_HIDDEN, HY_HIDDEN), small(1, HY_HIDDEN),
                  small(HY_HIDDEN, HY_HIDDEN), small(1, HY_HIDDEN),
                  small(HY_HIDDEN, ncol), small(1, HY_HIDDEN), small(1, ncol)],
        out_specs=[pl.BlockSpec((tt, ncol), lambda i: (i, 0)), small(1, ncol)],
        out_shape=[jax.ShapeDtypeStruct((n, ncol), F32), jax.ShapeDtypeStruct((1, ncol), F32)],
        compiler_params=_params(1),
        name="hyena_filter_mlp",
    )(feat, w1p, row(b1), w2.astype(F32), row(b2), w3.astype(F32), row(freq), delta_row)

    tc = 256
    per_o = HY_WIDTH // tc
    side0 = lambda jc: (jc // per_o) * 2 * per_o + jc % per_o
    side1 = lambda jc: (jc // per_o) * 2 * per_o + per_o + jc % per_o
    nout = HY_ORDER * HY_WIDTH
    out_spec = pl.BlockSpec((n, tc), lambda jc: (0, jc))
    return pl.pallas_call(
        _hy_spectrum_kernel,
        grid=(nout // tc,),
        in_specs=[pl.BlockSpec((n, tc), lambda jc: (0, side0(jc))),
                  pl.BlockSpec((n, tc), lambda jc: (0, side1(jc))),
                  pl.BlockSpec((1, tc), lambda jc: (0, side0(jc))),
                  pl.BlockSpec((1, tc), lambda jc: (0, side1(jc))),
                  _WHOLE_VMEM, _WHOLE_VMEM],
        out_specs=[out_spec] * 3,
        out_shape=[jax.ShapeDtypeStruct((n, nout), F32)] * 3,
        compiler_params=_params(1),
        name="hyena_filter_spectrum",
    )(h, h, colsum, colsum, cm, sf)


def _hy_fwd_kernel(u_ref, cm_ref, sf_ref, ar_ref, ai_ref, br_ref, yr_ref, yi_ref):
    u = u_ref[...].astype(MXU_DTYPE)
    p = jnp.dot(cm_ref[...], u, preferred_element_type=F32)
    q = jnp.dot(sf_ref[...], u, preferred_element_type=F32)
    ai = ai_ref[...]
    yr_ref[...] = (p * ar_ref[...] + q * ai).astype(yr_ref.dtype)
    yi_ref[...] = (q * br_ref[...] - p * ai).astype(yi_ref.dtype)


def _hy_inv_kernel(yr_ref, yi_ref, cm_ref, si_ref, u_ref, xg_ref, d_ref, o_ref):
    y = (jnp.dot(cm_ref[...], yr_ref[...], preferred_element_type=F32)
         + jnp.dot(si_ref[...], yi_ref[...], preferred_element_type=F32))
    u = u_ref[...].astype(F32)
    o_ref[...] = (xg_ref[...].astype(F32) * (y + u * d_ref[...])).astype(o_ref.dtype)


def _hyena_conv(u, u_col0, xg, xg_col0, spectra, order, d, tables, out_dtype, tc=256):
    cm, sf, si = tables
    ar, ai, br = spectra
    bsz, n, _ = u.shape
    nct = HY_WIDTH // tc
    ucol, gcol, scol = u_col0 // tc, xg_col0 // tc, order * nct
    tok = lambda c0: pl.BlockSpec((None, n, tc), lambda c, b: (b, 0, c0 + c))
    spec_sp = pl.BlockSpec((n, tc), lambda c, b: (0, scol + c))
    mid = pl.BlockSpec((None, n, tc), lambda c, b: (b, 0, c))
    yr, yi = pl.pallas_call(
        _hy_fwd_kernel,
        grid=(nct, bsz),
        in_specs=[tok(ucol), _WHOLE_VMEM, _WHOLE_VMEM, spec_sp, spec_sp, spec_sp],
        out_specs=[mid, mid],
        out_shape=[jax.ShapeDtypeStruct((bsz, n, HY_WIDTH), MXU_DTYPE)] * 2,
        compiler_params=_params(2),
        name="hyena_dft_forward",
    )(u, cm, sf, ar, ai, br)
    return pl.pallas_call(
        _hy_inv_kernel,
        grid=(nct, bsz),
        in_specs=[mid, mid, _WHOLE_VMEM, _WHOLE_VMEM, tok(ucol), tok(gcol),
                  pl.BlockSpec((1, tc), lambda c, b: (0, c))],
        out_specs=mid,
        out_shape=jax.ShapeDtypeStruct((bsz, n, HY_WIDTH), out_dtype),
        compiler_params=_params(2),
        name="hyena_dft_inverse",
    )(yr, yi, cm, si, u, xg, d.astype(F32).reshape(1, HY_WIDTH))


def _dft_tables(n):
    idx = jnp.arange(n, dtype=jnp.int32)
    alt = jnp.where(idx % 2 == 0, 1.0, -1.0).astype(F32)
    def rows(freqs):
        ang = ((freqs[:, None] * idx[None, :]) % (2 * n)).astype(F32) * (math.pi / n)
        return jnp.cos(ang), jnp.sin(ang)
    c_hi, s_hi = rows(idx[::DFT_SPLIT])
    c_lo, s_lo = rows(idx[:DFT_SPLIT])
    cm = (c_hi[:, None, :] * c_lo[None, :, :] - s_hi[:, None, :] * s_lo[None, :, :]).reshape(n, n)
    sm = (s_hi[:, None, :] * c_lo[None, :, :] + c_hi[:, None, :] * s_lo[None, :, :]).reshape(n, n)
    sf = jnp.where(idx[:, None] == 0, alt[None, :], sm)
    si = jnp.where(idx[None, :] == 0, alt[:, None], sm)
    return cm.astype(MXU_DTYPE), sf.astype(MXU_DTYPE), si.astype(MXU_DTYPE)


def _hyena(hy, conv_w, conv_b, spectra, d, tables):
    u3 = _dwconv([hy], conv_w, conv_b, act=False, tc=512, name="hyena_short_conv")
    z = _hyena_conv(u3, 0, u3, HY_WIDTH, spectra, 0, d[0], tables, MXU_DTYPE)
    return _hyena_conv(z, 0, u3, 2 * HY_WIDTH, spectra, 1, d[1], tables, MXU_DTYPE)


def _merge_kernel(b0, b1, b2, b3, gate_ref, wb_ref, wo_ref, x_ref, mg_ref, g_ref, o_ref):
    d = x_ref.shape[1]
    acc = None
    for i, br in enumerate((b0, b1, b2, b3)):
        proj = jnp.dot(br[...], wb_ref[i], preferred_element_type=F32)
        term = jax.nn.sigmoid(gate_ref[:, i * d:(i + 1) * d].astype(F32)) * proj
        acc = term if acc is None else acc + term
    y = jnp.dot(acc.astype(MXU_DTYPE), wo_ref[...], preferred_element_type=F32)
    o_ref[...] = x_ref[...] + mg_ref[...] * _rms(y, g_ref[...])


def _merge(branches, gate, w_branch, w_out, x, mod_gate, g, tm=512):
    bsz, t, d = x.shape
    tm = min(tm, t)
    bw = branches[0].shape[2]
    tok = lambda w: pl.BlockSpec((None, tm, w), lambda b, i: (b, i, 0))
    return pl.pallas_call(
        _merge_kernel,
        grid=(bsz, t // tm),
        in_specs=[tok(bw)] * N_BRANCH + [tok(N_BRANCH * d), _WHOLE_VMEM, _WHOLE_VMEM, tok(d),
                                         pl.BlockSpec((None, 1, d), lambda b, i: (b, 0, 0)),
                                         pl.BlockSpec((1, d), lambda b, i: (0, 0))],
        out_specs=tok(d),
        out_shape=jax.ShapeDtypeStruct((bsz, t, d), F32),
        compiler_params=_params(2),
        name="merge_branches",
    )(*branches, gate, w_branch, w_out, x, mod_gate, g.astype(F32).reshape(1, d))


def _ffn_kernel(x_ref, g_in, sh_ref, sc_ref, wgu_ref, wo_ref, g_out, mg_ref, o_ref, *, chunk):
    x = x_ref[...]
    a = (_rms(x, g_in[...]) * (1.0 + sc_ref[...]) + sh_ref[...]).astype(MXU_DTYPE)
    hidden = wo_ref.shape[0]
    acc = None
    for c0 in range(0, hidden, chunk):
        gch = jnp.dot(a, wgu_ref[:, c0:c0 + chunk], preferred_element_type=F32)
        uch = jnp.dot(a, wgu_ref[:, hidden + c0:hidden + c0 + chunk], preferred_element_type=F32)
        act = (_silu(gch) * uch).astype(MXU_DTYPE)
        part = jnp.dot(act, wo_ref[c0:c0 + chunk, :], preferred_element_type=F32)
        acc = part if acc is None else acc + part
    o_ref[...] = x + mg_ref[...] * _rms(acc, g_out[...])


def _ffn(x, g_in, shift, scale, w_gate_up, w_out, g_out, mod_gate, tm=512):
    bsz, t, d = x.shape
    tm = min(tm, t)
    hidden = w_out.shape[0]
    chunk = hidden // 2
    tok = pl.BlockSpec((None, tm, d), lambda b, i: (b, i, 0))
    per_b = pl.BlockSpec((None, 1, d), lambda b, i: (b, 0, 0))
    vec = pl.BlockSpec((1, d), lambda b, i: (0, 0))
    return pl.pallas_call(
        functools.partial(_ffn_kernel, chunk=chunk),
        grid=(bsz, t // tm),
        in_specs=[tok, vec, per_b, per_b, _WHOLE_VMEM, _WHOLE_VMEM, vec, per_b],
        out_specs=tok,
        out_shape=jax.ShapeDtypeStruct((bsz, t, d), F32),
        compiler_params=_params(2),
        name="swiglu_ffn",
    )(x, g_in.astype(F32).reshape(1, d), shift, scale, w_gate_up, w_out,
      g_out.astype(F32).reshape(1, d), mod_gate)


def _rot_cols(w, half):
    return jnp.concatenate([-w[:, half:], w[:, :half]], axis=1)


def _rot_heads(w, heads, dim):
    k = w.shape[0]
    w3 = w.reshape(k, heads, dim)
    return jnp.concatenate([-w3[:, :, dim // 2:], w3[:, :, :dim // 2]], axis=2).reshape(k, heads * dim)


def _dup_heads(w, heads, dim):
    k = w.shape[0]
    w3 = w.reshape(k, heads, 1, dim)
    return jnp.broadcast_to(w3, (k, heads, 2, dim)).reshape(k, heads * 2 * dim)


def _rope_tables(rows, dim):
    row = jnp.repeat(jnp.arange(rows, dtype=F32), GRID_W)
    col = jnp.tile(jnp.arange(GRID_W, dtype=F32), rows)
    n_freq = dim // 4
    inv = ROPE_BASE ** (-jnp.arange(n_freq, dtype=F32) / n_freq)
    ang = jnp.concatenate([row[:, None] * inv, col[:, None] * inv], axis=-1)
    return jnp.cos(ang), jnp.sin(ang)


def _mla_tables(t, rope):
    ones = jnp.ones((t, MLA_NOPE), F32)
    zeros_n = jnp.zeros((t, MLA_NOPE), F32)
    pad = jnp.zeros((t, LANES - MLA_NOPE - MLA_ROPE), F32)
    if rope is None:
        c = jnp.ones((t, MLA_ROPE), F32)
        s = jnp.zeros((t, MLA_ROPE), F32)
    else:
        c = jnp.concatenate([rope[0], rope[0]], axis=1)
        s = jnp.concatenate([rope[1], rope[1]], axis=1)
    plain_q = jnp.concatenate([ones, c, pad], axis=1)
    rot = jnp.concatenate([zeros_n, s, pad], axis=1)
    plain_k = jnp.concatenate([zeros_n, c, pad], axis=1)
    return plain_q, rot, plain_k


def _swa_tables(t, rope):
    if rope is None:
        return jnp.ones((t, LANES), F32), jnp.zeros((t, LANES), F32)
    c = jnp.concatenate([rope[0]] * 4, axis=1)
    s = jnp.concatenate([rope[1]] * 4, axis=1)
    return c, s


def _layer_weights(w_in, mla_w_uq, mla_w_ukv):
    dm = w_in.shape[0]
    sizes = (384, 256, MLA_ROPE, SSD_INNER, SSD_INNER + 2 * SSD_GROUPS * SSD_STATE, 2 * SSD_HEADS,
             SWA_HEADS * SWA_HEAD_DIM, 2 * SWA_KV_HEADS * SWA_HEAD_DIM, 3 * HY_WIDTH, N_BRANCH * dm)
    offs = [0]
    for s in sizes:
        offs.append(offs[-1] + s)
    seg = lambda i: w_in[:, offs[i]:offs[i + 1]]
    cast = lambda w: w.astype(MXU_DTYPE)
    pad_kr = lambda w: jnp.pad(w, ((0, 0), (MLA_NOPE, LANES - MLA_NOPE - MLA_ROPE)))
    w_kr = seg(2)
    kvw = SWA_KV_HEADS * SWA_HEAD_DIM
    w_swk, w_swv = seg(7)[:, :kvw], seg(7)[:, kvw:]
    w = {
        "qa": cast(seg(0)), "ckv": cast(seg(1)),
        "krp": cast(pad_kr(w_kr)), "krr": cast(pad_kr(_rot_cols(w_kr, MLA_ROPE // 2))),
        "z": cast(seg(3)), "xbc": cast(seg(4)), "dt": cast(seg(5)),
        "swq": cast(seg(6)), "swq_rot": cast(_rot_heads(seg(6), SWA_HEADS, SWA_HEAD_DIM)),
        "swk": cast(_dup_heads(w_swk, SWA_KV_HEADS, SWA_HEAD_DIM)),
        "swk_rot": cast(_dup_heads(_rot_heads(w_swk, SWA_KV_HEADS, SWA_HEAD_DIM), SWA_KV_HEADS, SWA_HEAD_DIM)),
        "swv": cast(w_swv),
        "hy": cast(seg(8)), "gate": cast(seg(9)),
    }
    kq = mla_w_uq.shape[0]
    dq = MLA_NOPE + MLA_ROPE
    uq = mla_w_uq.reshape(kq, MLA_HEADS, dq)
    padq = ((0, 0), (0, 0), (0, LANES - dq))
    w["uq"] = cast(jnp.pad(uq, padq).reshape(kq, MLA_HEADS * LANES))
    uq_rope = uq[:, :, MLA_NOPE:]
    uq_rot = jnp.concatenate([-uq_rope[:, :, MLA_ROPE // 2:], uq_rope[:, :, :MLA_ROPE // 2]], axis=2)
    uq_rot = jnp.pad(uq_rot, ((0, 0), (0, 0), (MLA_NOPE, LANES - dq)))
    w["uq_rot"] = cast(uq_rot.reshape(kq, MLA_HEADS * LANES))
    kk = mla_w_ukv.shape[0]
    ukv = mla_w_ukv.reshape(kk, MLA_HEADS, MLA_NOPE + MLA_V)
    w["uk"] = cast(jnp.pad(ukv[:, :, :MLA_NOPE], ((0, 0), (0, 0), (0, LANES - MLA_NOPE))).reshape(kk, MLA_HEADS * LANES))
    w["uv"] = cast(ukv[:, :, MLA_NOPE:].reshape(kk, MLA_HEADS * MLA_V))
    return w


def _token_mixers(x, xc, mod_l, mod_c, ctx_out, rope_mla, rope_swa, norm_g0, w, p, dft_l, dft_c):
    bsz, t, dm = x.shape
    s_c = xc.shape[1]
    bf = MXU_DTYPE

    lat_names = ["qa", "ckv", "krp", "krr", "z", "xbc", "swq", "swq_rot", "swk", "swk_rot", "swv", "hy", "gate", "dt"]
    lat_dt = [F32 if k == "dt" else bf for k in lat_names]
    lat = dict(zip(lat_names, _norm_matmul(x, norm_g0, [w[k] for k in lat_names], lat_dt,
                                           shift=mod_l[0], scale=mod_l[1], name="in_proj_latent")))
    ctx_names = ["ckv", "krp", "xbc", "swk", "swv"] + (["qa", "z", "swq", "hy", "gate"] if ctx_out else []) + ["dt"]
    ctx_dt = [F32 if k == "dt" else bf for k in ctx_names]
    ctx = dict(zip(ctx_names, _norm_matmul(xc, norm_g0, [w[k] for k in ctx_names], ctx_dt,
                                           shift=mod_c[0], scale=mod_c[1], name="in_proj_context")))

    knp_l, v_l = _norm_matmul(lat["ckv"], p["mla_kv_norm"], [w["uk"], w["uv"]], [bf, bf], name="mla_kv_up")
    knp_c, v_c = _norm_matmul(ctx["ckv"], p["mla_kv_norm"], [w["uk"], w["uv"]], [bf, bf], name="mla_kv_up_ctx")
    qp, qr = _norm_matmul(lat["qa"], p["mla_q_norm"], [w["uq"], w["uq_rot"]], [bf, bf], name="mla_q_up")
    cq, sq, ck = _mla_tables(t, rope_mla)
    mla_q_scale = MLA_SCALE * LOG2E
    swa_q_scale = SWA_SCALE * LOG2E
    mla_l = _mla_attention(qp, qr, cq * mla_q_scale, sq * mla_q_scale, knp_l, lat["krp"], lat["krr"], ck, sq, v_l,
                           ctx=(knp_c, ctx["krp"], v_c))

    xbc = _dwconv([lat["xbc"], ctx["xbc"]], p["ssd_conv_w"], p["ssd_conv_b"], act=True, tc=384, name="ssd_conv")
    dt_raw = jnp.concatenate([lat["dt"], ctx["dt"]], axis=1)
    yf, yb = _ssd_scan(xbc, dt_raw, jnp.swapaxes(dt_raw, 1, 2), p["ssd_dt_bias"], p["ssd_a_log"], s_c // SSD_CHUNK)
    ssd_l = _ssd_output(yf, yb, xbc, lat["z"], p["ssd_d"], p["ssd_norm"], 0, tm=1024)

    csw, ssw = _swa_tables(t, rope_swa)
    swa_l = _swa_attention(p["swa_sink"], lat["swq"], lat["swq_rot"], csw * swa_q_scale, ssw * swa_q_scale,
                           ctx["swk"], ctx["swv"], lat=(lat["swk"], lat["swk_rot"], csw, ssw, lat["swv"]))

    hy_args = (p["hy_w1"], p["hy_b1"], p["hy_w2"], p["hy_b2"], p["hy_w3"], p["hy_freq"])
    spec_l = _hyena_filters(t, *hy_args, dft_l[0], dft_l[1])
    hy_l = _hyena(lat["hy"], p["hy_conv_w"], p["hy_conv_b"], spec_l, p["hy_d"], dft_l)

    w_branch = p["w_branch"].astype(bf)
    w_out = p["w_out"].astype(bf)
    x_new = _merge([mla_l, ssd_l, swa_l, hy_l], lat["gate"], w_branch, w_out, x, mod_l[2], p["norm_g1"])
    if not ctx_out:
        return x_new, None

    qp_c, = _norm_matmul(ctx["qa"], p["mla_q_norm"], [w["uq"]], [bf], name="mla_q_up_ctx")
    cq_c, sq_c, ck_c = _mla_tables(s_c, None)
    mla_c = _mla_attention(qp_c, qp_c, cq_c * mla_q_scale, sq_c, knp_c, ctx["krp"], ctx["krp"], ck_c, sq_c, v_c)
    ssd_c = _ssd_output(yf, yb, xbc, ctx["z"], p["ssd_d"], p["ssd_norm"], t)
    c1, s0 = _swa_tables(s_c, None)
    swa_c = _swa_attention(p["swa_sink"], ctx["swq"], ctx["swq"], c1 * swa_q_scale, s0, ctx["swk"], ctx["swv"])
    spec_c = _hyena_filters(s_c, *hy_args, dft_c[0], dft_c[1])
    hy_c = _hyena(ctx["hy"], p["hy_conv_w"], p["hy_conv_b"], spec_c, p["hy_d"], dft_c)
    xc_new = _merge([mla_c, ssd_c, swa_c, hy_c], ctx["gate"], w_branch, w_out, xc, mod_c[2], p["norm_g1"])
    return x_new, xc_new


def kernel(x, c, ctx, c_ctx, ada_w, ada_b, norm_g, w_in, mla_q_norm, mla_w_uq, mla_kv_norm, mla_w_ukv, ssd_conv_w, ssd_conv_b, ssd_dt_bias, ssd_a_log, ssd_d, ssd_norm, swa_sink, hy_conv_w, hy_conv_b, hy_w1, hy_b1, hy_w2, hy_b2, hy_w3, hy_freq, hy_d, w_branch, w_out, ffn_w_in, ffn_w_out):
    bsz, t, dm = x.shape
    s_c = ctx.shape[1]
    depth = ada_w.shape[0]
    rows = t // GRID_W
    rope_mla = _rope_tables(rows, MLA_ROPE)
    rope_swa = _rope_tables(rows, SWA_HEAD_DIM)
    dft_l = _dft_tables(t)
    dft_c = _dft_tables(s_c)
    cond_rows = 16
    cond = jnp.concatenate([c, c_ctx[None, :], jnp.zeros((cond_rows - bsz - 1, dm), F32)], axis=0)
    ffn_hidden = ffn_w_out.shape[1]
    xc = ctx
    for l in range(depth):
        ctx_out = l < depth - 1
        mod = _ada(cond, ada_w[l], ada_b[l])
        mod_l = [mod[:bsz, k * dm:(k + 1) * dm].reshape(bsz, 1, dm) for k in range(6)]
        mod_c = [jnp.broadcast_to(mod[bsz, k * dm:(k + 1) * dm].reshape(1, 1, dm), (bsz, 1, dm)) for k in range(6)]
        w = _layer_weights(w_in[l], mla_w_uq[l], mla_w_ukv[l])
        p = {"mla_q_norm": mla_q_norm[l], "mla_kv_norm": mla_kv_norm[l], "ssd_conv_w": ssd_conv_w[l],
             "ssd_conv_b": ssd_conv_b[l], "ssd_dt_bias": ssd_dt_bias[l], "ssd_a_log": ssd_a_log[l],
             "ssd_d": ssd_d[l], "ssd_norm": ssd_norm[l], "swa_sink": swa_sink[l], "hy_conv_w": hy_conv_w[l],
             "hy_conv_b": hy_conv_b[l], "hy_w1": hy_w1[l], "hy_b1": hy_b1[l], "hy_w2": hy_w2[l],
             "hy_b2": hy_b2[l], "hy_w3": hy_w3[l], "hy_freq": hy_freq[l], "hy_d": hy_d[l],
             "w_branch": w_branch[l], "w_out": w_out[l], "norm_g1": norm_g[l, 1]}
        x, xc_new = _token_mixers(x, xc, mod_l, mod_c, ctx_out, rope_mla, rope_swa, norm_g[l, 0], w, p, dft_l, dft_c)
        wgu = ffn_w_in[l].astype(MXU_DTYPE)
        wo = ffn_w_out[l].astype(MXU_DTYPE)
        x = _ffn(x, norm_g[l, 2], mod_l[3], mod_l[4], wgu, wo, norm_g[l, 3], mod_l[5])
        if ctx_out:
            xc = _ffn(xc_new, norm_g[l, 2], mod_c[3], mod_c[4], wgu, wo, norm_g[l, 3], mod_c[5])
    return x
```

```python
import functools
import math

import jax
import jax.numpy as jnp
from jax import lax
from jax.experimental import pallas as pl
from jax.experimental.pallas import tpu as pltpu

F32 = jnp.float32
MXU_DTYPE = jnp.bfloat16
HIGHEST = lax.Precision.HIGHEST

GRID_W = 64
EPS = 1e-6
ROPE_BASE = 10000.0
MLA_HEADS, MLA_NOPE, MLA_ROPE, MLA_V = 8, 64, 32, 64
MLA_SCALE = (MLA_NOPE + MLA_ROPE) ** -0.5
SSD_HEADS, SSD_HEADDIM, SSD_GROUPS, SSD_STATE, SSD_CHUNK = 8, 64, 2, 64, 128
SSD_INNER = SSD_HEADS * SSD_HEADDIM
SWA_HEADS, SWA_KV_HEADS, SWA_HEAD_DIM, SWA_WINDOW, SWA_BLOCK = 8, 2, 64, 128, 128
SWA_SCALE = SWA_HEAD_DIM ** -0.5
HY_WIDTH, HY_ORDER, HY_BANDS, HY_HIDDEN = 512, 2, 16, 64
HY_DECAY_TARGET, HY_FAST_DECAY, HY_SLOW_DECAY = 1e-2, 0.3, 1.5
N_BRANCH = 4
LANES = 128
SSD_BATCH_ROWS = 8
VT_ROWS = 80
LOG2E = math.log2(math.e)
DFT_SPLIT = 8
CONV_EDGE = 16
MLA_KEY_BLOCK = 256

VMEM_LIMIT = 56 * 1024 * 1024
_WHOLE_VMEM = pl.BlockSpec(memory_space=pltpu.VMEM)


def _params(n_grid, vmem=VMEM_LIMIT):
    return pltpu.CompilerParams(dimension_semantics=("arbitrary",) * n_grid, vmem_limit_bytes=vmem)


def _silu(x):
    return x * jax.nn.sigmoid(x)


def _softplus(x):
    return jnp.maximum(x, 0.0) + jnp.log1p(jnp.exp(-jnp.abs(x)))


def _rms(x, g):
    return x * lax.rsqrt(jnp.mean(x * x, axis=-1, keepdims=True) + EPS) * g


def _ada_kernel(s_ref, w_ref, b_ref, o_ref):
    s = _silu(s_ref[...])
    o_ref[...] = jnp.dot(s, w_ref[...], preferred_element_type=F32, precision=HIGHEST) + b_ref[...]


def _ada(cond, w, b, tn=1536):
    m, k = cond.shape
    n = w.shape[1]
    return pl.pallas_call(
        _ada_kernel,
        grid=(n // tn,),
        in_specs=[pl.BlockSpec((m, k), lambda j: (0, 0)),
                  pl.BlockSpec((k, tn), lambda j: (0, j)),
                  pl.BlockSpec((1, tn), lambda j: (0, j))],
        out_specs=pl.BlockSpec((m, tn), lambda j: (0, j)),
        out_shape=jax.ShapeDtypeStruct((m, n), F32),
        compiler_params=_params(1),
        name="ada_ln",
    )(cond, w, b.reshape(1, n))


def _norm_mm_kernel(*refs, modulate):
    it = iter(refs)
    x_ref, g_ref = next(it), next(it)
    if modulate:
        sh_ref, sc_ref = next(it), next(it)
    w_ref = next(it)
    o_refs = list(it)
    a = _rms(x_ref[...].astype(F32), g_ref[...])
    if modulate:
        a = a * (1.0 + sc_ref[...]) + sh_ref[...]
    a = a.astype(MXU_DTYPE)
    base = 0
    for o_ref in o_refs:
        n = o_ref.shape[1]
        for c0 in range(0, n, 1024):
            c1 = min(n, c0 + 1024)
            o_ref[:, c0:c1] = jnp.dot(a, w_ref[:, base + c0:base + c1],
                                      preferred_element_type=F32).astype(o_ref.dtype)
        base += n


def _norm_matmul(x, g, ws, out_dtypes, shift=None, scale=None, tm=512, name="norm_matmul"):
    bsz, t, k = x.shape
    tm = min(tm, t)
    modulate = shift is not None
    in_specs = [pl.BlockSpec((None, tm, k), lambda b, i: (b, i, 0)),
                pl.BlockSpec((1, k), lambda b, i: (0, 0))]
    args = [x, g.reshape(1, k).astype(F32)]
    if modulate:
        in_specs += [pl.BlockSpec((None, 1, k), lambda b, i: (b, 0, 0))] * 2
        args += [shift, scale]
    assert all(w.shape[1] % LANES == 0 for w in ws[:-1])
    in_specs += [_WHOLE_VMEM]
    args += [jnp.concatenate(ws, axis=1) if len(ws) > 1 else ws[0]]
    return pl.pallas_call(
        functools.partial(_norm_mm_kernel, modulate=modulate),
        grid=(bsz, t // tm),
        in_specs=in_specs,
        out_specs=[pl.BlockSpec((None, tm, w.shape[1]), lambda b, i: (b, i, 0)) for w in ws],
        out_shape=[jax.ShapeDtypeStruct((bsz, t, w.shape[1]), dt) for w, dt in zip(ws, out_dtypes)],
        compiler_params=_params(2),
        name=name,
    )(*args)


def _mla_kernel(*refs, t_l, s_c):
    if s_c:
        (qp, qr, cq, sq, knp_l, krp_l, krr_l, ck, sk, v_l, knp_c, krp_c, v_c, o_ref, kcat, vt) = refs
    else:
        (qp, qr, cq, sq, knp_l, krp_l, krr_l, ck, sk, v_l, o_ref, kcat, vt) = refs

    @pl.when(pl.program_id(1) == 0)
    def _():
        kro = (krp_l[...].astype(F32) * ck[...] + krr_l[...].astype(F32) * sk[...]).astype(kcat.dtype)
        for h in range(MLA_HEADS):
            blk = slice(h * LANES, (h + 1) * LANES)
            kcat[h, 0:t_l, :] = knp_l[:, blk] + kro
            if s_c:
                kcat[h, t_l:t_l + s_c, :] = knp_c[:, blk] + krp_c[...]
            vt[h, MLA_V:, :] = jnp.ones((VT_ROWS - MLA_V, t_l + s_c), vt.dtype)
        for j in range(MLA_HEADS // 2):
            blk = slice(j * LANES, (j + 1) * LANES)
            vp = v_l[:, blk].astype(F32).T
            vt[2 * j, 0:MLA_V, 0:t_l] = vp[0:MLA_V].astype(vt.dtype)
            vt[2 * j + 1, 0:MLA_V, 0:t_l] = vp[MLA_V:].astype(vt.dtype)
            if s_c:
                vp = v_c[:, blk].astype(F32).T
                vt[2 * j, 0:MLA_V, t_l:t_l + s_c] = vp[0:MLA_V].astype(vt.dtype)
                vt[2 * j + 1, 0:MLA_V, t_l:t_l + s_c] = vp[MLA_V:].astype(vt.dtype)

    cqv, sqv = cq[...], sq[...]
    tq = qp.shape[0]
    s_tot = t_l + s_c
    nt = (((1,), (1,)), ((), ()))
    heads = range(MLA_HEADS)
    q, m, acc = {}, {}, {}
    for h in heads:
        blk = slice(h * LANES, (h + 1) * LANES)
        q[h] = (qp[:, blk].astype(F32) * cqv + qr[:, blk].astype(F32) * sqv).astype(MXU_DTYPE)
        m[h] = jnp.full((1, tq), -1e30, F32)
        acc[h] = jnp.zeros((VT_ROWS, tq), F32)
    for k0 in range(0, s_tot, MLA_KEY_BLOCK):
        k1 = min(s_tot, k0 + MLA_KEY_BLOCK)
        s = {h: lax.dot_general(kcat[h, k0:k1, :], q[h], nt, preferred_element_type=F32) for h in heads}
        m_new = {h: jnp.maximum(m[h], jnp.max(s[h], axis=0, keepdims=True)) for h in heads}
        p = {h: jnp.exp2(s[h] - m_new[h]).astype(MXU_DTYPE) for h in heads}
        pv = {h: jnp.dot(vt[h, :, k0:k1], p[h], preferred_element_type=F32) for h in heads}
        for h in heads:
            acc[h] = acc[h] * jnp.exp2(m[h] - m_new[h]) + pv[h]
            m[h] = m_new[h]
    for j in range(MLA_HEADS // 2):
        halves = [acc[h][0:MLA_V] * (1.0 / acc[h][MLA_V:MLA_V + 1]) for h in (2 * j, 2 * j + 1)]
        o_ref[:, j * LANES:(j + 1) * LANES] = jnp.concatenate(halves, axis=0).T.astype(o_ref.dtype)


def _mla_attention(qp, qr, cq, sq, knp_l, krp_l, krr_l, ck, sk, v_l, ctx=None, tq=512):
    bsz, t_q, _ = qp.shape
    t_l = knp_l.shape[1]
    s_c = 0 if ctx is None else ctx[0].shape[1]
    tq = min(tq, t_q)
    hw = MLA_HEADS * LANES
    vw = MLA_HEADS * MLA_V
    per_b = lambda rows, cols: pl.BlockSpec((None, rows, cols), lambda b, i: (b, 0, 0))
    in_specs = [pl.BlockSpec((None, tq, hw), lambda b, i: (b, i, 0)),
                pl.BlockSpec((None, tq, hw), lambda b, i: (b, i, 0)),
                pl.BlockSpec((tq, LANES), lambda b, i: (i, 0)),
                pl.BlockSpec((tq, LANES), lambda b, i: (i, 0)),
                per_b(t_l, hw), per_b(t_l, LANES), per_b(t_l, LANES),
                pl.BlockSpec((t_l, LANES), lambda b, i: (0, 0)),
                pl.BlockSpec((t_l, LANES), lambda b, i: (0, 0)),
                per_b(t_l, vw)]
    args = [qp, qr, cq, sq, knp_l, krp_l, krr_l, ck, sk, v_l]
    if s_c:
        in_specs += [per_b(s_c, hw), per_b(s_c, LANES), per_b(s_c, vw)]
        args += list(ctx)
    return pl.pallas_call(
        functools.partial(_mla_kernel, t_l=t_l, s_c=s_c),
        grid=(bsz, t_q // tq),
        in_specs=in_specs,
        out_specs=pl.BlockSpec((None, tq, vw), lambda b, i: (b, i, 0)),
        out_shape=jax.ShapeDtypeStruct((bsz, t_q, vw), MXU_DTYPE),
        scratch_shapes=[pltpu.VMEM((MLA_HEADS, t_l + s_c, LANES), MXU_DTYPE),
                        pltpu.VMEM((MLA_HEADS, VT_ROWS, t_l + s_c), MXU_DTYPE)],
        compiler_params=_params(2),
        name="mla_attention",
    )(*args)


def _swa_kernel(*refs, band, t_k):
    if band:
        (sink, q, qrot, cq, sq, k, krot, ck, sk, v, kc, vc, o_ref, kro, vt, vct) = refs
    else:
        (sink, q, qrot, cq, sq, kc, vc, o_ref, vct) = refs
    i = pl.program_id(1)
    tq = q.shape[0]
    gw = 2 * SWA_HEAD_DIM
    dh = SWA_HEAD_DIM
    kb_rows = min(tq + 2 * SWA_BLOCK, t_k)

    @pl.when(i == 0)
    def _():
        def put_vt(dst, src):
            vtr = src[...].astype(F32).T
            for g in range(SWA_KV_HEADS):
                dst[g, 0:dh, :] = vtr[g * dh:(g + 1) * dh].astype(dst.dtype)
                dst[g, dh:, :] = jnp.ones((VT_ROWS - dh, src.shape[0]), dst.dtype)

        put_vt(vct, vc)
        if band:
            put_vt(vt, v)
            ckv, skv = ck[...], sk[...]
            for g in range(SWA_KV_HEADS):
                blk = slice(g * gw, (g + 1) * gw)
                kro[:, blk] = (k[:, blk].astype(F32) * ckv + krot[:, blk].astype(F32) * skv).astype(kro.dtype)

    if band:
        start = pl.multiple_of(jnp.clip(i * tq - SWA_BLOCK, 0, t_k - kb_rows), SWA_BLOCK)
        k_pos = start + lax.broadcasted_iota(jnp.int32, (kb_rows, tq), 0)
        q_pos = i * tq + lax.broadcasted_iota(jnp.int32, (kb_rows, tq), 1)
        in_band = jnp.abs(q_pos - k_pos) <= SWA_WINDOW

    lane_lo = lax.broadcasted_iota(jnp.int32, (tq, LANES), 1) < dh
    cqv, sqv = cq[...], sq[...]
    nt = (((1,), (1,)), ((), ()))
    heads = range(SWA_HEADS)
    group = {h: h // (SWA_HEADS // SWA_KV_HEADS) for h in heads}
    gblk = {h: slice(group[h] * gw, (group[h] + 1) * gw) for h in heads}
    qm, snk = {}, {}
    for j in range(SWA_HEADS // 2):
        blk = slice(j * LANES, (j + 1) * LANES)
        qro = q[:, blk].astype(F32) * cqv + qrot[:, blk].astype(F32) * sqv
        qm[2 * j] = jnp.where(lane_lo, qro, 0.0).astype(MXU_DTYPE)
        qm[2 * j + 1] = jnp.where(lane_lo, 0.0, qro).astype(MXU_DTYPE)
    for h in heads:
        snk[h] = sink[h] * LOG2E
    s_c = {h: lax.dot_general(kc[:, gblk[h]], qm[h], nt, preferred_element_type=F32) for h in heads}
    m = {h: jnp.maximum(jnp.max(s_c[h], axis=0, keepdims=True), snk[h]) for h in heads}
    if band:
        s_b = {h: jnp.where(in_band, lax.dot_general(kro[pl.ds(start, kb_rows), gblk[h]], qm[h], nt,
                                                     preferred_element_type=F32), -1e30) for h in heads}
        m = {h: jnp.maximum(m[h], jnp.max(s_b[h], axis=0, keepdims=True)) for h in heads}
    o = {h: jnp.dot(vct[group[h]], jnp.exp2(s_c[h] - m[h]).astype(MXU_DTYPE), preferred_element_type=F32)
         for h in heads}
    if band:
        o = {h: o[h] + jnp.dot(vt[group[h], :, pl.ds(start, kb_rows)], jnp.exp2(s_b[h] - m[h]).astype(MXU_DTYPE),
                               preferred_element_type=F32) for h in heads}
    for j in range(SWA_HEADS // 2):
        halves = []
        for h in (2 * j, 2 * j + 1):
            l = o[h][dh:dh + 1] + jnp.exp2(snk[h] - m[h])
            halves.append(o[h][0:dh] * (1.0 / l))
        o_ref[:, j * LANES:(j + 1) * LANES] = jnp.concatenate(halves, axis=0).T.astype(o_ref.dtype)


def _swa_attention(sink, q, qrot, cq, sq, kc, vc, lat=None):
    bsz, t, hw = q.shape
    s_c = kc.shape[1]
    tq = min(t, 2 * SWA_BLOCK)
    kw, vw = kc.shape[2], vc.shape[2]
    band = lat is not None
    per_b = lambda rows, cols: pl.BlockSpec((None, rows, cols), lambda b, i: (b, 0, 0))
    qspec = pl.BlockSpec((None, tq, hw), lambda b, i: (b, i, 0))
    tspec = pl.BlockSpec((tq, LANES), lambda b, i: (i, 0))
    in_specs = [pl.BlockSpec(memory_space=pltpu.SMEM), qspec, qspec, tspec, tspec]
    args = [sink.astype(F32), q, qrot, cq, sq]
    scratch = []
    if band:
        k, krot, ck, sk, v = lat
        full_t = pl.BlockSpec((t, LANES), lambda b, i: (0, 0))
        in_specs += [per_b(t, kw), per_b(t, kw), full_t, full_t, per_b(t, vw)]
        args += [k, krot, ck, sk, v]
        scratch = [pltpu.VMEM((t, kw), MXU_DTYPE), pltpu.VMEM((SWA_KV_HEADS, VT_ROWS, t), MXU_DTYPE)]
    scratch += [pltpu.VMEM((SWA_KV_HEADS, VT_ROWS, s_c), MXU_DTYPE)]
    in_specs += [per_b(s_c, kw), per_b(s_c, vw)]
    args += [kc, vc]
    return pl.pallas_call(
        functools.partial(_swa_kernel, band=band, t_k=t),
        grid=(bsz, t // tq),
        in_specs=in_specs,
        out_specs=pl.BlockSpec((None, tq, hw), lambda b, i: (b, i, 0)),
        out_shape=jax.ShapeDtypeStruct((bsz, t, hw), MXU_DTYPE),
        scratch_shapes=scratch,
        compiler_params=_params(2),
        name="swa_attention" if band else "ctx_sink_attention",
    )(*args)


def _dwconv_kernel(*refs, n_seg, act):
    x_refs = refs[:n_seg]
    w_ref, b_ref, o_ref = refs[n_seg:]
    taps = w_ref.shape[0]
    off = 0
    for x_ref in x_refs:
        t = x_ref.shape[0]
        x = x_ref[...].astype(F32)
        e = CONV_EDGE
        row = lax.broadcasted_iota(jnp.int32, (e, x.shape[1]), 0)
        acc = jnp.zeros_like(x) + b_ref[...]
        top = jnp.zeros((e, x.shape[1]), F32) + b_ref[...]
        bot = top
        for kk in range(taps):
            d = kk - taps // 2
            wk = w_ref[kk:kk + 1, :]
            xs = x if d == 0 else pltpu.roll(x, (-d) % t, axis=0)
            acc = acc + xs * wk
            top = top + jnp.where(row + d >= 0, xs[0:e], 0.0) * wk
            bot = bot + jnp.where(row + d < e, xs[t - e:t], 0.0) * wk
        if act:
            acc, top, bot = _silu(acc), _silu(top), _silu(bot)
        o_ref[off:off + t, :] = acc.astype(o_ref.dtype)
        o_ref[off:off + e, :] = top.astype(o_ref.dtype)
        o_ref[off + t - e:off + t, :] = bot.astype(o_ref.dtype)
        off += t


def _dwconv(xs, w, b, act, tc=256, name="dwconv"):
    bsz, _, c = xs[0].shape
    t_tot = sum(x.shape[1] for x in xs)
    taps = w.shape[0]
    in_specs = [pl.BlockSpec((None, x.shape[1], tc), lambda bb, j: (bb, 0, j)) for x in xs]
    in_specs += [pl.BlockSpec((taps, tc), lambda bb, j: (0, j)), pl.BlockSpec((1, tc), lambda bb, j: (0, j))]
    return pl.pallas_call(
        functools.partial(_dwconv_kernel, n_seg=len(xs), act=act),
        grid=(bsz, c // tc),
        in_specs=in_specs,
        out_specs=pl.BlockSpec((None, t_tot, tc), lambda bb, j: (bb, 0, j)),
        out_shape=jax.ShapeDtypeStruct((bsz, t_tot, c), MXU_DTYPE),
        compiler_params=_params(2),
        name=name,
    )(*xs, w.astype(F32), b.reshape(1, c).astype(F32))


def _ssd_scan_kernel(xs_f, bm_f, cm_f, dt_f, dtt_f, xs_b, bm_b, cm_b, dt_b, dtt_b,
                     b_row, al_row, b_col, al_col, yf_ref, yb_ref, st_ref):
    @pl.when(pl.program_id(1) == 0)
    def _():
        st_ref[...] = jnp.zeros_like(st_ref)

    q = SSD_CHUNK
    row = lax.broadcasted_iota(jnp.int32, (q, q), 0)
    col = lax.broadcasted_iota(jnp.int32, (q, q), 1)
    lane_lo = col < SSD_HEADDIM
    a_row = -jnp.exp(al_row[...])
    a_col = -jnp.exp(al_col[...])
    nt = (((1,), (1,)), ((), ()))
    tn = (((0,), (0,)), ((), ()))
    dirs = ((xs_f, bm_f, cm_f, dt_f, dtt_f, yf_ref), (xs_b, bm_b, cm_b, dt_b, dtt_b, yb_ref))
    units = [(bi, d) for bi in range(st_ref.shape[0]) for d in range(2)]
    pairs_per_group = SSD_HEADS // SSD_GROUPS // 2
    feeds = {0: row >= col, 1: row <= col}
    last = {0: q - 1, 1: 0}
    dt, dtt, cs, cst, cg, bg, gmat, sc, ecol, wcol, dec = ({} for _ in range(11))
    for u in units:
        bi, d = u
        dt_ref, dtt_ref = dirs[d][3], dirs[d][4]
        dt[u] = _softplus(dt_ref[bi] + b_row[...])
        dtt[u] = _softplus(dtt_ref[bi] + b_col[...])
    for u in units:
        d = u[1]
        cs[u] = jnp.dot(feeds[d].astype(F32), dt[u] * a_row, preferred_element_type=F32, precision=HIGHEST)
        cst[u] = jnp.dot(dtt[u] * a_col, feeds[1 - d].astype(F32), preferred_element_type=F32, precision=HIGHEST)
    for u in units:
        bi, d = u
        bblk, cblk = dirs[d][1][bi].astype(F32), dirs[d][2][bi].astype(F32)
        for g in range(SSD_GROUPS):
            gmask = (col // SSD_STATE) == g
            cg[u, g] = jnp.where(gmask, cblk, 0.0).astype(MXU_DTYPE)
            bg[u, g] = jnp.where(gmask, bblk, 0.0)
            gmat[u, g] = lax.dot_general(cg[u, g], bg[u, g].astype(MXU_DTYPE), nt,
                                         preferred_element_type=F32)
    for u in units:
        d = u[1]
        tot = cs[u][last[d]:last[d] + 1, :]
        e_all = jnp.exp(cs[u])
        w_all = dt[u] * jnp.exp(tot - cs[u])
        d_all = jnp.exp(tot)
        for h in range(SSD_HEADS):
            c = d * SSD_HEADS + h
            ccol = cs[u][:, c:c + 1]
            crow = cst[u][c:c + 1, :]
            lmat = jnp.where(feeds[d], jnp.exp(ccol - crow), 0.0)
            sc[u, h] = (gmat[u, h // (SSD_HEADS // SSD_GROUPS)] * lmat * dtt[u][c:c + 1, :]).astype(MXU_DTYPE)
            ecol[u, h] = e_all[:, c:c + 1]
            wcol[u, h] = w_all[:, c:c + 1]
            dec[u, h] = d_all[:, c:c + 1]
    for u in units:
        bi, d = u
        xs_ref, y_ref = dirs[d][0], dirs[d][5]
        for j in range(SSD_HEADS // 2):
            g = j // pairs_per_group
            xb = xs_ref[bi, :, j * LANES:(j + 1) * LANES]
            xblk = xb.astype(F32)
            s_in = st_ref[bi, d, j]
            y_inter = jnp.dot(cg[u, g], s_in.astype(MXU_DTYPE), preferred_element_type=F32)
            ys, news = [], []
            for hh in range(2):
                h = 2 * j + hh
                ys.append(jnp.dot(sc[u, h], xb, preferred_element_type=F32) + ecol[u, h] * y_inter)
                bw = (bg[u, g] * wcol[u, h]).astype(MXU_DTYPE)
                xh = jnp.where(lane_lo if hh == 0 else ~lane_lo, xblk, 0.0).astype(MXU_DTYPE)
                news.append(lax.dot_general(bw, xh, tn, preferred_element_type=F32))
            y_ref[bi, :, j * LANES:(j + 1) * LANES] = jnp.where(lane_lo, ys[0], ys[1]).astype(y_ref.dtype)
            st_ref[bi, d, j] = (s_in * jnp.where(lane_lo, dec[u, 2 * j], dec[u, 2 * j + 1])
                                + news[0] + news[1])


def _ssd_scan(xbc, dt_raw, dt_raw_t, dt_bias, a_log, nc_ctx):
    bsz, t_c, _ = xbc.shape
    q = SSD_CHUNK
    nc = t_c // q
    nh2 = 2 * SSD_HEADS

    def fwd(s):
        return jnp.where(s < nc_ctx, nc - nc_ctx + s, s - nc_ctx)

    def bwd(s):
        return nc - 1 - s

    bt = math.gcd(bsz, SSD_BATCH_ROWS)

    def specs(order):
        return [pl.BlockSpec((bt, q, SSD_INNER), lambda b, s: (b, order(s), 0)),
                pl.BlockSpec((bt, q, LANES), lambda b, s: (b, order(s), SSD_INNER // LANES)),
                pl.BlockSpec((bt, q, LANES), lambda b, s: (b, order(s), SSD_INNER // LANES + 1)),
                pl.BlockSpec((bt, q, nh2), lambda b, s: (b, order(s), 0)),
                pl.BlockSpec((bt, nh2, q), lambda b, s: (b, 0, order(s)))]

    small = lambda r, c: pl.BlockSpec((r, c), lambda b, s: (0, 0))
    bias = dt_bias.astype(F32).reshape(1, nh2)
    alog = a_log.astype(F32).reshape(1, nh2)
    return pl.pallas_call(
        _ssd_scan_kernel,
        grid=(bsz // bt, nc),
        in_specs=specs(fwd) + specs(bwd) + [small(1, nh2), small(1, nh2), small(nh2, 1), small(nh2, 1)],
        out_specs=[pl.BlockSpec((bt, q, SSD_INNER), lambda b, s: (b, fwd(s), 0)),
                   pl.BlockSpec((bt, q, SSD_INNER), lambda b, s: (b, bwd(s), 0))],
        out_shape=[jax.ShapeDtypeStruct((bsz, t_c, SSD_INNER), MXU_DTYPE)] * 2,
        scratch_shapes=[pltpu.VMEM((bt, 2, SSD_HEADS // 2, LANES, LANES), F32)],
        compiler_params=_params(2),
        name="ssd_scan",
    )(xbc, xbc, xbc, dt_raw, dt_raw_t, xbc, xbc, xbc, dt_raw, dt_raw_t,
      bias, alog, bias.reshape(nh2, 1), alog.reshape(nh2, 1))


def _ssd_out_kernel(yf, yb, xs, z, dexp, ng, o_ref):
    y = yf[...].astype(F32) + yb[...].astype(F32) + dexp[...] * xs[...].astype(F32)
    yg = y * _silu(z[...].astype(F32))
    gw = SSD_INNER // SSD_GROUPS
    for g in range(SSD_GROUPS):
        blk = slice(g * gw, (g + 1) * gw)
        o_ref[:, blk] = _rms(yg[:, blk], ng[:, blk]).astype(o_ref.dtype)


def _ssd_output(yf, yb, xbc, z, d_skip, norm_g, row0, tm=256):
    bsz, t, _ = z.shape
    tm = min(tm, t)
    off = row0 // tm
    sp_y = pl.BlockSpec((None, tm, SSD_INNER), lambda b, i: (b, i + off, 0))
    vec = pl.BlockSpec((1, SSD_INNER), lambda b, i: (0, 0))
    dexp = jnp.repeat(d_skip.astype(F32), SSD_HEADDIM).reshape(1, SSD_INNER)
    return pl.pallas_call(
        _ssd_out_kernel,
        grid=(bsz, t // tm),
        in_specs=[sp_y, sp_y, sp_y, pl.BlockSpec((None, tm, SSD_INNER), lambda b, i: (b, i, 0)), vec, vec],
        out_specs=pl.BlockSpec((None, tm, SSD_INNER), lambda b, i: (b, i, 0)),
        out_shape=jax.ShapeDtypeStruct((bsz, t, SSD_INNER), MXU_DTYPE),
        compiler_params=_params(2),
        name="ssd_output",
    )(yf, yb, xbc, z, dexp, norm_g.astype(F32).reshape(1, SSD_INNER))


def _hy_mlp_kernel(feat_ref, w1, b1, w2, b2, w3, fr, delta, h_ref, cs_ref):
    feat = feat_ref[...]
    h = jnp.sin(fr[...] * (jnp.dot(feat, w1[...], preferred_element_type=F32, precision=HIGHEST) + b1[...]))
    h = jnp.sin(fr[...] * (jnp.dot(h, w2[...], preferred_element_type=F32, precision=HIGHEST) + b2[...]))
    h = jnp.dot(h, w3[...], preferred_element_type=F32, precision=HIGHEST)
    h = h * jnp.exp(-feat[:, 0:1] * delta[...])
    h_ref[...] = h
    s = jnp.sum(jnp.abs(h), axis=0, keepdims=True)

    @pl.when(pl.program_id(0) == 0)
    def _():
        cs_ref[...] = s

    @pl.when(pl.program_id(0) != 0)
    def _():
        cs_ref[...] += s


def _hy_spectrum_kernel(h0_ref, h1_ref, c0_ref, c1_ref, cm_ref, sf_ref, ar_ref, ai_ref, br_ref):
    n = h0_ref.shape[0]
    inv = 1.0 / (c0_ref[...] + c1_ref[...])
    row = lax.broadcasted_iota(jnp.int32, h0_ref.shape, 0)
    first = row == 0
    h0 = h0_ref[...] * inv
    h1 = jnp.where(first, 0.0, h1_ref[...] * inv)
    a = h0 + h1
    kr = jnp.dot(cm_ref[...], a.astype(MXU_DTYPE), preferred_element_type=F32)
    kq = jnp.dot(sf_ref[...], (h1 - h0).astype(MXU_DTYPE), preferred_element_type=F32)
    k_nyq = jnp.sum(jnp.where(row % 2 == 0, a, -a), axis=0, keepdims=True)
    inv_n = 1.0 / (2 * n)
    ar = kr * jnp.where(first, inv_n, 2.0 * inv_n)
    ar_ref[...] = ar
    ai_ref[...] = jnp.where(first, 0.0, kq * (2.0 * inv_n))
    br_ref[...] = jnp.where(first, k_nyq * inv_n, ar)


def _hyena_filters(n, w1, b1, w2, b2, w3, freq, cm, sf):
    t = jnp.arange(n, dtype=F32)
    tnorm = t / n
    bands = jnp.linspace(1e-4, HY_BANDS - 1, HY_BANDS, dtype=F32)
    ang = 2 * math.pi * t[:, None] * bands[None, :] / n
    feat = jnp.concatenate([tnorm[:, None], jnp.cos(ang), -jnp.sin(ang)], axis=-1)
    emb = feat.shape[1]
    feat = jnp.pad(feat, ((0, 0), (0, HY_HIDDEN - emb)))
    w1p = jnp.pad(w1.astype(F32), ((0, HY_HIDDEN - emb), (0, 0)))
    deltas = jnp.abs(jnp.linspace(math.log(HY_DECAY_TARGET) / HY_SLOW_DECAY,
                                  math.log(HY_DECAY_TARGET) / HY_FAST_DECAY, HY_WIDTH, dtype=F32))
    ncol = HY_ORDER * 2 * HY_WIDTH
    delta_row = jnp.tile(deltas, HY_ORDER * 2).reshape(1, ncol)
    tt = min(n, 256)
    small = lambda r, c: pl.BlockSpec((r, c), lambda i: (0, 0))
    row = lambda v: v.astype(F32).reshape(1, -1)
    h, colsum = pl.pallas_call(
        _hy_mlp_kernel,
        grid=(n // tt,),
        in_specs=[pl.BlockSpec((tt, HY_HIDDEN), lambda i: (i, 0)),
                  small(HY_HIDDEN, HY_HIDDEN), small(1, HY_HIDDEN),
                  small(HY_HIDDEN, HY_HIDDEN), small(1, HY_HIDDEN),
                  small(HY_HIDDEN, ncol), small(1, HY_HIDDEN), small(1, ncol)],
        out_specs=[pl.BlockSpec((tt, ncol), lambda i: (i, 0)), small(1, ncol)],
        out_shape=[jax.ShapeDtypeStruct((n, ncol), F32), jax.ShapeDtypeStruct((1, ncol), F32)],
        compiler_params=_params(1),
        name="hyena_filter_mlp",
    )(feat, w1p, row(b1), w2.astype(F32), row(b2), w3.astype(F32), row(freq), delta_row)

    tc = 256
    per_o = HY_WIDTH // tc
    side0 = lambda jc: (jc // per_o) * 2 * per_o + jc % per_o
    side1 = lambda jc: (jc // per_o) * 2 * per_o + per_o + jc % per_o
    nout = HY_ORDER * HY_WIDTH
    out_spec = pl.BlockSpec((n, tc), lambda jc: (0, jc))
    return pl.pallas_call(
        _hy_spectrum_kernel,
        grid=(nout // tc,),
        in_specs=[pl.BlockSpec((n, tc), lambda jc: (0, side0(jc))),
                  pl.BlockSpec((n, tc), lambda jc: (0, side1(jc))),
                  pl.BlockSpec((1, tc), lambda jc: (0, side0(jc))),
                  pl.BlockSpec((1, tc), lambda jc: (0, side1(jc))),
                  _WHOLE_VMEM, _WHOLE_VMEM],
        out_specs=[out_spec] * 3,
        out_shape=[jax.ShapeDtypeStruct((n, nout), F32)] * 3,
        compiler_params=_params(1),
        name="hyena_filter_spectrum",
    )(h, h, colsum, colsum, cm, sf)


def _hy_fwd_kernel(u_ref, cm_ref, sf_ref, ar_ref, ai_ref, br_ref, yr_ref, yi_ref):
    u = u_ref[...].astype(MXU_DTYPE)
    p = jnp.dot(cm_ref[...], u, preferred_element_type=F32)
    q = jnp.dot(sf_ref[...], u, preferred_element_type=F32)
    ai = ai_ref[...]
    yr_ref[...] = (p * ar_ref[...] + q * ai).astype(yr_ref.dtype)
    yi_ref[...] = (q * br_ref[...] - p * ai).astype(yi_ref.dtype)


def _hy_inv_kernel(yr_ref, yi_ref, cm_ref, si_ref, u_ref, xg_ref, d_ref, o_ref):
    y = (jnp.dot(cm_ref[...], yr_ref[...], preferred_element_type=F32)
         + jnp.dot(si_ref[...], yi_ref[...], preferred_element_type=F32))
    u = u_ref[...].astype(F32)
    o_ref[...] = (xg_ref[...].astype(F32) * (y + u * d_ref[...])).astype(o_ref.dtype)


def _hyena_conv(u, u_col0, xg, xg_col0, spectra, order, d, tables, out_dtype, tc=256):
    cm, sf, si = tables
    ar, ai, br = spectra
    bsz, n, _ = u.shape
    nct = HY_WIDTH // tc
    ucol, gcol, scol = u_col0 // tc, xg_col0 // tc, order * nct
    tok = lambda c0: pl.BlockSpec((None, n, tc), lambda c, b: (b, 0, c0 + c))
    spec_sp = pl.BlockSpec((n, tc), lambda c, b: (0, scol + c))
    mid = pl.BlockSpec((None, n, tc), lambda c, b: (b, 0, c))
    yr, yi = pl.pallas_call(
        _hy_fwd_kernel,
        grid=(nct, bsz),
        in_specs=[tok(ucol), _WHOLE_VMEM, _WHOLE_VMEM, spec_sp, spec_sp, spec_sp],
        out_specs=[mid, mid],
        out_shape=[jax.ShapeDtypeStruct((bsz, n, HY_WIDTH), MXU_DTYPE)] * 2,
        compiler_params=_params(2),
        name="hyena_dft_forward",
    )(u, cm, sf, ar, ai, br)
    return pl.pallas_call(
        _hy_inv_kernel,
        grid=(nct, bsz),
        in_specs=[mid, mid, _WHOLE_VMEM, _WHOLE_VMEM, tok(ucol), tok(gcol),
                  pl.BlockSpec((1, tc), lambda c, b: (0, c))],
        out_specs=mid,
        out_shape=jax.ShapeDtypeStruct((bsz, n, HY_WIDTH), out_dtype),
        compiler_params=_params(2),
        name="hyena_dft_inverse",
    )(yr, yi, cm, si, u, xg, d.astype(F32).reshape(1, HY_WIDTH))


def _dft_tables(n):
    idx = jnp.arange(n, dtype=jnp.int32)
    alt = jnp.where(idx % 2 == 0, 1.0, -1.0).astype(F32)
    def rows(freqs):
        ang = ((freqs[:, None] * idx[None, :]) % (2 * n)).astype(F32) * (math.pi / n)
        return jnp.cos(ang), jnp.sin(ang)
    c_hi, s_hi = rows(idx[::DFT_SPLIT])
    c_lo, s_lo = rows(idx[:DFT_SPLIT])
    cm = (c_hi[:, None, :] * c_lo[None, :, :] - s_hi[:, None, :] * s_lo[None, :, :]).reshape(n, n)
    sm = (s_hi[:, None, :] * c_lo[None, :, :] + c_hi[:, None, :] * s_lo[None, :, :]).reshape(n, n)
    sf = jnp.where(idx[:, None] == 0, alt[None, :], sm)
    si = jnp.where(idx[None, :] == 0, alt[:, None], sm)
    return cm.astype(MXU_DTYPE), sf.astype(MXU_DTYPE), si.astype(MXU_DTYPE)


def _hyena(hy, conv_w, conv_b, spectra, d, tables):
    u3 = _dwconv([hy], conv_w, conv_b, act=False, tc=512, name="hyena_short_conv")
    z = _hyena_conv(u3, 0, u3, HY_WIDTH, spectra, 0, d[0], tables, MXU_DTYPE)
    return _hyena_conv(z, 0, u3, 2 * HY_WIDTH, spectra, 1, d[1], tables, MXU_DTYPE)


def _merge_kernel(b0, b1, b2, b3, gate_ref, wb_ref, wo_ref, x_ref, mg_ref, g_ref, o_ref):
    d = x_ref.shape[1]
    acc = None
    for i, br in enumerate((b0, b1, b2, b3)):
        proj = jnp.dot(br[...], wb_ref[i], preferred_element_type=F32)
        term = jax.nn.sigmoid(gate_ref[:, i * d:(i + 1) * d].astype(F32)) * proj
        acc = term if acc is None else acc + term
    y = jnp.dot(acc.astype(MXU_DTYPE), wo_ref[...], preferred_element_type=F32)
    o_ref[...] = x_ref[...] + mg_ref[...] * _rms(y, g_ref[...])


def _merge(branches, gate, w_branch, w_out, x, mod_gate, g, tm=512):
    bsz, t, d = x.shape
    tm = min(tm, t)
    bw = branches[0].shape[2]
    tok = lambda w: pl.BlockSpec((None, tm, w), lambda b, i: (b, i, 0))
    return pl.pallas_call(
        _merge_kernel,
        grid=(bsz, t // tm),
        in_specs=[tok(bw)] * N_BRANCH + [tok(N_BRANCH * d), _WHOLE_VMEM, _WHOLE_VMEM, tok(d),
                                         pl.BlockSpec((None, 1, d), lambda b, i: (b, 0, 0)),
                                         pl.BlockSpec((1, d), lambda b, i: (0, 0))],
        out_specs=tok(d),
        out_shape=jax.ShapeDtypeStruct((bsz, t, d), F32),
        compiler_params=_params(2),
        name="merge_branches",
    )(*branches, gate, w_branch, w_out, x, mod_gate, g.astype(F32).reshape(1, d))


def _ffn_kernel(x_ref, g_in, sh_ref, sc_ref, wgu_ref, wo_ref, g_out, mg_ref, o_ref, *, chunk):
    x = x_ref[...]
    a = (_rms(x, g_in[...]) * (1.0 + sc_ref[...]) + sh_ref[...]).astype(MXU_DTYPE)
    hidden = wo_ref.shape[0]
    acc = None
    for c0 in range(0, hidden, chunk):
        gch = jnp.dot(a, wgu_ref[:, c0:c0 + chunk], preferred_element_type=F32)
        uch = jnp.dot(a, wgu_ref[:, hidden + c0:hidden + c0 + chunk], preferred_element_type=F32)
        act = (_silu(gch) * uch).astype(MXU_DTYPE)
        part = jnp.dot(act, wo_ref[c0:c0 + chunk, :], preferred_element_type=F32)
        acc = part if acc is None else acc + part
    o_ref[...] = x + mg_ref[...] * _rms(acc, g_out[...])


def _ffn(x, g_in, shift, scale, w_gate_up, w_out, g_out, mod_gate, tm=512):
    bsz, t, d = x.shape
    tm = min(tm, t)
    hidden = w_out.shape[0]
    chunk = hidden // 2
    tok = pl.BlockSpec((None, tm, d), lambda b, i: (b, i, 0))
    per_b = pl.BlockSpec((None, 1, d), lambda b, i: (b, 0, 0))
    vec = pl.BlockSpec((1, d), lambda b, i: (0, 0))
    return pl.pallas_call(
        functools.partial(_ffn_kernel, chunk=chunk),
        grid=(bsz, t // tm),
        in_specs=[tok, vec, per_b, per_b, _WHOLE_VMEM, _WHOLE_VMEM, vec, per_b],
        out_specs=tok,
        out_shape=jax.ShapeDtypeStruct((bsz, t, d), F32),
        compiler_params=_params(2),
        name="swiglu_ffn",
    )(x, g_in.astype(F32).reshape(1, d), shift, scale, w_gate_up, w_out,
      g_out.astype(F32).reshape(1, d), mod_gate)


def _rot_cols(w, half):
    return jnp.concatenate([-w[:, half:], w[:, :half]], axis=1)


def _rot_heads(w, heads, dim):
    k = w.shape[0]
    w3 = w.reshape(k, heads, dim)
    return jnp.concatenate([-w3[:, :, dim // 2:], w3[:, :, :dim // 2]], axis=2).reshape(k, heads * dim)


def _dup_heads(w, heads, dim):
    k = w.shape[0]
    w3 = w.reshape(k, heads, 1, dim)
    return jnp.broadcast_to(w3, (k, heads, 2, dim)).reshape(k, heads * 2 * dim)


def _rope_tables(rows, dim):
    row = jnp.repeat(jnp.arange(rows, dtype=F32), GRID_W)
    col = jnp.tile(jnp.arange(GRID_W, dtype=F32), rows)
    n_freq = dim // 4
    inv = ROPE_BASE ** (-jnp.arange(n_freq, dtype=F32) / n_freq)
    ang = jnp.concatenate([row[:, None] * inv, col[:, None] * inv], axis=-1)
    return jnp.cos(ang), jnp.sin(ang)


def _mla_tables(t, rope):
    ones = jnp.ones((t, MLA_NOPE), F32)
    zeros_n = jnp.zeros((t, MLA_NOPE), F32)
    pad = jnp.zeros((t, LANES - MLA_NOPE - MLA_ROPE), F32)
    if rope is None:
        c = jnp.ones((t, MLA_ROPE), F32)
        s = jnp.zeros((t, MLA_ROPE), F32)
    else:
        c = jnp.concatenate([rope[0], rope[0]], axis=1)
        s = jnp.concatenate([rope[1], rope[1]], axis=1)
    plain_q = jnp.concatenate([ones, c, pad], axis=1)
    rot = jnp.concatenate([zeros_n, s, pad], axis=1)
    plain_k = jnp.concatenate([zeros_n, c, pad], axis=1)
    return plain_q, rot, plain_k


def _swa_tables(t, rope):
    if rope is None:
        return jnp.ones((t, LANES), F32), jnp.zeros((t, LANES), F32)
    c = jnp.concatenate([rope[0]] * 4, axis=1)
    s = jnp.concatenate([rope[1]] * 4, axis=1)
    return c, s


def _layer_weights(w_in, mla_w_uq, mla_w_ukv):
    dm = w_in.shape[0]
    sizes = (384, 256, MLA_ROPE, SSD_INNER, SSD_INNER + 2 * SSD_GROUPS * SSD_STATE, 2 * SSD_HEADS,
             SWA_HEADS * SWA_HEAD_DIM, 2 * SWA_KV_HEADS * SWA_HEAD_DIM, 3 * HY_WIDTH, N_BRANCH * dm)
    offs = [0]
    for s in sizes:
        offs.append(offs[-1] + s)
    seg = lambda i: w_in[:, offs[i]:offs[i + 1]]
    cast = lambda w: w.astype(MXU_DTYPE)
    pad_kr = lambda w: jnp.pad(w, ((0, 0), (MLA_NOPE, LANES - MLA_NOPE - MLA_ROPE)))
    w_kr = seg(2)
    kvw = SWA_KV_HEADS * SWA_HEAD_DIM
    w_swk, w_swv = seg(7)[:, :kvw], seg(7)[:, kvw:]
    w = {
        "qa": cast(seg(0)), "ckv": cast(seg(1)),
        "krp": cast(pad_kr(w_kr)), "krr": cast(pad_kr(_rot_cols(w_kr, MLA_ROPE // 2))),
        "z": cast(seg(3)), "xbc": cast(seg(4)), "dt": cast(seg(5)),
        "swq": cast(seg(6)), "swq_rot": cast(_rot_heads(seg(6), SWA_HEADS, SWA_HEAD_DIM)),
        "swk": cast(_dup_heads(w_swk, SWA_KV_HEADS, SWA_HEAD_DIM)),
        "swk_rot": cast(_dup_heads(_rot_heads(w_swk, SWA_KV_HEADS, SWA_HEAD_DIM), SWA_KV_HEADS, SWA_HEAD_DIM)),
        "swv": cast(w_swv),
        "hy": cast(seg(8)), "gate": cast(seg(9)),
    }
    kq = mla_w_uq.shape[0]
    dq = MLA_NOPE + MLA_ROPE
    uq = mla_w_uq.reshape(kq, MLA_HEADS, dq)
    padq = ((0, 0), (0, 0), (0, LANES - dq))
    w["uq"] = cast(jnp.pad(uq, padq).reshape(kq, MLA_HEADS * LANES))
    uq_rope = uq[:, :, MLA_NOPE:]
    uq_rot = jnp.concatenate([-uq_rope[:, :, MLA_ROPE // 2:], uq_rope[:, :, :MLA_ROPE // 2]], axis=2)
    uq_rot = jnp.pad(uq_rot, ((0, 0), (0, 0), (MLA_NOPE, LANES - dq)))
    w["uq_rot"] = cast(uq_rot.reshape(kq, MLA_HEADS * LANES))
    kk = mla_w_ukv.shape[0]
    ukv = mla_w_ukv.reshape(kk, MLA_HEADS, MLA_NOPE + MLA_V)
    w["uk"] = cast(jnp.pad(ukv[:, :, :MLA_NOPE], ((0, 0), (0, 0), (0, LANES - MLA_NOPE))).reshape(kk, MLA_HEADS * LANES))
    w["uv"] = cast(ukv[:, :, MLA_NOPE:].reshape(kk, MLA_HEADS * MLA_V))
    return w


def _token_mixers(x, xc, mod_l, mod_c, ctx_out, rope_mla, rope_swa, norm_g0, w, p, dft_l, dft_c):
    bsz, t, dm = x.shape
    s_c = xc.shape[1]
    bf = MXU_DTYPE

    lat_names = ["qa", "ckv", "krp", "krr", "z", "xbc", "swq", "swq_rot", "swk", "swk_rot", "swv", "hy", "gate", "dt"]
    lat_dt = [F32 if k == "dt" else bf for k in lat_names]
    lat = dict(zip(lat_names, _norm_matmul(x, norm_g0, [w[k] for k in lat_names], lat_dt,
                                           shift=mod_l[0], scale=mod_l[1], name="in_proj_latent")))
    ctx_names = ["ckv", "krp", "xbc", "swk", "swv"] + (["qa", "z", "swq", "hy", "gate"] if ctx_out else []) + ["dt"]
    ctx_dt = [F32 if k == "dt" else bf for k in ctx_names]
    ctx = dict(zip(ctx_names, _norm_matmul(xc, norm_g0, [w[k] for k in ctx_names], ctx_dt,
                                           shift=mod_c[0], scale=mod_c[1], name="in_proj_context")))

    knp_l, v_l = _norm_matmul(lat["ckv"], p["mla_kv_norm"], [w["uk"], w["uv"]], [bf, bf], name="mla_kv_up")
    knp_c, v_c = _norm_matmul(ctx["ckv"], p["mla_kv_norm"], [w["uk"], w["uv"]], [bf, bf], name="mla_kv_up_ctx")
    qp, qr = _norm_matmul(lat["qa"], p["mla_q_norm"], [w["uq"], w["uq_rot"]], [bf, bf], name="mla_q_up")
    cq, sq, ck = _mla_tables(t, rope_mla)
    mla_q_scale = MLA_SCALE * LOG2E
    swa_q_scale = SWA_SCALE * LOG2E
    mla_l = _mla_attention(qp, qr, cq * mla_q_scale, sq * mla_q_scale, knp_l, lat["krp"], lat["krr"], ck, sq, v_l,
                           ctx=(knp_c, ctx["krp"], v_c))

    xbc = _dwconv([lat["xbc"], ctx["xbc"]], p["ssd_conv_w"], p["ssd_conv_b"], act=True, tc=384, name="ssd_conv")
    dt_raw = jnp.concatenate([lat["dt"], ctx["dt"]], axis=1)
    yf, yb = _ssd_scan(xbc, dt_raw, jnp.swapaxes(dt_raw, 1, 2), p["ssd_dt_bias"], p["ssd_a_log"], s_c // SSD_CHUNK)
    ssd_l = _ssd_output(yf, yb, xbc, lat["z"], p["ssd_d"], p["ssd_norm"], 0, tm=1024)

    csw, ssw = _swa_tables(t, rope_swa)
    swa_l = _swa_attention(p["swa_sink"], lat["swq"], lat["swq_rot"], csw * swa_q_scale, ssw * swa_q_scale,
                           ctx["swk"], ctx["swv"], lat=(lat["swk"], lat["swk_rot"], csw, ssw, lat["swv"]))

    hy_args = (p["hy_w1"], p["hy_b1"], p["hy_w2"], p["hy_b2"], p["hy_w3"], p["hy_freq"])
    spec_l = _hyena_filters(t, *hy_args, dft_l[0], dft_l[1])
    hy_l = _hyena(lat["hy"], p["hy_conv_w"], p["hy_conv_b"], spec_l, p["hy_d"], dft_l)

    w_branch = p["w_branch"].astype(bf)
    w_out = p["w_out"].astype(bf)
    x_new = _merge([mla_l, ssd_l, swa_l, hy_l], lat["gate"], w_branch, w_out, x, mod_l[2], p["norm_g1"])
    if not ctx_out:
        return x_new, None

    qp_c, = _norm_matmul(ctx["qa"], p["mla_q_norm"], [w["uq"]], [bf], name="mla_q_up_ctx")
    cq_c, sq_c, ck_c = _mla_tables(s_c, None)
    mla_c = _mla_attention(qp_c, qp_c, cq_c * mla_q_scale, sq_c, knp_c, ctx["krp"], ctx["krp"], ck_c, sq_c, v_c)
    ssd_c = _ssd_output(yf, yb, xbc, ctx["z"], p["ssd_d"], p["ssd_norm"], t)
    c1, s0 = _swa_tables(s_c, None)
    swa_c = _swa_attention(p["swa_sink"], ctx["swq"], ctx["swq"], c1 * swa_q_scale, s0, ctx["swk"], ctx["swv"])
    spec_c = _hyena_filters(s_c, *hy_args, dft_c[0], dft_c[1])
    hy_c = _hyena(ctx["hy"], p["hy_conv_w"], p["hy_conv_b"], spec_c, p["hy_d"], dft_c)
    xc_new = _merge([mla_c, ssd_c, swa_c, hy_c], ctx["gate"], w_branch, w_out, xc, mod_c[2], p["norm_g1"])
    return x_new, xc_new


def kernel(x, c, ctx, c_ctx, ada_w, ada_b, norm_g, w_in, mla_q_norm, mla_w_uq, mla_kv_norm, mla_w_ukv, ssd_conv_w, ssd_conv_b, ssd_dt_bias, ssd_a_log, ssd_d, ssd_norm, swa_sink, hy_conv_w, hy_conv_b, hy_w1, hy_b1, hy_w2, hy_b2, hy_w3, hy_freq, hy_d, w_branch, w_out, ffn_w_in, ffn_w_out):
    bsz, t, dm = x.shape
    s_c = ctx.shape[1]
    depth = ada_w.shape[0]
    rows = t // GRID_W
    rope_mla = _rope_tables(rows, MLA_ROPE)
    rope_swa = _rope_tables(rows, SWA_HEAD_DIM)
    dft_l = _dft_tables(t)
    dft_c = _dft_tables(s_c)
    cond_rows = 16
    cond = jnp.concatenate([c, c_ctx[None, :], jnp.zeros((cond_rows - bsz - 1, dm), F32)], axis=0)
    w_in, mla_w_uq, mla_w_ukv, w_branch, w_out, ffn_w_in, ffn_w_out = (
        a.astype(MXU_DTYPE) for a in (w_in, mla_w_uq, mla_w_ukv, w_branch, w_out, ffn_w_in, ffn_w_out))
    xc = ctx
    for l in range(depth):
        ctx_out = l < depth - 1
        mod = _ada(cond, ada_w[l], ada_b[l])
        mod_l = [mod[:bsz, k * dm:(k + 1) * dm].reshape(bsz, 1, dm) for k in range(6)]
        mod_c = [jnp.broadcast_to(mod[bsz, k * dm:(k + 1) * dm].reshape(1, 1, dm), (bsz, 1, dm)) for k in range(6)]
        w = _layer_weights(w_in[l], mla_w_uq[l], mla_w_ukv[l])
        p = {"mla_q_norm": mla_q_norm[l], "mla_kv_norm": mla_kv_norm[l], "ssd_conv_w": ssd_conv_w[l],
             "ssd_conv_b": ssd_conv_b[l], "ssd_dt_bias": ssd_dt_bias[l], "ssd_a_log": ssd_a_log[l],
             "ssd_d": ssd_d[l], "ssd_norm": ssd_norm[l], "swa_sink": swa_sink[l], "hy_conv_w": hy_conv_w[l],
             "hy_conv_b": hy_conv_b[l], "hy_w1": hy_w1[l], "hy_b1": hy_b1[l], "hy_w2": hy_w2[l],
             "hy_b2": hy_b2[l], "hy_w3": hy_w3[l], "hy_freq": hy_freq[l], "hy_d": hy_d[l],
             "w_branch": w_branch[l], "w_out": w_out[l], "norm_g1": norm_g[l, 1]}
        x, xc_new = _token_mixers(x, xc, mod_l, mod_c, ctx_out, rope_mla, rope_swa, norm_g[l, 0], w, p, dft_l, dft_c)
        wgu = ffn_w_in[l].astype(MXU_DTYPE)
        wo = ffn_w_out[l].astype(MXU_DTYPE)
        x = _ffn(x, norm_g[l, 2], mod_l[3], mod_l[4], wgu, wo, norm_g[l, 3], mod_l[5])
        if ctx_out:
            xc = _ffn(xc_new, norm_g[l, 2], mod_c[3], mod_c[4], wgu, wo, norm_g[l, 3], mod_c[5])
    return x
```

```python
import functools
import math

import jax
import jax.numpy as jnp
from jax import lax
from jax.experimental import pallas as pl
from jax.experimental.pallas import tpu as pltpu

F32 = jnp.float32
MXU_DTYPE = jnp.bfloat16
HIGHEST = lax.Precision.HIGHEST

GRID_W = 64
EPS = 1e-6
ROPE_BASE = 10000.0
MLA_HEADS, MLA_NOPE, MLA_ROPE, MLA_V = 8, 64, 32, 64
MLA_SCALE = (MLA_NOPE + MLA_ROPE) ** -0.5
SSD_HEADS, SSD_HEADDIM, SSD_GROUPS, SSD_STATE, SSD_CHUNK = 8, 64, 2, 64, 128
SSD_INNER = SSD_HEADS * SSD_HEADDIM
SWA_HEADS, SWA_KV_HEADS, SWA_HEAD_DIM, SWA_WINDOW, SWA_BLOCK = 8, 2, 64, 128, 128
SWA_SCALE = SWA_HEAD_DIM ** -0.5
HY_WIDTH, HY_ORDER, HY_BANDS, HY_HIDDEN = 512, 2, 16, 64
HY_DECAY_TARGET, HY_FAST_DECAY, HY_SLOW_DECAY = 1e-2, 0.3, 1.5
N_BRANCH = 4
LANES = 128
SSD_BATCH_ROWS = 8
VT_ROWS = 80
LOG2E = math.log2(math.e)
DFT_SPLIT = 8
CONV_EDGE = 16
MLA_KEY_BLOCK = 256

VMEM_LIMIT = 56 * 1024 * 1024
_WHOLE_VMEM = pl.BlockSpec(memory_space=pltpu.VMEM)


def _params(n_grid, vmem=VMEM_LIMIT):
    return pltpu.CompilerParams(dimension_semantics=("arbitrary",) * n_grid, vmem_limit_bytes=vmem)


def _silu(x):
    return x * jax.nn.sigmoid(x)


def _softplus(x):
    return jnp.maximum(x, 0.0) + jnp.log1p(jnp.exp(-jnp.abs(x)))


def _rms(x, g):
    return x * lax.rsqrt(jnp.mean(x * x, axis=-1, keepdims=True) + EPS) * g


def _ada_kernel(s_ref, w_ref, b_ref, o_ref):
    s = _silu(s_ref[...])
    o_ref[...] = jnp.dot(s, w_ref[...], preferred_element_type=F32, precision=HIGHEST) + b_ref[...]


def _ada(cond, w, b, tn=1536):
    m, k = cond.shape
    n = w.shape[1]
    return pl.pallas_call(
        _ada_kernel,
        grid=(n // tn,),
        in_specs=[pl.BlockSpec((m, k), lambda j: (0, 0)),
                  pl.BlockSpec((k, tn), lambda j: (0, j)),
                  pl.BlockSpec((1, tn), lambda j: (0, j))],
        out_specs=pl.BlockSpec((m, tn), lambda j: (0, j)),
        out_shape=jax.ShapeDtypeStruct((m, n), F32),
        compiler_params=_params(1),
        name="ada_ln",
    )(cond, w, b.reshape(1, n))


def _norm_mm_kernel(*refs, n_w, modulate):
    it = iter(refs)
    x_ref, g_ref = next(it), next(it)
    if modulate:
        sh_ref, sc_ref = next(it), next(it)
    w_refs = [next(it) for _ in range(n_w)]
    o_refs = [next(it) for _ in range(n_w)]
    a = _rms(x_ref[...].astype(F32), g_ref[...])
    if modulate:
        a = a * (1.0 + sc_ref[...]) + sh_ref[...]
    a = a.astype(MXU_DTYPE)
    for w_ref, o_ref in zip(w_refs, o_refs):
        n = w_ref.shape[1]
        for c0 in range(0, n, 1024):
            c1 = min(n, c0 + 1024)
            o_ref[:, c0:c1] = jnp.dot(a, w_ref[:, c0:c1], preferred_element_type=F32).astype(o_ref.dtype)


def _norm_matmul(x, g, ws, out_dtypes, shift=None, scale=None, tm=512, name="norm_matmul"):
    bsz, t, k = x.shape
    tm = min(tm, t)
    modulate = shift is not None
    in_specs = [pl.BlockSpec((None, tm, k), lambda b, i: (b, i, 0)),
                pl.BlockSpec((1, k), lambda b, i: (0, 0))]
    args = [x, g.reshape(1, k).astype(F32)]
    if modulate:
        in_specs += [pl.BlockSpec((None, 1, k), lambda b, i: (b, 0, 0))] * 2
        args += [shift, scale]
    in_specs += [_WHOLE_VMEM] * len(ws)
    args += list(ws)
    return pl.pallas_call(
        functools.partial(_norm_mm_kernel, n_w=len(ws), modulate=modulate),
        grid=(bsz, t // tm),
        in_specs=in_specs,
        out_specs=[pl.BlockSpec((None, tm, w.shape[1]), lambda b, i: (b, i, 0)) for w in ws],
        out_shape=[jax.ShapeDtypeStruct((bsz, t, w.shape[1]), dt) for w, dt in zip(ws, out_dtypes)],
        compiler_params=_params(2),
        name=name,
    )(*args)


def _mla_kernel(*refs, t_l, s_c):
    if s_c:
        (qp, qr, cq, sq, knp_l, krp_l, krr_l, ck, sk, v_l, knp_c, krp_c, v_c, o_ref, kcat, vt) = refs
    else:
        (qp, qr, cq, sq, knp_l, krp_l, krr_l, ck, sk, v_l, o_ref, kcat, vt) = refs

    @pl.when(pl.program_id(1) == 0)
    def _():
        kro = (krp_l[...].astype(F32) * ck[...] + krr_l[...].astype(F32) * sk[...]).astype(kcat.dtype)
        for h in range(MLA_HEADS):
            blk = slice(h * LANES, (h + 1) * LANES)
            kcat[h, 0:t_l, :] = knp_l[:, blk] + kro
            if s_c:
                kcat[h, t_l:t_l + s_c, :] = knp_c[:, blk] + krp_c[...]
            vt[h, MLA_V:, :] = jnp.ones((VT_ROWS - MLA_V, t_l + s_c), vt.dtype)
        for j in range(MLA_HEADS // 2):
            blk = slice(j * LANES, (j + 1) * LANES)
            vp = v_l[:, blk].astype(F32).T
            vt[2 * j, 0:MLA_V, 0:t_l] = vp[0:MLA_V].astype(vt.dtype)
            vt[2 * j + 1, 0:MLA_V, 0:t_l] = vp[MLA_V:].astype(vt.dtype)
            if s_c:
                vp = v_c[:, blk].astype(F32).T
                vt[2 * j, 0:MLA_V, t_l:t_l + s_c] = vp[0:MLA_V].astype(vt.dtype)
                vt[2 * j + 1, 0:MLA_V, t_l:t_l + s_c] = vp[MLA_V:].astype(vt.dtype)

    cqv, sqv = cq[...], sq[...]
    tq = qp.shape[0]
    s_tot = t_l + s_c
    nt = (((1,), (1,)), ((), ()))
    heads = range(MLA_HEADS)
    q, m, acc = {}, {}, {}
    for h in heads:
        blk = slice(h * LANES, (h + 1) * LANES)
        q[h] = (qp[:, blk].astype(F32) * cqv + qr[:, blk].astype(F32) * sqv).astype(MXU_DTYPE)
        m[h] = jnp.full((1, tq), -1e30, F32)
        acc[h] = jnp.zeros((VT_ROWS, tq), F32)
    for k0 in range(0, s_tot, MLA_KEY_BLOCK):
        k1 = min(s_tot, k0 + MLA_KEY_BLOCK)
        s = {h: lax.dot_general(kcat[h, k0:k1, :], q[h], nt, preferred_element_type=F32) for h in heads}
        m_new = {h: jnp.maximum(m[h], jnp.max(s[h], axis=0, keepdims=True)) for h in heads}
        p = {h: jnp.exp2(s[h] - m_new[h]).astype(MXU_DTYPE) for h in heads}
        pv = {h: jnp.dot(vt[h, :, k0:k1], p[h], preferred_element_type=F32) for h in heads}
        for h in heads:
            acc[h] = acc[h] * jnp.exp2(m[h] - m_new[h]) + pv[h]
            m[h] = m_new[h]
    for j in range(MLA_HEADS // 2):
        halves = [acc[h][0:MLA_V] * (1.0 / acc[h][MLA_V:MLA_V + 1]) for h in (2 * j, 2 * j + 1)]
        o_ref[:, j * LANES:(j + 1) * LANES] = jnp.concatenate(halves, axis=0).T.astype(o_ref.dtype)


def _mla_attention(qp, qr, cq, sq, knp_l, krp_l, krr_l, ck, sk, v_l, ctx=None, tq=512):
    bsz, t_q, _ = qp.shape
    t_l = knp_l.shape[1]
    s_c = 0 if ctx is None else ctx[0].shape[1]
    tq = min(tq, t_q)
    hw = MLA_HEADS * LANES
    vw = MLA_HEADS * MLA_V
    per_b = lambda rows, cols: pl.BlockSpec((None, rows, cols), lambda b, i: (b, 0, 0))
    in_specs = [pl.BlockSpec((None, tq, hw), lambda b, i: (b, i, 0)),
                pl.BlockSpec((None, tq, hw), lambda b, i: (b, i, 0)),
                pl.BlockSpec((tq, LANES), lambda b, i: (i, 0)),
                pl.BlockSpec((tq, LANES), lambda b, i: (i, 0)),
                per_b(t_l, hw), per_b(t_l, LANES), per_b(t_l, LANES),
                pl.BlockSpec((t_l, LANES), lambda b, i: (0, 0)),
                pl.BlockSpec((t_l, LANES), lambda b, i: (0, 0)),
                per_b(t_l, vw)]
    args = [qp, qr, cq, sq, knp_l, krp_l, krr_l, ck, sk, v_l]
    if s_c:
        in_specs += [per_b(s_c, hw), per_b(s_c, LANES), per_b(s_c, vw)]
        args += list(ctx)
    return pl.pallas_call(
        functools.partial(_mla_kernel, t_l=t_l, s_c=s_c),
        grid=(bsz, t_q // tq),
        in_specs=in_specs,
        out_specs=pl.BlockSpec((None, tq, vw), lambda b, i: (b, i, 0)),
        out_shape=jax.ShapeDtypeStruct((bsz, t_q, vw), MXU_DTYPE),
        scratch_shapes=[pltpu.VMEM((MLA_HEADS, t_l + s_c, LANES), MXU_DTYPE),
                        pltpu.VMEM((MLA_HEADS, VT_ROWS, t_l + s_c), MXU_DTYPE)],
        compiler_params=_params(2),
        name="mla_attention",
    )(*args)


def _swa_kernel(*refs, band, t_k):
    if band:
        (sink, q, qrot, cq, sq, k, krot, ck, sk, v, kc, vc, o_ref, kro, vt, vct) = refs
    else:
        (sink, q, qrot, cq, sq, kc, vc, o_ref, vct) = refs
    i = pl.program_id(1)
    tq = q.shape[0]
    gw = 2 * SWA_HEAD_DIM
    dh = SWA_HEAD_DIM
    kb_rows = min(tq + 2 * SWA_BLOCK, t_k)

    @pl.when(i == 0)
    def _():
        def put_vt(dst, src):
            vtr = src[...].astype(F32).T
            for g in range(SWA_KV_HEADS):
                dst[g, 0:dh, :] = vtr[g * dh:(g + 1) * dh].astype(dst.dtype)
                dst[g, dh:, :] = jnp.ones((VT_ROWS - dh, src.shape[0]), dst.dtype)

        put_vt(vct, vc)
        if band:
            put_vt(vt, v)
            ckv, skv = ck[...], sk[...]
            for g in range(SWA_KV_HEADS):
                blk = slice(g * gw, (g + 1) * gw)
                kro[:, blk] = (k[:, blk].astype(F32) * ckv + krot[:, blk].astype(F32) * skv).astype(kro.dtype)

    if band:
        start = pl.multiple_of(jnp.clip(i * tq - SWA_BLOCK, 0, t_k - kb_rows), SWA_BLOCK)
        k_pos = start + lax.broadcasted_iota(jnp.int32, (kb_rows, tq), 0)
        q_pos = i * tq + lax.broadcasted_iota(jnp.int32, (kb_rows, tq), 1)
        in_band = jnp.abs(q_pos - k_pos) <= SWA_WINDOW

    lane_lo = lax.broadcasted_iota(jnp.int32, (tq, LANES), 1) < dh
    cqv, sqv = cq[...], sq[...]
    nt = (((1,), (1,)), ((), ()))
    heads = range(SWA_HEADS)
    group = {h: h // (SWA_HEADS // SWA_KV_HEADS) for h in heads}
    gblk = {h: slice(group[h] * gw, (group[h] + 1) * gw) for h in heads}
    qm, snk = {}, {}
    for j in range(SWA_HEADS // 2):
        blk = slice(j * LANES, (j + 1) * LANES)
        qro = q[:, blk].astype(F32) * cqv + qrot[:, blk].astype(F32) * sqv
        qm[2 * j] = jnp.where(lane_lo, qro, 0.0).astype(MXU_DTYPE)
        qm[2 * j + 1] = jnp.where(lane_lo, 0.0, qro).astype(MXU_DTYPE)
    for h in heads:
        snk[h] = sink[h] * LOG2E
    s_c = {h: lax.dot_general(kc[:, gblk[h]], qm[h], nt, preferred_element_type=F32) for h in heads}
    m = {h: jnp.maximum(jnp.max(s_c[h], axis=0, keepdims=True), snk[h]) for h in heads}
    if band:
        s_b = {h: jnp.where(in_band, lax.dot_general(kro[pl.ds(start, kb_rows), gblk[h]], qm[h], nt,
                                                     preferred_element_type=F32), -1e30) for h in heads}
        m = {h: jnp.maximum(m[h], jnp.max(s_b[h], axis=0, keepdims=True)) for h in heads}
    o = {h: jnp.dot(vct[group[h]], jnp.exp2(s_c[h] - m[h]).astype(MXU_DTYPE), preferred_element_type=F32)
         for h in heads}
    if band:
        o = {h: o[h] + jnp.dot(vt[group[h], :, pl.ds(start, kb_rows)], jnp.exp2(s_b[h] - m[h]).astype(MXU_DTYPE),
                               preferred_element_type=F32) for h in heads}
    for j in range(SWA_HEADS // 2):
        halves = []
        for h in (2 * j, 2 * j + 1):
            l = o[h][dh:dh + 1] + jnp.exp2(snk[h] - m[h])
            halves.append(o[h][0:dh] * (1.0 / l))
        o_ref[:, j * LANES:(j + 1) * LANES] = jnp.concatenate(halves, axis=0).T.astype(o_ref.dtype)


def _swa_attention(sink, q, qrot, cq, sq, kc, vc, lat=None):
    bsz, t, hw = q.shape
    s_c = kc.shape[1]
    tq = min(t, 2 * SWA_BLOCK)
    kw, vw = kc.shape[2], vc.shape[2]
    band = lat is not None
    per_b = lambda rows, cols: pl.BlockSpec((None, rows, cols), lambda b, i: (b, 0, 0))
    qspec = pl.BlockSpec((None, tq, hw), lambda b, i: (b, i, 0))
    tspec = pl.BlockSpec((tq, LANES), lambda b, i: (i, 0))
    in_specs = [pl.BlockSpec(memory_space=pltpu.SMEM), qspec, qspec, tspec, tspec]
    args = [sink.astype(F32), q, qrot, cq, sq]
    scratch = []
    if band:
        k, krot, ck, sk, v = lat
        full_t = pl.BlockSpec((t, LANES), lambda b, i: (0, 0))
        in_specs += [per_b(t, kw), per_b(t, kw), full_t, full_t, per_b(t, vw)]
        args += [k, krot, ck, sk, v]
        scratch = [pltpu.VMEM((t, kw), MXU_DTYPE), pltpu.VMEM((SWA_KV_HEADS, VT_ROWS, t), MXU_DTYPE)]
    scratch += [pltpu.VMEM((SWA_KV_HEADS, VT_ROWS, s_c), MXU_DTYPE)]
    in_specs += [per_b(s_c, kw), per_b(s_c, vw)]
    args += [kc, vc]
    return pl.pallas_call(
        functools.partial(_swa_kernel, band=band, t_k=t),
        grid=(bsz, t // tq),
        in_specs=in_specs,
        out_specs=pl.BlockSpec((None, tq, hw), lambda b, i: (b, i, 0)),
        out_shape=jax.ShapeDtypeStruct((bsz, t, hw), MXU_DTYPE),
        scratch_shapes=scratch,
        compiler_params=_params(2),
        name="swa_attention" if band else "ctx_sink_attention",
    )(*args)


def _dwconv_kernel(*refs, n_seg, act):
    x_refs = refs[:n_seg]
    w_ref, b_ref, o_ref = refs[n_seg:]
    taps = w_ref.shape[0]
    off = 0
    for x_ref in x_refs:
        t = x_ref.shape[0]
        x = x_ref[...].astype(F32)
        e = CONV_EDGE
        row = lax.broadcasted_iota(jnp.int32, (e, x.shape[1]), 0)
        acc = jnp.zeros_like(x) + b_ref[...]
        top = jnp.zeros((e, x.shape[1]), F32) + b_ref[...]
        bot = top
        for kk in range(taps):
            d = kk - taps // 2
            wk = w_ref[kk:kk + 1, :]
            xs = x if d == 0 else pltpu.roll(x, (-d) % t, axis=0)
            acc = acc + xs * wk
            top = top + jnp.where(row + d >= 0, xs[0:e], 0.0) * wk
            bot = bot + jnp.where(row + d < e, xs[t - e:t], 0.0) * wk
        if act:
            acc, top, bot = _silu(acc), _silu(top), _silu(bot)
        o_ref[off:off + t, :] = acc.astype(o_ref.dtype)
        o_ref[off:off + e, :] = top.astype(o_ref.dtype)
        o_ref[off + t - e:off + t, :] = bot.astype(o_ref.dtype)
        off += t


def _dwconv(xs, w, b, act, tc=256, name="dwconv"):
    bsz, _, c = xs[0].shape
    t_tot = sum(x.shape[1] for x in xs)
    taps = w.shape[0]
    in_specs = [pl.BlockSpec((None, x.shape[1], tc), lambda bb, j: (bb, 0, j)) for x in xs]
    in_specs += [pl.BlockSpec((taps, tc), lambda bb, j: (0, j)), pl.BlockSpec((1, tc), lambda bb, j: (0, j))]
    return pl.pallas_call(
        functools.partial(_dwconv_kernel, n_seg=len(xs), act=act),
        grid=(bsz, c // tc),
        in_specs=in_specs,
        out_specs=pl.BlockSpec((None, t_tot, tc), lambda bb, j: (bb, 0, j)),
        out_shape=jax.ShapeDtypeStruct((bsz, t_tot, c), MXU_DTYPE),
        compiler_params=_params(2),
        name=name,
    )(*xs, w.astype(F32), b.reshape(1, c).astype(F32))


def _ssd_scan_kernel(xs_f, bm_f, cm_f, dt_f, dtt_f, xs_b, bm_b, cm_b, dt_b, dtt_b,
                     b_row, al_row, b_col, al_col, yf_ref, yb_ref, st_ref):
    @pl.when(pl.program_id(1) == 0)
    def _():
        st_ref[...] = jnp.zeros_like(st_ref)

    q = SSD_CHUNK
    row = lax.broadcasted_iota(jnp.int32, (q, q), 0)
    col = lax.broadcasted_iota(jnp.int32, (q, q), 1)
    lane_lo = col < SSD_HEADDIM
    a_row = -jnp.exp(al_row[...])
    a_col = -jnp.exp(al_col[...])
    nt = (((1,), (1,)), ((), ()))
    tn = (((0,), (0,)), ((), ()))
    dirs = ((xs_f, bm_f, cm_f, dt_f, dtt_f, yf_ref), (xs_b, bm_b, cm_b, dt_b, dtt_b, yb_ref))
    units = [(bi, d) for bi in range(st_ref.shape[0]) for d in range(2)]
    pairs_per_group = SSD_HEADS // SSD_GROUPS // 2
    feeds = {0: row >= col, 1: row <= col}
    last = {0: q - 1, 1: 0}
    dt, dtt, cs, cst, cg, bg, gmat, sc, ecol, wcol, dec = ({} for _ in range(11))
    for u in units:
        bi, d = u
        dt_ref, dtt_ref = dirs[d][3], dirs[d][4]
        dt[u] = _softplus(dt_ref[bi] + b_row[...])
        dtt[u] = _softplus(dtt_ref[bi] + b_col[...])
    for u in units:
        d = u[1]
        cs[u] = jnp.dot(feeds[d].astype(F32), dt[u] * a_row, preferred_element_type=F32, precision=HIGHEST)
        cst[u] = jnp.dot(dtt[u] * a_col, feeds[1 - d].astype(F32), preferred_element_type=F32, precision=HIGHEST)
    for u in units:
        bi, d = u
        bblk, cblk = dirs[d][1][bi].astype(F32), dirs[d][2][bi].astype(F32)
        for g in range(SSD_GROUPS):
            gmask = (col // SSD_STATE) == g
            cg[u, g] = jnp.where(gmask, cblk, 0.0).astype(MXU_DTYPE)
            bg[u, g] = jnp.where(gmask, bblk, 0.0)
            gmat[u, g] = lax.dot_general(cg[u, g], bg[u, g].astype(MXU_DTYPE), nt,
                                         preferred_element_type=F32)
    for u in units:
        d = u[1]
        tot = cs[u][last[d]:last[d] + 1, :]
        e_all = jnp.exp(cs[u])
        w_all = dt[u] * jnp.exp(tot - cs[u])
        d_all = jnp.exp(tot)
        for h in range(SSD_HEADS):
            c = d * SSD_HEADS + h
            ccol = cs[u][:, c:c + 1]
            crow = cst[u][c:c + 1, :]
            lmat = jnp.where(feeds[d], jnp.exp(ccol - crow), 0.0)
            sc[u, h] = (gmat[u, h // (SSD_HEADS // SSD_GROUPS)] * lmat * dtt[u][c:c + 1, :]).astype(MXU_DTYPE)
            ecol[u, h] = e_all[:, c:c + 1]
            wcol[u, h] = w_all[:, c:c + 1]
            dec[u, h] = d_all[:, c:c + 1]
    for u in units:
        bi, d = u
        xs_ref, y_ref = dirs[d][0], dirs[d][5]
        for j in range(SSD_HEADS // 2):
            g = j // pairs_per_group
            xb = xs_ref[bi, :, j * LANES:(j + 1) * LANES]
            xblk = xb.astype(F32)
            s_in = st_ref[bi, d, j]
            y_inter = jnp.dot(cg[u, g], s_in.astype(MXU_DTYPE), preferred_element_type=F32)
            ys, news = [], []
            for hh in range(2):
                h = 2 * j + hh
                ys.append(jnp.dot(sc[u, h], xb, preferred_element_type=F32) + ecol[u, h] * y_inter)
                bw = (bg[u, g] * wcol[u, h]).astype(MXU_DTYPE)
                xh = jnp.where(lane_lo if hh == 0 else ~lane_lo, xblk, 0.0).astype(MXU_DTYPE)
                news.append(lax.dot_general(bw, xh, tn, preferred_element_type=F32))
            y_ref[bi, :, j * LANES:(j + 1) * LANES] = jnp.where(lane_lo, ys[0], ys[1]).astype(y_ref.dtype)
            st_ref[bi, d, j] = (s_in * jnp.where(lane_lo, dec[u, 2 * j], dec[u, 2 * j + 1])
                                + news[0] + news[1])


def _ssd_scan(xbc, dt_raw, dt_raw_t, dt_bias, a_log, nc_ctx):
    bsz, t_c, _ = xbc.shape
    q = SSD_CHUNK
    nc = t_c // q
    nh2 = 2 * SSD_HEADS

    def fwd(s):
        return jnp.where(s < nc_ctx, nc - nc_ctx + s, s - nc_ctx)

    def bwd(s):
        return nc - 1 - s

    bt = math.gcd(bsz, SSD_BATCH_ROWS)

    def specs(order):
        return [pl.BlockSpec((bt, q, SSD_INNER), lambda b, s: (b, order(s), 0)),
                pl.BlockSpec((bt, q, LANES), lambda b, s: (b, order(s), SSD_INNER // LANES)),
                pl.BlockSpec((bt, q, LANES), lambda b, s: (b, order(s), SSD_INNER // LANES + 1)),
                pl.BlockSpec((bt, q, nh2), lambda b, s: (b, order(s), 0)),
                pl.BlockSpec((bt, nh2, q), lambda b, s: (b, 0, order(s)))]

    small = lambda r, c: pl.BlockSpec((r, c), lambda b, s: (0, 0))
    bias = dt_bias.astype(F32).reshape(1, nh2)
    alog = a_log.astype(F32).reshape(1, nh2)
    return pl.pallas_call(
        _ssd_scan_kernel,
        grid=(bsz // bt, nc),
        in_specs=specs(fwd) + specs(bwd) + [small(1, nh2), small(1, nh2), small(nh2, 1), small(nh2, 1)],
        out_specs=[pl.BlockSpec((bt, q, SSD_INNER), lambda b, s: (b, fwd(s), 0)),
                   pl.BlockSpec((bt, q, SSD_INNER), lambda b, s: (b, bwd(s), 0))],
        out_shape=[jax.ShapeDtypeStruct((bsz, t_c, SSD_INNER), MXU_DTYPE)] * 2,
        scratch_shapes=[pltpu.VMEM((bt, 2, SSD_HEADS // 2, LANES, LANES), F32)],
        compiler_params=_params(2),
        name="ssd_scan",
    )(xbc, xbc, xbc, dt_raw, dt_raw_t, xbc, xbc, xbc, dt_raw, dt_raw_t,
      bias, alog, bias.reshape(nh2, 1), alog.reshape(nh2, 1))


def _ssd_out_kernel(yf, yb, xs, z, dexp, ng, o_ref):
    y = yf[...].astype(F32) + yb[...].astype(F32) + dexp[...] * xs[...].astype(F32)
    yg = y * _silu(z[...].astype(F32))
    gw = SSD_INNER // SSD_GROUPS
    for g in range(SSD_GROUPS):
        blk = slice(g * gw, (g + 1) * gw)
        o_ref[:, blk] = _rms(yg[:, blk], ng[:, blk]).astype(o_ref.dtype)


def _ssd_output(yf, yb, xbc, z, d_skip, norm_g, row0, tm=256):
    bsz, t, _ = z.shape
    tm = min(tm, t)
    off = row0 // tm
    sp_y = pl.BlockSpec((None, tm, SSD_INNER), lambda b, i: (b, i + off, 0))
    vec = pl.BlockSpec((1, SSD_INNER), lambda b, i: (0, 0))
    dexp = jnp.repeat(d_skip.astype(F32), SSD_HEADDIM).reshape(1, SSD_INNER)
    return pl.pallas_call(
        _ssd_out_kernel,
        grid=(bsz, t // tm),
        in_specs=[sp_y, sp_y, sp_y, pl.BlockSpec((None, tm, SSD_INNER), lambda b, i: (b, i, 0)), vec, vec],
        out_specs=pl.BlockSpec((None, tm, SSD_INNER), lambda b, i: (b, i, 0)),
        out_shape=jax.ShapeDtypeStruct((bsz, t, SSD_INNER), MXU_DTYPE),
        compiler_params=_params(2),
        name="ssd_output",
    )(yf, yb, xbc, z, dexp, norm_g.astype(F32).reshape(1, SSD_INNER))


def _hy_mlp_kernel(feat_ref, w1, b1, w2, b2, w3, fr, delta, h_ref, cs_ref):
    feat = feat_ref[...]
    h = jnp.sin(fr[...] * (jnp.dot(feat, w1[...], preferred_element_type=F32, precision=HIGHEST) + b1[...]))
    h = jnp.sin(fr[...] * (jnp.dot(h, w2[...], preferred_element_type=F32, precision=HIGHEST) + b2[...]))
    h = jnp.dot(h, w3[...], preferred_element_type=F32, precision=HIGHEST)
    h = h * jnp.exp(-feat[:, 0:1] * delta[...])
    h_ref[...] = h
    s = jnp.sum(jnp.abs(h), axis=0, keepdims=True)

    @pl.when(pl.program_id(0) == 0)
    def _():
        cs_ref[...] = s

    @pl.when(pl.program_id(0) != 0)
    def _():
        cs_ref[...] += s


def _hy_spectrum_kernel(h0_ref, h1_ref, c0_ref, c1_ref, cm_ref, sf_ref, ar_ref, ai_ref, br_ref):
    n = h0_ref.shape[0]
    inv = 1.0 / (c0_ref[...] + c1_ref[...])
    row = lax.broadcasted_iota(jnp.int32, h0_ref.shape, 0)
    first = row == 0
    h0 = h0_ref[...] * inv
    h1 = jnp.where(first, 0.0, h1_ref[...] * inv)
    a = h0 + h1
    kr = jnp.dot(cm_ref[...], a.astype(MXU_DTYPE), preferred_element_type=F32)
    kq = jnp.dot(sf_ref[...], (h1 - h0).astype(MXU_DTYPE), preferred_element_type=F32)
    k_nyq = jnp.sum(jnp.where(row % 2 == 0, a, -a), axis=0, keepdims=True)
    inv_n = 1.0 / (2 * n)
    ar = kr * jnp.where(first, inv_n, 2.0 * inv_n)
    ar_ref[...] = ar
    ai_ref[...] = jnp.where(first, 0.0, kq * (2.0 * inv_n))
    br_ref[...] = jnp.where(first, k_nyq * inv_n, ar)


def _hyena_filters(n, w1, b1, w2, b2, w3, freq, cm, sf):
    t = jnp.arange(n, dtype=F32)
    tnorm = t / n
    bands = jnp.linspace(1e-4, HY_BANDS - 1, HY_BANDS, dtype=F32)
    ang = 2 * math.pi * t[:, None] * bands[None, :] / n
    feat = jnp.concatenate([tnorm[:, None], jnp.cos(ang), -jnp.sin(ang)], axis=-1)
    emb = feat.shape[1]
    feat = jnp.pad(feat, ((0, 0), (0, HY_HIDDEN - emb)))
    w1p = jnp.pad(w1.astype(F32), ((0, HY_HIDDEN - emb), (0, 0)))
    deltas = jnp.abs(jnp.linspace(math.log(HY_DECAY_TARGET) / HY_SLOW_DECAY,
                                  math.log(HY_DECAY_TARGET) / HY_FAST_DECAY, HY_WIDTH, dtype=F32))
    ncol = HY_ORDER * 2 * HY_WIDTH
    delta_row = jnp.tile(deltas, HY_ORDER * 2).reshape(1, ncol)
    tt = min(n, 256)
    small = lambda r, c: pl.BlockSpec((r, c), lambda i: (0, 0))
    row = lambda v: v.astype(F32).reshape(1, -1)
    h, colsum = pl.pallas_call(
        _hy_mlp_kernel,
        grid=(n // tt,),
        in_specs=[pl.BlockSpec((tt, HY_HIDDEN), lambda i: (i, 0)),
                  small(HY_HIDDEN, HY_HIDDEN), small(1, HY_HIDDEN),
                  small(HY_HIDDEN, HY_HIDDEN), small(1, HY_HIDDEN),
                  small(HY_HIDDEN, ncol), small(1, HY_HIDDEN), small(1, ncol)],
        out_specs=[pl.BlockSpec((tt, ncol), lambda i: (i, 0)), small(1, ncol)],
        out_shape=[jax.ShapeDtypeStruct((n, ncol), F32), jax.ShapeDtypeStruct((1, ncol), F32)],
        compiler_params=_params(1),
        name="hyena_filter_mlp",
    )(feat, w1p, row(b1), w2.astype(F32), row(b2), w3.astype(F32), row(freq), delta_row)

    tc = 256
    per_o = HY_WIDTH // tc
    side0 = lambda jc: (jc // per_o) * 2 * per_o + jc % per_o
    side1 = lambda jc: (jc // per_o) * 2 * per_o + per_o + jc % per_o
    nout = HY_ORDER * HY_WIDTH
    out_spec = pl.BlockSpec((n, tc), lambda jc: (0, jc))
    return pl.pallas_call(
        _hy_spectrum_kernel,
        grid=(nout // tc,),
        in_specs=[pl.BlockSpec((n, tc), lambda jc: (0, side0(jc))),
                  pl.BlockSpec((n, tc), lambda jc: (0, side1(jc))),
                  pl.BlockSpec((1, tc), lambda jc: (0, side0(jc))),
                  pl.BlockSpec((1, tc), lambda jc: (0, side1(jc))),
                  _WHOLE_VMEM, _WHOLE_VMEM],
        out_specs=[out_spec] * 3,
        out_shape=[jax.ShapeDtypeStruct((n, nout), F32)] * 3,
        compiler_params=_params(1),
        name="hyena_filter_spectrum",
    )(h, h, colsum, colsum, cm, sf)


def _hy_fwd_kernel(u_ref, cm_ref, sf_ref, ar_ref, ai_ref, br_ref, yr_ref, yi_ref):
    u = u_ref[...].astype(MXU_DTYPE)
    p = jnp.dot(cm_ref[...], u, preferred_element_type=F32)
    q = jnp.dot(sf_ref[...], u, preferred_element_type=F32)
    ai = ai_ref[...]
    yr_ref[...] = (p * ar_ref[...] + q * ai).astype(yr_ref.dtype)
    yi_ref[...] = (q * br_ref[...] - p * ai).astype(yi_ref.dtype)


def _hy_inv_kernel(yr_ref, yi_ref, cm_ref, si_ref, u_ref, xg_ref, d_ref, o_ref):
    y = (jnp.dot(cm_ref[...], yr_ref[...], preferred_element_type=F32)
         + jnp.dot(si_ref[...], yi_ref[...], preferred_element_type=F32))
    u = u_ref[...].astype(F32)
    o_ref[...] = (xg_ref[...].astype(F32) * (y + u * d_ref[...])).astype(o_ref.dtype)


def _hyena_conv(u, u_col0, xg, xg_col0, spectra, order, d, tables, out_dtype, tc=256):
    cm, sf, si = tables
    ar, ai, br = spectra
    bsz, n, _ = u.shape
    nct = HY_WIDTH // tc
    ucol, gcol, scol = u_col0 // tc, xg_col0 // tc, order * nct
    tok = lambda c0: pl.BlockSpec((None, n, tc), lambda c, b: (b, 0, c0 + c))
    spec_sp = pl.BlockSpec((n, tc), lambda c, b: (0, scol + c))
    mid = pl.BlockSpec((None, n, tc), lambda c, b: (b, 0, c))
    yr, yi = pl.pallas_call(
        _hy_fwd_kernel,
        grid=(nct, bsz),
        in_specs=[tok(ucol), _WHOLE_VMEM, _WHOLE_VMEM, spec_sp, spec_sp, spec_sp],
        out_specs=[mid, mid],
        out_shape=[jax.ShapeDtypeStruct((bsz, n, HY_WIDTH), MXU_DTYPE)] * 2,
        compiler_params=_params(2),
        name="hyena_dft_forward",
    )(u, cm, sf, ar, ai, br)
    return pl.pallas_call(
        _hy_inv_kernel,
        grid=(nct, bsz),
        in_specs=[mid, mid, _WHOLE_VMEM, _WHOLE_VMEM, tok(ucol), tok(gcol),
                  pl.BlockSpec((1, tc), lambda c, b: (0, c))],
        out_specs=mid,
        out_shape=jax.ShapeDtypeStruct((bsz, n, HY_WIDTH), out_dtype),
        compiler_params=_params(2),
        name="hyena_dft_inverse",
    )(yr, yi, cm, si, u, xg, d.astype(F32).reshape(1, HY_WIDTH))


def _dft_tables(n):
    idx = jnp.arange(n, dtype=jnp.int32)
    alt = jnp.where(idx % 2 == 0, 1.0, -1.0).astype(F32)
    def rows(freqs):
        ang = ((freqs[:, None] * idx[None, :]) % (2 * n)).astype(F32) * (math.pi / n)
        return jnp.cos(ang), jnp.sin(ang)
    c_hi, s_hi = rows(idx[::DFT_SPLIT])
    c_lo, s_lo = rows(idx[:DFT_SPLIT])
    cm = (c_hi[:, None, :] * c_lo[None, :, :] - s_hi[:, None, :] * s_lo[None, :, :]).reshape(n, n)
    sm = (s_hi[:, None, :] * c_lo[None, :, :] + c_hi[:, None, :] * s_lo[None, :, :]).reshape(n, n)
    sf = jnp.where(idx[:, None] == 0, alt[None, :], sm)
    si = jnp.where(idx[None, :] == 0, alt[:, None], sm)
    return cm.astype(MXU_DTYPE), sf.astype(MXU_DTYPE), si.astype(MXU_DTYPE)


def _hyena(hy, conv_w, conv_b, spectra, d, tables):
    u3 = _dwconv([hy], conv_w, conv_b, act=False, tc=512, name="hyena_short_conv")
    z = _hyena_conv(u3, 0, u3, HY_WIDTH, spectra, 0, d[0], tables, MXU_DTYPE)
    return _hyena_conv(z, 0, u3, 2 * HY_WIDTH, spectra, 1, d[1], tables, MXU_DTYPE)


def _merge_kernel(b0, b1, b2, b3, gate_ref, wb_ref, wo_ref, x_ref, mg_ref, g_ref, o_ref):
    d = x_ref.shape[1]
    acc = None
    for i, br in enumerate((b0, b1, b2, b3)):
        proj = jnp.dot(br[...], wb_ref[i], preferred_element_type=F32)
        term = jax.nn.sigmoid(gate_ref[:, i * d:(i + 1) * d].astype(F32)) * proj
        acc = term if acc is None else acc + term
    y = jnp.dot(acc.astype(MXU_DTYPE), wo_ref[...], preferred_element_type=F32)
    o_ref[...] = x_ref[...] + mg_ref[...] * _rms(y, g_ref[...])


def _merge(branches, gate, w_branch, w_out, x, mod_gate, g, tm=512):
    bsz, t, d = x.shape
    tm = min(tm, t)
    bw = branches[0].shape[2]
    tok = lambda w: pl.BlockSpec((None, tm, w), lambda b, i: (b, i, 0))
    return pl.pallas_call(
        _merge_kernel,
        grid=(bsz, t // tm),
        in_specs=[tok(bw)] * N_BRANCH + [tok(N_BRANCH * d), _WHOLE_VMEM, _WHOLE_VMEM, tok(d),
                                         pl.BlockSpec((None, 1, d), lambda b, i: (b, 0, 0)),
                                         pl.BlockSpec((1, d), lambda b, i: (0, 0))],
        out_specs=tok(d),
        out_shape=jax.ShapeDtypeStruct((bsz, t, d), F32),
        compiler_params=_params(2),
        name="merge_branches",
    )(*branches, gate, w_branch, w_out, x, mod_gate, g.astype(F32).reshape(1, d))


def _ffn_kernel(x_ref, g_in, sh_ref, sc_ref, wgu_ref, wo_ref, g_out, mg_ref, o_ref, *, chunk):
    x = x_ref[...]
    a = (_rms(x, g_in[...]) * (1.0 + sc_ref[...]) + sh_ref[...]).astype(MXU_DTYPE)
    hidden = wo_ref.shape[0]
    acc = None
    for c0 in range(0, hidden, chunk):
        gch = jnp.dot(a, wgu_ref[:, c0:c0 + chunk], preferred_element_type=F32)
        uch = jnp.dot(a, wgu_ref[:, hidden + c0:hidden + c0 + chunk], preferred_element_type=F32)
        act = (_silu(gch) * uch).astype(MXU_DTYPE)
        part = jnp.dot(act, wo_ref[c0:c0 + chunk, :], preferred_element_type=F32)
        acc = part if acc is None else acc + part
    o_ref[...] = x + mg_ref[...] * _rms(acc, g_out[...])


def _ffn(x, g_in, shift, scale, w_gate_up, w_out, g_out, mod_gate, tm=512):
    bsz, t, d = x.shape
    tm = min(tm, t)
    hidden = w_out.shape[0]
    chunk = hidden // 2
    tok = pl.BlockSpec((None, tm, d), lambda b, i: (b, i, 0))
    per_b = pl.BlockSpec((None, 1, d), lambda b, i: (b, 0, 0))
    vec = pl.BlockSpec((1, d), lambda b, i: (0, 0))
    return pl.pallas_call(
        functools.partial(_ffn_kernel, chunk=chunk),
        grid=(bsz, t // tm),
        in_specs=[tok, vec, per_b, per_b, _WHOLE_VMEM, _WHOLE_VMEM, vec, per_b],
        out_specs=tok,
        out_shape=jax.ShapeDtypeStruct((bsz, t, d), F32),
        compiler_params=_params(2),
        name="swiglu_ffn",
    )(x, g_in.astype(F32).reshape(1, d), shift, scale, w_gate_up, w_out,
      g_out.astype(F32).reshape(1, d), mod_gate)


def _rot_cols(w, half):
    return jnp.concatenate([-w[:, half:], w[:, :half]], axis=1)


def _rot_heads(w, heads, dim):
    k = w.shape[0]
    w3 = w.reshape(k, heads, dim)
    return jnp.concatenate([-w3[:, :, dim // 2:], w3[:, :, :dim // 2]], axis=2).reshape(k, heads * dim)


def _dup_heads(w, heads, dim):
    k = w.shape[0]
    w3 = w.reshape(k, heads, 1, dim)
    return jnp.broadcast_to(w3, (k, heads, 2, dim)).reshape(k, heads * 2 * dim)


def _rope_tables(rows, dim):
    row = jnp.repeat(jnp.arange(rows, dtype=F32), GRID_W)
    col = jnp.tile(jnp.arange(GRID_W, dtype=F32), rows)
    n_freq = dim // 4
    inv = ROPE_BASE ** (-jnp.arange(n_freq, dtype=F32) / n_freq)
    ang = jnp.concatenate([row[:, None] * inv, col[:, None] * inv], axis=-1)
    return jnp.cos(ang), jnp.sin(ang)


def _mla_tables(t, rope):
    ones = jnp.ones((t, MLA_NOPE), F32)
    zeros_n = jnp.zeros((t, MLA_NOPE), F32)
    pad = jnp.zeros((t, LANES - MLA_NOPE - MLA_ROPE), F32)
    if rope is None:
        c = jnp.ones((t, MLA_ROPE), F32)
        s = jnp.zeros((t, MLA_ROPE), F32)
    else:
        c = jnp.concatenate([rope[0], rope[0]], axis=1)
        s = jnp.concatenate([rope[1], rope[1]], axis=1)
    plain_q = jnp.concatenate([ones, c, pad], axis=1)
    rot = jnp.concatenate([zeros_n, s, pad], axis=1)
    plain_k = jnp.concatenate([zeros_n, c, pad], axis=1)
    return plain_q, rot, plain_k


def _swa_tables(t, rope):
    if rope is None:
        return jnp.ones((t, LANES), F32), jnp.zeros((t, LANES), F32)
    c = jnp.concatenate([rope[0]] * 4, axis=1)
    s = jnp.concatenate([rope[1]] * 4, axis=1)
    return c, s


def _layer_weights(w_in, mla_w_uq, mla_w_ukv):
    dm = w_in.shape[0]
    sizes = (384, 256, MLA_ROPE, SSD_INNER, SSD_INNER + 2 * SSD_GROUPS * SSD_STATE, 2 * SSD_HEADS,
             SWA_HEADS * SWA_HEAD_DIM, 2 * SWA_KV_HEADS * SWA_HEAD_DIM, 3 * HY_WIDTH, N_BRANCH * dm)
    offs = [0]
    for s in sizes:
        offs.append(offs[-1] + s)
    seg = lambda i: w_in[:, offs[i]:offs[i + 1]]
    cast = lambda w: w.astype(MXU_DTYPE)
    pad_kr = lambda w: jnp.pad(w, ((0, 0), (MLA_NOPE, LANES - MLA_NOPE - MLA_ROPE)))
    w_kr = seg(2)
    kvw = SWA_KV_HEADS * SWA_HEAD_DIM
    w_swk, w_swv = seg(7)[:, :kvw], seg(7)[:, kvw:]
    w = {
        "qa": cast(seg(0)), "ckv": cast(seg(1)),
        "krp": cast(pad_kr(w_kr)), "krr": cast(pad_kr(_rot_cols(w_kr, MLA_ROPE // 2))),
        "z": cast(seg(3)), "xbc": cast(seg(4)), "dt": cast(seg(5)),
        "swq": cast(seg(6)), "swq_rot": cast(_rot_heads(seg(6), SWA_HEADS, SWA_HEAD_DIM)),
        "swk": cast(_dup_heads(w_swk, SWA_KV_HEADS, SWA_HEAD_DIM)),
        "swk_rot": cast(_dup_heads(_rot_heads(w_swk, SWA_KV_HEADS, SWA_HEAD_DIM), SWA_KV_HEADS, SWA_HEAD_DIM)),
        "swv": cast(w_swv),
        "hy": cast(seg(8)), "gate": cast(seg(9)),
    }
    kq = mla_w_uq.shape[0]
    dq = MLA_NOPE + MLA_ROPE
    uq = mla_w_uq.reshape(kq, MLA_HEADS, dq)
    padq = ((0, 0), (0, 0), (0, LANES - dq))
    w["uq"] = cast(jnp.pad(uq, padq).reshape(kq, MLA_HEADS * LANES))
    uq_rope = uq[:, :, MLA_NOPE:]
    uq_rot = jnp.concatenate([-uq_rope[:, :, MLA_ROPE // 2:], uq_rope[:, :, :MLA_ROPE // 2]], axis=2)
    uq_rot = jnp.pad(uq_rot, ((0, 0), (0, 0), (MLA_NOPE, LANES - dq)))
    w["uq_rot"] = cast(uq_rot.reshape(kq, MLA_HEADS * LANES))
    kk = mla_w_ukv.shape[0]
    ukv = mla_w_ukv.reshape(kk, MLA_HEADS, MLA_NOPE + MLA_V)
    w["uk"] = cast(jnp.pad(ukv[:, :, :MLA_NOPE], ((0, 0), (0, 0), (0, LANES - MLA_NOPE))).reshape(kk, MLA_HEADS * LANES))
    w["uv"] = cast(ukv[:, :, MLA_NOPE:].reshape(kk, MLA_HEADS * MLA_V))
    return w


def _token_mixers(x, xc, mod_l, mod_c, ctx_out, rope_mla, rope_swa, norm_g0, w, p, dft_l, dft_c):
    bsz, t, dm = x.shape
    s_c = xc.shape[1]
    bf = MXU_DTYPE

    lat_names = ["qa", "ckv", "krp", "krr", "z", "xbc", "swq", "swq_rot", "swk", "swk_rot", "swv", "hy", "gate", "dt"]
    lat_dt = [F32 if k == "dt" else bf for k in lat_names]
    lat = dict(zip(lat_names, _norm_matmul(x, norm_g0, [w[k] for k in lat_names], lat_dt,
                                           shift=mod_l[0], scale=mod_l[1], name="in_proj_latent")))
    ctx_names = ["ckv", "krp", "xbc", "swk", "swv"] + (["qa", "z", "swq", "hy", "gate"] if ctx_out else []) + ["dt"]
    ctx_dt = [F32 if k == "dt" else bf for k in ctx_names]
    ctx = dict(zip(ctx_names, _norm_matmul(xc, norm_g0, [w[k] for k in ctx_names], ctx_dt,
                                           shift=mod_c[0], scale=mod_c[1], name="in_proj_context")))

    knp_l, v_l = _norm_matmul(lat["ckv"], p["mla_kv_norm"], [w["uk"], w["uv"]], [bf, bf], name="mla_kv_up")
    knp_c, v_c = _norm_matmul(ctx["ckv"], p["mla_kv_norm"], [w["uk"], w["uv"]], [bf, bf], name="mla_kv_up_ctx")
    qp, qr = _norm_matmul(lat["qa"], p["mla_q_norm"], [w["uq"], w["uq_rot"]], [bf, bf], name="mla_q_up")
    cq, sq, ck = _mla_tables(t, rope_mla)
    mla_q_scale = MLA_SCALE * LOG2E
    swa_q_scale = SWA_SCALE * LOG2E
    mla_l = _mla_attention(qp, qr, cq * mla_q_scale, sq * mla_q_scale, knp_l, lat["krp"], lat["krr"], ck, sq, v_l,
                           ctx=(knp_c, ctx["krp"], v_c))

    xbc = _dwconv([lat["xbc"], ctx["xbc"]], p["ssd_conv_w"], p["ssd_conv_b"], act=True, tc=384, name="ssd_conv")
    dt_raw = jnp.concatenate([lat["dt"], ctx["dt"]], axis=1)
    yf, yb = _ssd_scan(xbc, dt_raw, jnp.swapaxes(dt_raw, 1, 2), p["ssd_dt_bias"], p["ssd_a_log"], s_c // SSD_CHUNK)
    ssd_l = _ssd_output(yf, yb, xbc, lat["z"], p["ssd_d"], p["ssd_norm"], 0, tm=1024)

    csw, ssw = _swa_tables(t, rope_swa)
    swa_l = _swa_attention(p["swa_sink"], lat["swq"], lat["swq_rot"], csw * swa_q_scale, ssw * swa_q_scale,
                           ctx["swk"], ctx["swv"], lat=(lat["swk"], lat["swk_rot"], csw, ssw, lat["swv"]))

    hy_args = (p["hy_w1"], p["hy_b1"], p["hy_w2"], p["hy_b2"], p["hy_w3"], p["hy_freq"])
    spec_l = _hyena_filters(t, *hy_args, dft_l[0], dft_l[1])
    hy_l = _hyena(lat["hy"], p["hy_conv_w"], p["hy_conv_b"], spec_l, p["hy_d"], dft_l)

    w_branch = p["w_branch"].astype(bf)
    w_out = p["w_out"].astype(bf)
    x_new = _merge([mla_l, ssd_l, swa_l, hy_l], lat["gate"], w_branch, w_out, x, mod_l[2], p["norm_g1"])
    if not ctx_out:
        return x_new, None

    qp_c, = _norm_matmul(ctx["qa"], p["mla_q_norm"], [w["uq"]], [bf], name="mla_q_up_ctx")
    cq_c, sq_c, ck_c = _mla_tables(s_c, None)
    mla_c = _mla_attention(qp_c, qp_c, cq_c * mla_q_scale, sq_c, knp_c, ctx["krp"], ctx["krp"], ck_c, sq_c, v_c)
    ssd_c = _ssd_output(yf, yb, xbc, ctx["z"], p["ssd_d"], p["ssd_norm"], t)
    c1, s0 = _swa_tables(s_c, None)
    swa_c = _swa_attention(p["swa_sink"], ctx["swq"], ctx["swq"], c1 * swa_q_scale, s0, ctx["swk"], ctx["swv"])
    spec_c = _hyena_filters(s_c, *hy_args, dft_c[0], dft_c[1])
    hy_c = _hyena(ctx["hy"], p["hy_conv_w"], p["hy_conv_b"], spec_c, p["hy_d"], dft_c)
    xc_new = _merge([mla_c, ssd_c, swa_c, hy_c], ctx["gate"], w_branch, w_out, xc, mod_c[2], p["norm_g1"])
    return x_new, xc_new


def kernel(x, c, ctx, c_ctx, ada_w, ada_b, norm_g, w_in, mla_q_norm, mla_w_uq, mla_kv_norm, mla_w_ukv, ssd_conv_w, ssd_conv_b, ssd_dt_bias, ssd_a_log, ssd_d, ssd_norm, swa_sink, hy_conv_w, hy_conv_b, hy_w1, hy_b1, hy_w2, hy_b2, hy_w3, hy_freq, hy_d, w_branch, w_out, ffn_w_in, ffn_w_out):
    bsz, t, dm = x.shape
    s_c = ctx.shape[1]
    depth = ada_w.shape[0]
    rows = t // GRID_W
    rope_mla = _rope_tables(rows, MLA_ROPE)
    rope_swa = _rope_tables(rows, SWA_HEAD_DIM)
    dft_l = _dft_tables(t)
    dft_c = _dft_tables(s_c)
    cond_rows = 16
    cond = jnp.concatenate([c, c_ctx[None, :], jnp.zeros((cond_rows - bsz - 1, dm), F32)], axis=0)
    w_in, mla_w_uq, mla_w_ukv, w_branch, w_out, ffn_w_in, ffn_w_out = (
        a.astype(MXU_DTYPE) for a in (w_in, mla_w_uq, mla_w_ukv, w_branch, w_out, ffn_w_in, ffn_w_out))
    xc = ctx
    for l in range(depth):
        ctx_out = l < depth - 1
        mod = _ada(cond, ada_w[l], ada_b[l])
        mod_l = [mod[:bsz, k * dm:(k + 1) * dm].reshape(bsz, 1, dm) for k in range(6)]
        mod_c = [jnp.broadcast_to(mod[bsz, k * dm:(k + 1) * dm].reshape(1, 1, dm), (bsz, 1, dm)) for k in range(6)]
        w = _layer_weights(w_in[l], mla_w_uq[l], mla_w_ukv[l])
        p = {"mla_q_norm": mla_q_norm[l], "mla_kv_norm": mla_kv_norm[l], "ssd_conv_w": ssd_conv_w[l],
             "ssd_conv_b": ssd_conv_b[l], "ssd_dt_bias": ssd_dt_bias[l], "ssd_a_log": ssd_a_log[l],
             "ssd_d": ssd_d[l], "ssd_norm": ssd_norm[l], "swa_sink": swa_sink[l], "hy_conv_w": hy_conv_w[l],
             "hy_conv_b": hy_conv_b[l], "hy_w1": hy_w1[l], "hy_b1": hy_b1[l], "hy_w2": hy_w2[l],
             "hy_b2": hy_b2[l], "hy_w3": hy_w3[l], "hy_freq": hy_freq[l], "hy_d": hy_d[l],
             "w_branch": w_branch[l], "w_out": w_out[l], "norm_g1": norm_g[l, 1]}
        x, xc_new = _token_mixers(x, xc, mod_l, mod_c, ctx_out, rope_mla, rope_swa, norm_g[l, 0], w, p, dft_l, dft_c)
        wgu = ffn_w_in[l].astype(MXU_DTYPE)
        wo = ffn_w_out[l].astype(MXU_DTYPE)
        x = _ffn(x, norm_g[l, 2], mod_l[3], mod_l[4], wgu, wo, norm_g[l, 3], mod_l[5])
        if ctx_out:
            xc = _ffn(xc_new, norm_g[l, 2], mod_c[3], mod_c[4], wgu, wo, norm_g[l, 3], mod_c[5])
    return x
```
